```python
import math
import jax, jax.numpy as jnp
from jax import lax
import numpy as np

D_MODEL = 2048
BATCH = 2
SEQ = 4096
DEPTH = 4
DEC_BATCH = 8
DEC_SEQ = 8
PAST_LEN = 16384
PAGE_SIZE = 128

CONV_C = 1024
CONV_K = 31
N_HEADS = 8
DIFF_DH = 64
KD = 2 * DIFF_DH
ATT_W = N_HEADS * KD
NUM_BUCKETS = 32
MAX_DISTANCE = 128
Q_BLOCK = 128
NEG_INF = -1e30
SSD_DI = 1024
SSD_P = 64
SSD_H = SSD_DI // SSD_P
SSD_N = 128
SSD_G = 2
SSD_E = SSD_H // SSD_G
SSD_K = 4
SSD_XBC = SSD_DI + 2 * SSD_G * SSD_N
SSD_CHUNK = 128
D_FF = 5632
FFN_K = 3
N_BRANCH = 3
SPLIT_SIZES = (2 * CONV_C, ATT_W, ATT_W, ATT_W, SSD_DI, SSD_XBC, SSD_H, N_BRANCH * D_MODEL)
IN_COLS = 2 * CONV_C + 3 * ATT_W + SSD_DI + SSD_XBC + SSD_H + N_BRANCH * D_MODEL
N_PAGES = PAST_LEN // PAGE_SIZE
POOL_PAGES = (5 * DEC_BATCH * N_PAGES + 3) // 4
EPS = 1e-6

kernel_name = 'hybrid_gated_conformer_diffattn_ssd_decoder_step'


def rms_norm(x, g, eps=EPS):
    xf = x.astype(jnp.float32)
    y = xf * lax.rsqrt(jnp.mean(xf * xf, axis=-1, keepdims=True) + eps)
    return (y * g.astype(jnp.float32)).astype(x.dtype)


def group_rms_norm(y, g):
    b, l, d = y.shape
    yf = y.astype(jnp.float32).reshape(b, l, SSD_G, d // SSD_G)
    yf = yf * lax.rsqrt(jnp.mean(yf * yf, axis=-1, keepdims=True) + EPS)
    return (yf.reshape(b, l, d) * g.astype(jnp.float32)).astype(y.dtype)


def layer_norm(x, g, b, eps=1e-5):
    xf = x.astype(jnp.float32)
    mu = jnp.mean(xf, axis=-1, keepdims=True)
    var = jnp.mean(jnp.square(xf - mu), axis=-1, keepdims=True)
    y = (xf - mu) * lax.rsqrt(var + eps) * g.astype(jnp.float32) + b.astype(jnp.float32)
    return y.astype(x.dtype)


def causal_dwconv(u, prev, w, b):
    k, c = w.shape
    full = jnp.concatenate([prev.astype(u.dtype), u], axis=1)
    out = lax.conv_general_dilated(full, w[:, None, :].astype(u.dtype), window_strides=(1,),
                                   padding='VALID', dimension_numbers=('NWC', 'WIO', 'NWC'),
                                   feature_group_count=c)
    return out + b.astype(u.dtype), full[:, full.shape[1] - (k - 1):]


def t5_bucket(rel):
    n = jnp.maximum(rel, 0)
    max_exact = NUM_BUCKETS // 2
    nf = jnp.maximum(n, 1).astype(jnp.float32)
    large = max_exact + (jnp.log(nf / max_exact) / math.log(MAX_DISTANCE / max_exact)
                         * (NUM_BUCKETS - max_exact)).astype(jnp.int32)
    large = jnp.minimum(large, NUM_BUCKETS - 1)
    return jnp.where(n < max_exact, n, large)


def diff_attn_block(q, k, v, q_pos, k_pos, bias_table, lam):
    b, tq = q.shape[:2]
    tk = k.shape[1]
    qf = q.astype(jnp.float32).reshape(b, tq, N_HEADS, 2, DIFF_DH)
    kf = k.astype(jnp.float32).reshape(b, tk, N_HEADS, 2, DIFF_DH)
    rel = q_pos[:, None] - k_pos[None, :]
    bias = jnp.transpose(bias_table.astype(jnp.float32)[t5_bucket(rel)], (2, 0, 1))
    s = jnp.einsum('bqhmd,bkhmd->bmhqk', qf, kf) * (DIFF_DH ** -0.5) + bias[None, None]
    s = jnp.where((rel >= 0)[None, None, None], s, NEG_INF)
    p = jax.nn.softmax(s, axis=-1)
    a = p[:, 0] - lam * p[:, 1]
    o = jnp.einsum('bhqk,bkhe->bqhe', a, v.astype(jnp.float32))
    return o.astype(q.dtype)


def prompt_diff_attention(q, k, v, pos, bias_table, lam):
    b, l = q.shape[:2]
    nb = l // Q_BLOCK if l % Q_BLOCK == 0 else 1
    tq = l // nb
    qb = jnp.moveaxis(q.reshape(b, nb, tq, N_HEADS, KD), 1, 0)
    pb = pos.reshape(nb, tq)
    out = lax.map(lambda a: diff_attn_block(a[0], k, v, a[1], pos, bias_table, lam), (qb, pb))
    return jnp.moveaxis(out, 0, 1).reshape(b, l, N_HEADS, KD)


def ssd_scan(x, dt, a, bm, cm, h0):
    f32 = jnp.float32
    b, l = x.shape[:2]
    t = SSD_CHUNK if l % SSD_CHUNK == 0 else l
    nc = l // t
    xdt = (x.astype(f32) * dt[..., None]).reshape(b, nc, t, SSD_G, SSD_E, SSD_P)
    da = (dt * a).reshape(b, nc, t, SSD_G, SSD_E)
    bc = bm.astype(f32).reshape(b, nc, t, SSD_G, SSD_N)
    cc = cm.astype(f32).reshape(b, nc, t, SSD_G, SSD_N)
    cum = jnp.cumsum(da, axis=2)
    causal = jnp.tril(jnp.ones((t, t), bool))[None, None, :, :, None, None]
    seg = cum[:, :, :, None] - cum[:, :, None, :]
    decay = jnp.where(causal, jnp.exp(jnp.where(causal, seg, 0.0)), 0.0)
    cb = jnp.einsum('bctgn,bcsgn->bctsg', cc, bc)
    y_diag = jnp.einsum('bctsg,bctsge,bcsgep->bctgep', cb, decay, xdt)
    to_end = jnp.exp(cum[:, :, -1:] - cum)
    chunk_states = jnp.einsum('bctgn,bctge,bctgep->bcgepn', bc, to_end, xdt)
    chunk_decay = jnp.exp(cum[:, :, -1])

    def step(hc, inp):
        s, d = inp
        return hc * d[..., None, None] + s, hc

    h_init = h0.astype(f32).reshape(b, SSD_G, SSD_E, SSD_P, SSD_N)
    h_t, h_in = lax.scan(step, h_init, (jnp.moveaxis(chunk_states, 1, 0), jnp.moveaxis(chunk_decay, 1, 0)))
    h_in = jnp.moveaxis(h_in, 0, 1)
    y_off = jnp.einsum('bctgn,bcgepn,bctge->bctgep', cc, h_in, jnp.exp(cum))
    y = (y_diag + y_off).reshape(b, l, SSD_H, SSD_P)
    return y, h_t.reshape(b, SSD_H, SSD_P, SSD_N)


def _trunk(x, c, pos0, prev_a, prev_c, prev_h, prev_f, p, cache_k=None, cache_v=None, page_table=None):
    f32 = jnp.float32
    bsz, l, _ = x.shape
    q_pos = pos0 + jnp.arange(l, dtype=jnp.int32)
    cut = np.cumsum(SPLIT_SIZES)[:-1].tolist()
    out_k, out_v, out_a, out_c, out_h, out_f = [], [], [], [], [], []
    for li in range(DEPTH):
        mod = jax.nn.silu(c) @ p['w_ada'][li] + p['b_ada'][li]
        sh1, sc1, g1, sh2, sc2, g2 = jnp.split(mod[:, None, :], 6, axis=-1)
        h = rms_norm(x, p['norm1_g'][li]) * (1 + sc1) + sh1
        glu, q, k, v, z, xbc, dt, gates = jnp.split(h @ p['w_in'][li], cut, axis=-1)
        u = glu[..., :CONV_C] * jax.nn.sigmoid(glu[..., CONV_C:])
        u, st = causal_dwconv(u, prev_a[li], p['conv_a_w'][li], p['conv_a_b'][li])
        out_a.append(st)
        y_a = jax.nn.silu(layer_norm(u, p['ln_a_g'][li], p['ln_a_b'][li])) @ p['w_proj_a'][li]
        qh = q.reshape(bsz, l, N_HEADS, KD)
        kh = k.reshape(bsz, l, N_HEADS, KD)
        vh = v.reshape(bsz, l, N_HEADS, KD)
        out_k.append(kh)
        out_v.append(vh)
        lam_init = 0.8 - 0.6 * math.exp(-0.3 * li)
        lam = (jnp.exp(jnp.sum(p['lambda_q1'][li].astype(f32) * p['lambda_k1'][li].astype(f32)))
               - jnp.exp(jnp.sum(p['lambda_q2'][li].astype(f32) * p['lambda_k2'][li].astype(f32)))
               + lam_init)
        if cache_k is None:
            o = prompt_diff_attention(qh, kh, vh, q_pos, p['t5_bias'], lam)
        else:
            past_k = cache_k[li][page_table].reshape(bsz, -1, N_HEADS, KD)
            past_v = cache_v[li][page_table].reshape(bsz, -1, N_HEADS, KD)
            keys = jnp.concatenate([past_k.astype(kh.dtype), kh], axis=1)
            vals = jnp.concatenate([past_v.astype(vh.dtype), vh], axis=1)
            k_pos = jnp.arange(keys.shape[1], dtype=jnp.int32)
            o = diff_attn_block(qh, keys, vals, q_pos, k_pos, p['t5_bias'], lam)
        o = rms_norm(o, p['subln_g'][li]) * (1.0 - lam_init)
        y_b = o.reshape(bsz, l, ATT_W) @ p['w_proj_b'][li]
        xbc, st = causal_dwconv(xbc, prev_c[li], p['ssd_conv_w'][li], p['ssd_conv_b'][li])
        out_c.append(st)
        xbc = jax.nn.silu(xbc)
        xs = xbc[..., :SSD_DI].reshape(bsz, l, SSD_H, SSD_P)
        bm = xbc[..., SSD_DI:SSD_DI + SSD_G * SSD_N].reshape(bsz, l, SSD_G, SSD_N)
        cm = xbc[..., SSD_DI + SSD_G * SSD_N:].reshape(bsz, l, SSD_G, SSD_N)
        dts = jax.nn.softplus(dt.astype(f32) + p['dt_bias'][li].astype(f32))
        a = -jnp.exp(p['a_log'][li].astype(f32))
        y, h_t = ssd_scan(xs, dts, a, bm, cm, prev_h[li])
        out_h.append(h_t.astype(prev_h.dtype))
        y = (y + p['d_skip'][li].astype(f32)[:, None] * xs.astype(f32)).astype(x.dtype)
        y = group_rms_norm(y.reshape(bsz, l, SSD_DI) * jax.nn.silu(z), p['ssd_norm_g'][li])
        y_c = y @ p['w_proj_c'][li]
        g_a, g_b, g_c = jnp.split(jax.nn.sigmoid(gates), N_BRANCH, axis=-1)
        x = x + g1 * ((g_a * y_a + g_b * y_b + g_c * y_c) @ p['w_out'][li])
        h2 = rms_norm(x, p['norm2_g'][li]) * (1 + sc2) + sh2
        fa, fb = jnp.split(h2 @ p['ffn_up'][li], 2, axis=-1)
        fa, st = causal_dwconv(fa, prev_f[li], p['ffn_conv_w'][li], p['ffn_conv_b'][li])
        out_f.append(st)
        x = x + g2 * ((jax.nn.gelu(fa) * fb) @ p['ffn_down'][li])
    y_out = rms_norm(x, p['final_g'])
    return (y_out, jnp.stack(out_k), jnp.stack(out_v), jnp.stack(out_a), jnp.stack(out_c),
            jnp.stack(out_h), jnp.stack(out_f))


def setup_inputs(seed: int = 0) -> dict:
    key = jax.random.key(seed)
    ks = iter(jax.random.split(key, 48))
    f32 = jnp.float32
    D = D_MODEL

    def nrm(shape, scale=1.0):
        return jax.random.normal(next(ks), shape, f32) * scale

    def gain(shape):
        return 1.0 + nrm(shape, 0.01)

    x_prompt = nrm((BATCH, SEQ, D))
    x_sample = nrm((DEC_BATCH, DEC_SEQ, D))
    cache_k = nrm((DEPTH, POOL_PAGES, PAGE_SIZE, N_HEADS, KD))
    cache_v = nrm((DEPTH, POOL_PAGES, PAGE_SIZE, N_HEADS, KD))
    perm = jax.random.permutation(next(ks), POOL_PAGES)
    page_table = perm[:DEC_BATCH * N_PAGES].reshape(DEC_BATCH, N_PAGES).astype(jnp.int32)
    state_conv_a = nrm((DEPTH, DEC_BATCH, CONV_K - 1, CONV_C), 0.5)
    state_ssd_conv = nrm((DEPTH, DEC_BATCH, SSD_K - 1, SSD_XBC))
    state_ssm = nrm((DEPTH, DEC_BATCH, SSD_H, SSD_P, SSD_N), 0.1)
    state_ffn_conv = nrm((DEPTH, DEC_BATCH, FFN_K - 1, D_FF))
    c_prompt = nrm((BATCH, D))
    c_sample = nrm((DEC_BATCH, D))
    w_ada = nrm((DEPTH, D, 6 * D), 0.5 * D ** -0.5)
    b_ada = nrm((DEPTH, 6 * D), 0.02)
    norm1_g = gain((DEPTH, D))
    w_in = nrm((DEPTH, D, IN_COLS), D ** -0.5)
    conv_a_w = nrm((DEPTH, CONV_K, CONV_C), CONV_K ** -0.5)
    conv_a_b = nrm((DEPTH, CONV_C), 0.02)
    ln_a_g = gain((DEPTH, CONV_C))
    ln_a_b = nrm((DEPTH, CONV_C), 0.02)
    w_proj_a = nrm((DEPTH, CONV_C, D), CONV_C ** -0.5)
    t5_bias = nrm((NUM_BUCKETS, N_HEADS), 0.5)
    lambda_q1 = nrm((DEPTH, DIFF_DH), 0.1)
    lambda_k1 = nrm((DEPTH, DIFF_DH), 0.1)
    lambda_q2 = nrm((DEPTH, DIFF_DH), 0.1)
    lambda_k2 = nrm((DEPTH, DIFF_DH), 0.1)
    subln_g = gain((DEPTH, KD))
    w_proj_b = nrm((DEPTH, ATT_W, D), ATT_W ** -0.5)
    ssd_conv_w = nrm((DEPTH, SSD_K, SSD_XBC), SSD_K ** -0.5)
    ssd_conv_b = nrm((DEPTH, SSD_XBC), 0.02)
    u = jax.random.uniform(next(ks), (DEPTH, SSD_H), f32)
    dt0 = jnp.exp(u * (math.log(0.1) - math.log(0.001)) + math.log(0.001))
    dt_bias = dt0 + jnp.log(-jnp.expm1(-dt0))
    a_log = jnp.log(jax.random.uniform(next(ks), (DEPTH, SSD_H), f32, 1.0, 16.0))
    d_skip = gain((DEPTH, SSD_H))
    ssd_norm_g = gain((DEPTH, SSD_DI))
    w_proj_c = nrm((DEPTH, SSD_DI, D), SSD_DI ** -0.5)
    w_out = nrm((DEPTH, D, D), D ** -0.5)
    norm2_g = gain((DEPTH, D))
    ffn_up = nrm((DEPTH, D, 2 * D_FF), D ** -0.5)
    ffn_conv_w = nrm((DEPTH, FFN_K, D_FF), FFN_K ** -0.5)
    ffn_conv_b = nrm((DEPTH, D_FF), 0.02)
    ffn_down = nrm((DEPTH, D_FF, D), D_FF ** -0.5)
    final_g = gain((D,))
    return {'x_prompt': x_prompt, 'x_sample': x_sample, 'cache_k': cache_k, 'cache_v': cache_v,
            'page_table': page_table, 'state_conv_a': state_conv_a, 'state_ssd_conv': state_ssd_conv,
            'state_ssm': state_ssm, 'state_ffn_conv': state_ffn_conv, 'c_prompt': c_prompt,
            'c_sample': c_sample, 'w_ada': w_ada, 'b_ada': b_ada, 'norm1_g': norm1_g, 'w_in': w_in,
            'conv_a_w': conv_a_w, 'conv_a_b': conv_a_b, 'ln_a_g': ln_a_g, 'ln_a_b': ln_a_b,
            'w_proj_a': w_proj_a, 't5_bias': t5_bias, 'lambda_q1': lambda_q1, 'lambda_k1': lambda_k1,
            'lambda_q2': lambda_q2, 'lambda_k2': lambda_k2, 'subln_g': subln_g, 'w_proj_b': w_proj_b,
            'ssd_conv_w': ssd_conv_w, 'ssd_conv_b': ssd_conv_b, 'dt_bias': dt_bias, 'a_log': a_log,
            'd_skip': d_skip, 'ssd_norm_g': ssd_norm_g, 'w_proj_c': w_proj_c, 'w_out': w_out,
            'norm2_g': norm2_g, 'ffn_up': ffn_up, 'ffn_conv_w': ffn_conv_w, 'ffn_conv_b': ffn_conv_b,
            'ffn_down': ffn_down, 'final_g': final_g}


def reference(x_prompt, x_sample, cache_k, cache_v, page_table, state_conv_a, state_ssd_conv,
              state_ssm, state_ffn_conv, c_prompt, c_sample, w_ada, b_ada, norm1_g, w_in,
              conv_a_w, conv_a_b, ln_a_g, ln_a_b, w_proj_a, t5_bias, lambda_q1, lambda_k1,
              lambda_q2, lambda_k2, subln_g, w_proj_b, ssd_conv_w, ssd_conv_b, dt_bias, a_log,
              d_skip, ssd_norm_g, w_proj_c, w_out, norm2_g, ffn_up, ffn_conv_w, ffn_conv_b,
              ffn_down, final_g):
    p = dict(w_ada=w_ada, b_ada=b_ada, norm1_g=norm1_g, w_in=w_in, conv_a_w=conv_a_w,
             conv_a_b=conv_a_b, ln_a_g=ln_a_g, ln_a_b=ln_a_b, w_proj_a=w_proj_a, t5_bias=t5_bias,
             lambda_q1=lambda_q1, lambda_k1=lambda_k1, lambda_q2=lambda_q2, lambda_k2=lambda_k2,
             subln_g=subln_g, w_proj_b=w_proj_b, ssd_conv_w=ssd_conv_w, ssd_conv_b=ssd_conv_b,
             dt_bias=dt_bias, a_log=a_log, d_skip=d_skip, ssd_norm_g=ssd_norm_g, w_proj_c=w_proj_c,
             w_out=w_out, norm2_g=norm2_g, ffn_up=ffn_up, ffn_conv_w=ffn_conv_w,
             ffn_conv_b=ffn_conv_b, ffn_down=ffn_down, final_g=final_g)
    bp = x_prompt.shape[0]
    dtp = x_prompt.dtype
    zero_a = jnp.zeros((DEPTH, bp, CONV_K - 1, CONV_C), dtp)
    zero_c = jnp.zeros((DEPTH, bp, SSD_K - 1, SSD_XBC), dtp)
    zero_h = jnp.zeros((DEPTH, bp, SSD_H, SSD_P, SSD_N), dtp)
    zero_f = jnp.zeros((DEPTH, bp, FFN_K - 1, D_FF), dtp)
    y_p, k_p, v_p, a_p, c_p, h_p, f_p = _trunk(x_prompt, c_prompt, 0, zero_a, zero_c, zero_h, zero_f, p)
    y_s, k_s, v_s, a_s, c_s, h_s, f_s = _trunk(x_sample, c_sample, PAST_LEN, state_conv_a,
                                               state_ssd_conv, state_ssm, state_ffn_conv, p,
                                               cache_k, cache_v, page_table)
    return (y_p, y_s, k_p, v_p, k_s, v_s, a_p, a_s, c_p, c_s, h_p, h_s, f_p, f_s)
```

```python
import functools
import math

import jax
import jax.numpy as jnp
from jax import lax
from jax.experimental import pallas as pl
from jax.experimental.pallas import tpu as pltpu

F32 = jnp.float32
BF16 = jnp.bfloat16

EPS = 1e-6
LN_EPS = 1e-5
NEG_INF = -1e30
DIFF_DH = 64
KD = 2 * DIFF_DH
NUM_BUCKETS = 32
MAX_DISTANCE = 128
SSD_P = 64
SSD_N = 128
SSD_G = 2
SSD_CHUNK = 128
PAGE_SIZE = 128
LANES = 128
VMEM_LIMIT = 56 * 1024 * 1024

ATTN_TQ = 256
PAGES_PER_STEP = 4


def _cparams(sem):
    return pltpu.CompilerParams(dimension_semantics=sem, vmem_limit_bytes=VMEM_LIMIT)


def _pick(n, cands):
    for c in cands:
        if n % c == 0:
            return c
    return n


def _mm_body(*refs, nl, nt, nr, epi):
    lrefs = refs[:nl]
    wrefs = refs[nl:2 * nl]
    trefs = refs[2 * nl:2 * nl + nt]
    rrefs = refs[2 * nl + nt:2 * nl + nt + nr]
    orefs = refs[2 * nl + nt + nr:]
    accs = [jnp.dot(l[...], w[...], preferred_element_type=F32) for l, w in zip(lrefs, wrefs)]
    res = epi(accs, [t[...] for t in trefs], [r[0] for r in rrefs], pl.program_id(1))
    for o, r in zip(orefs, res):
        o[...] = r.astype(o.dtype)


def _mm(lhs, rhs, epi, out_dtypes, n, *, tiles=(), rows=(), tm, tn, tpb=1):
    m = lhs[0].shape[0]
    assert m % tm == 0 and n % tn == 0
    in_specs, args = [], []
    for a in lhs:
        in_specs.append(pl.BlockSpec((tm, a.shape[1]), lambda i, j: (i, 0)))
        args.append(a)
    for a, off in rhs:
        in_specs.append(pl.BlockSpec((a.shape[0], tn), lambda i, j, off=off: (0, j + off)))
        args.append(a)
    for a, off in tiles:
        in_specs.append(pl.BlockSpec((tm, tn), lambda i, j, off=off: (i, j + off)))
        args.append(a)
    for a, off in rows:
        in_specs.append(pl.BlockSpec((1, a.shape[1], tn),
                                     lambda i, j, off=off: (i // tpb, 0, j + off)))
        args.append(a)
    out_specs = [pl.BlockSpec((tm, tn), lambda i, j: (i, j)) for _ in out_dtypes]
    out_shape = [jax.ShapeDtypeStruct((m, n), d) for d in out_dtypes]
    body = functools.partial(_mm_body, nl=len(lhs), nt=len(tiles), nr=len(rows), epi=epi)
    return pl.pallas_call(
        body, grid=(m // tm, n // tn), in_specs=in_specs, out_specs=out_specs,
        out_shape=out_shape, compiler_params=_cparams(("parallel", "arbitrary")))(*args)


def _ada_body(c_ref, w_ref, b_ref, o_ref):
    c = c_ref[...]
    s = (c * jax.nn.sigmoid(c)).astype(BF16)
    o_ref[0] = jnp.dot(s, w_ref[0].astype(BF16), preferred_element_type=F32) + b_ref[0]


def _ada(c_all, w_ada, b_ada):
    depth, d, n6 = w_ada.shape
    rows = c_all.shape[0]
    tn = _pick(n6, (1024, 512, 256, 128))
    return pl.pallas_call(
        _ada_body, grid=(depth, n6 // tn),
        in_specs=[pl.BlockSpec((rows, d), lambda l, j: (0, 0)),
                  pl.BlockSpec((1, d, tn), lambda l, j: (l, 0, j)),
                  pl.BlockSpec((1, 1, tn), lambda l, j: (l, 0, j))],
        out_specs=pl.BlockSpec((1, rows, tn), lambda l, j: (l, 0, j)),
        out_shape=jax.ShapeDtypeStruct((depth, rows, n6), F32),
        compiler_params=_cparams(("arbitrary", "arbitrary")))(c_all, w_ada, b_ada[:, None, :])


def _norm_mod_body(x_ref, g_ref, sc_ref, sh_ref, o_ref):
    x = x_ref[...]
    y = x * lax.rsqrt(jnp.mean(x * x, axis=-1, keepdims=True) + EPS) * g_ref[...]
    o_ref[...] = (y * (1.0 + sc_ref[0]) + sh_ref[0]).astype(o_ref.dtype)


def _norm_mod(x, g, mod, sc_blk, sh_blk, *, tm, tpb):
    m, d = x.shape
    r = mod.shape[1]
    return pl.pallas_call(
        _norm_mod_body, grid=(m // tm,),
        in_specs=[pl.BlockSpec((tm, d), lambda i: (i, 0)),
                  pl.BlockSpec((1, d), lambda i: (0, 0)),
                  pl.BlockSpec((1, r, d), lambda i: (i // tpb, 0, sc_blk)),
                  pl.BlockSpec((1, r, d), lambda i: (i // tpb, 0, sh_blk))],
        out_specs=pl.BlockSpec((tm, d), lambda i: (i, 0)),
        out_shape=jax.ShapeDtypeStruct((m, d), BF16),
        compiler_params=_cparams(("parallel",)))(x, g, mod, mod)


def _rms_body(x_ref, g_ref, o_ref):
    x = x_ref[...]
    o_ref[...] = x * lax.rsqrt(jnp.mean(x * x, axis=-1, keepdims=True) + EPS) * g_ref[...]


def _rms(x, g, *, tm):
    m, d = x.shape
    return pl.pallas_call(
        _rms_body, grid=(m // tm,),
        in_specs=[pl.BlockSpec((tm, d), lambda i: (i, 0)),
                  pl.BlockSpec((1, d), lambda i: (0, 0))],
        out_specs=pl.BlockSpec((tm, d), lambda i: (i, 0)),
        out_shape=jax.ShapeDtypeStruct((m, d), F32),
        compiler_params=_cparams(("parallel",)))(x, g)


def _dwconv_body(*refs, kc, halo, t, tc, n_seq_tiles, nx, npr, epi):
    u_ref, prev_ref, w_ref, b_ref = refs[:4]
    xrefs = refs[4:4 + nx]
    prefs = refs[4 + nx:4 + nx + npr]
    o_ref = refs[4 + nx + npr]
    win, cv = refs[4 + nx + npr + 1:]

    @pl.when(pl.program_id(2) == 0)
    def _():
        win[0:halo, :] = prev_ref[0]

    win[halo:halo + t, :] = u_ref[0]
    base = halo - (kc - 1)
    rc = min(t, 64)
    cc = min(tc, 256)
    for r0 in range(0, t, rc):
        for c0 in range(0, tc, cc):
            acc = jnp.broadcast_to(b_ref[:, c0:c0 + cc], (rc, cc))
            for r in range(min(8, kc)):
                taps = list(range(r, kc, 8))
                span = rc + 8 * (len(taps) - 1)
                start = base + r0 + r
                sh = win[start:start + span, c0:c0 + cc]
                for ai, j in enumerate(taps):
                    acc = acc + w_ref[j:j + 1, c0:c0 + cc] * sh[8 * ai:8 * ai + rc]
            cv[r0:r0 + rc, c0:c0 + cc] = acc
    o_ref[0] = epi(cv[...], [x[0] for x in xrefs], [p[...] for p in prefs]).astype(o_ref.dtype)
    if n_seq_tiles > 1:
        win[0:halo, :] = win[t:t + halo, :]


def _dwconv(u, u_blk, prev, w, b, epi, out_dtype, *, c, t, tc, halo, xtiles=(), prows=()):
    bsz, l, _ = u.shape
    kc = w.shape[0]
    assert l % t == 0 and c % tc == 0 and halo >= kc - 1 and (t >= halo or l == t)
    nlt = l // t
    in_specs = [pl.BlockSpec((1, t, tc), lambda bi, ci, li: (bi, li, ci + u_blk)),
                pl.BlockSpec((1, halo, tc), lambda bi, ci, li: (bi, 0, ci)),
                pl.BlockSpec((kc, tc), lambda bi, ci, li: (0, ci)),
                pl.BlockSpec((1, tc), lambda bi, ci, li: (0, ci))]
    args = [u, prev, w, b]
    for a, off in xtiles:
        in_specs.append(pl.BlockSpec((1, t, tc), lambda bi, ci, li, off=off: (bi, li, ci + off)))
        args.append(a)
    for a in prows:
        in_specs.append(pl.BlockSpec((1, tc), lambda bi, ci, li: (0, ci)))
        args.append(a)
    body = functools.partial(_dwconv_body, kc=kc, halo=halo, t=t, tc=tc, n_seq_tiles=nlt,
                             nx=len(xtiles), npr=len(prows), epi=epi)
    return pl.pallas_call(
        body, grid=(bsz, c // tc, nlt), in_specs=in_specs,
        out_specs=pl.BlockSpec((1, t, tc), lambda bi, ci, li: (bi, li, ci)),
        out_shape=jax.ShapeDtypeStruct((bsz, l, c), out_dtype),
        scratch_shapes=[pltpu.VMEM((halo + t, tc), F32), pltpu.VMEM((t, tc), F32)],
        compiler_params=_cparams(("parallel", "parallel", "arbitrary")))(*args)


def _epi_conv_a(cv, xs, ps):
    g, b = ps
    mu = jnp.mean(cv, axis=-1, keepdims=True)
    d = cv - mu
    var = jnp.mean(d * d, axis=-1, keepdims=True)
    y = d * lax.rsqrt(var + LN_EPS) * g + b
    return y * jax.nn.sigmoid(y)


def _epi_silu(cv, xs, ps):
    return cv * jax.nn.sigmoid(cv)


def _epi_ffn(cv, xs, ps):
    return jax.nn.gelu(cv) * xs[0]


def _conv_state(prev, u, k1):
    l = u.shape[1]
    if l >= k1:
        return u[:, l - k1:]
    return jnp.concatenate([prev[:, l:], u], axis=1)


def _pad_prev(prev, halo):
    return jnp.pad(prev, ((0, 0), (halo - prev.shape[1], 0), (0, 0)))


def _t5_bucket(rel):
    n = jnp.maximum(rel, 0)
    max_exact = NUM_BUCKETS // 2
    nf = jnp.maximum(n, 1).astype(F32)
    large = max_exact + (jnp.log(nf / max_exact) / math.log(MAX_DISTANCE / max_exact)
                         * (NUM_BUCKETS - max_exact)).astype(jnp.int32)
    large = jnp.minimum(large, NUM_BUCKETS - 1)
    return jnp.where(n < max_exact, n, large)


def _bias_of_rel(t5_bias, rel):
    bias = jnp.moveaxis(t5_bias.astype(F32)[_t5_bucket(rel)], -1, 0)
    return jnp.where((rel >= 0)[None], bias, NEG_INF)


def _lambda_full(lq1, lk1, lq2, lk2, li, lam_init):
    a = jnp.sum(lq1[li:li + 1, :] * lk1[li:li + 1, :], axis=-1, keepdims=True)
    b = jnp.sum(lq2[li:li + 1, :] * lk2[li:li + 1, :], axis=-1, keepdims=True)
    return jnp.exp(a) - jnp.exp(b) + lam_init


def _softmax_step(s, v, m_sc, l_sc, acc_sc):
    m_prev = m_sc[...]
    m_new = jnp.maximum(m_prev, jnp.max(s, axis=-1, keepdims=True))
    alpha = jnp.exp(m_prev - m_new)
    p = jnp.exp(s - m_new)
    l_sc[...] = alpha * l_sc[...] + jnp.sum(p, axis=-1, keepdims=True)
    acc_sc[...] = alpha * acc_sc[...] + jnp.dot(p.astype(BF16), v, preferred_element_type=F32)
    m_sc[...] = m_new


def _sub_rms(o, g, lam_init):
    y = o * lax.rsqrt(jnp.mean(o * o, axis=-1, keepdims=True) + EPS) * g
    return y * (1.0 - lam_init)


def _pattn_body(q_ref, k_ref, v_ref, bt_ref, lq1, lk1, lq2, lk2, g_ref, o_ref,
                m_sc, l_sc, acc_sc, *, li, tq, lam_init):
    i = pl.program_id(2)
    q = q_ref[0]
    lane = lax.broadcasted_iota(jnp.int32, q.shape, 1)
    zero = jnp.zeros_like(q)
    qq = jnp.concatenate([jnp.where(lane < DIFF_DH, q, zero),
                          jnp.where(lane >= DIFF_DH, q, zero)], axis=0)
    m_sc[...] = jnp.full(m_sc.shape, NEG_INF, F32)
    l_sc[...] = jnp.zeros(l_sc.shape, F32)
    acc_sc[...] = jnp.zeros(acc_sc.shape, F32)

    def block(j, bias):
        off = pl.multiple_of(j * tq, tq)
        kb = k_ref[0, pl.ds(off, tq), :]
        vb = v_ref[0, pl.ds(off, tq), :]
        s = lax.dot_general(qq, kb, (((1,), (1,)), ((), ())), preferred_element_type=F32)
        _softmax_step(s + bias, vb, m_sc, l_sc, acc_sc)

    far = bt_ref[0, 1, tq - 1:tq, 0:1]

    def far_body(j, carry):
        block(j, far)
        return carry

    lax.fori_loop(0, jnp.maximum(i - 1, 0), far_body, 0)

    @pl.when(i >= 1)
    def _():
        b1 = bt_ref[0, 1]
        block(i - 1, jnp.concatenate([b1, b1], axis=0))

    b0 = bt_ref[0, 0]
    block(i, jnp.concatenate([b0, b0], axis=0))

    lam = _lambda_full(lq1, lk1, lq2, lk2, li, lam_init)
    o = acc_sc[...] / l_sc[...]
    o = o[:tq] - lam * o[tq:]
    o_ref[0] = _sub_rms(o, g_ref[li:li + 1, :], lam_init).astype(o_ref.dtype)


def _prompt_attn(qkv, bt, lams, subln_g, *, li, n_heads, lam_init):
    bsz, l, _ = qkv.shape
    tq = bt.shape[-1]
    h = n_heads
    full = lambda a: pl.BlockSpec(a.shape, lambda b, hh, i: (0,) * a.ndim)
    body = functools.partial(_pattn_body, li=li, tq=tq, lam_init=lam_init)
    return pl.pallas_call(
        body, grid=(bsz, h, l // tq),
        in_specs=[pl.BlockSpec((1, tq, KD), lambda b, hh, i: (b, i, hh)),
                  pl.BlockSpec((1, l, KD), lambda b, hh, i: (b, 0, h + hh)),
                  pl.BlockSpec((1, l, KD), lambda b, hh, i: (b, 0, 2 * h + hh)),
                  pl.BlockSpec((1, 2, tq, tq), lambda b, hh, i: (hh, 0, 0, 0))]
                 + [full(a) for a in lams] + [full(subln_g)],
        out_specs=pl.BlockSpec((1, tq, KD), lambda b, hh, i: (b, i, hh)),
        out_shape=jax.ShapeDtypeStruct((bsz, l, h * KD), BF16),
        scratch_shapes=[pltpu.VMEM((2 * tq, 1), F32), pltpu.VMEM((2 * tq, 1), F32),
                        pltpu.VMEM((2 * tq, KD), F32)],
        compiler_params=_cparams(("parallel", "parallel", "arbitrary")))(
            qkv, qkv, qkv, bt, *lams, subln_g)


def _sattn_body(pt_ref, qt_ref, *refs, li, pp, n_heads, n_q, lam_init):
    krefs = refs[:pp]
    vrefs = refs[pp:2 * pp]
    kn_ref, vn_ref, bias_ref, lq1, lk1, lq2, lk2, g_ref, o_ref, m_sc, l_sc, acc_sc = refs[2 * pp:]
    jp = pl.program_id(1)
    last = pl.num_programs(1) - 1

    @pl.when(jp == 0)
    def _():
        m_sc[...] = jnp.full(m_sc.shape, NEG_INF, F32)
        l_sc[...] = jnp.zeros(l_sc.shape, F32)
        acc_sc[...] = jnp.zeros(acc_sc.shape, F32)

    qt = qt_ref[0]

    def block(kb, vb, bias):
        s = lax.dot_general(qt, kb, (((1,), (1,)), ((), ())), preferred_element_type=F32)
        _softmax_step(s + bias, vb, m_sc, l_sc, acc_sc)

    kb = jnp.concatenate([r[0, 0].astype(BF16) for r in krefs], axis=0)
    vb = jnp.concatenate([r[0, 0].astype(BF16) for r in vrefs], axis=0)
    far = bias_ref[0]
    near = jnp.where(jp == last, bias_ref[1], far)
    block(kb, vb, jnp.concatenate([far] * (pp - 1) + [near], axis=1))

    @pl.when(jp == last)
    def _():
        block(kn_ref[0], vn_ref[0], bias_ref[2])
        lam = _lambda_full(lq1, lk1, lq2, lk2, li, lam_init)
        o = acc_sc[...] / l_sc[...]
        g = g_ref[li:li + 1, :]
        for h in range(n_heads):
            r0 = h * 2 * n_q
            o1 = o[r0:r0 + n_q, h * KD:(h + 1) * KD]
            o2 = o[r0 + n_q:r0 + 2 * n_q, h * KD:(h + 1) * KD]
            o_ref[0, :, h * KD:(h + 1) * KD] = _sub_rms(o1 - lam * o2, g, lam_init)


def _sample_attn(page_table, qt, cache_k, cache_v, k_new, v_new, bias3, lams, subln_g,
                 *, li, n_heads, n_q, lam_init):
    dbsz, n_pages = page_table.shape
    pp = _pick(n_pages, (PAGES_PER_STEP, 2, 1))
    w = n_heads * KD
    pt_flat = page_table.reshape(-1)

    def page_spec(g):
        return pl.BlockSpec((1, 1, PAGE_SIZE, w),
                            lambda b, j, pt, g=g: (li, pt[b * n_pages + j * pp + g], 0, 0))

    full = lambda a: pl.BlockSpec(a.shape, lambda b, j, pt: (0,) * a.ndim)
    per_b = pl.BlockSpec((1, LANES, w), lambda b, j, pt: (b, 0, 0))
    grid_spec = pltpu.PrefetchScalarGridSpec(
        num_scalar_prefetch=1, grid=(dbsz, n_pages // pp),
        in_specs=[per_b] + [page_spec(g) for g in range(pp)] * 2 + [per_b, per_b, full(bias3)]
                 + [full(a) for a in lams] + [full(subln_g)],
        out_specs=pl.BlockSpec((1, n_q, w), lambda b, j, pt: (b, 0, 0)),
        scratch_shapes=[pltpu.VMEM((LANES, 1), F32), pltpu.VMEM((LANES, 1), F32),
                        pltpu.VMEM((LANES, w), F32)])
    body = functools.partial(_sattn_body, li=li, pp=pp, n_heads=n_heads, n_q=n_q,
                             lam_init=lam_init)
    return pl.pallas_call(
        body, grid_spec=grid_spec,
        out_shape=jax.ShapeDtypeStruct((dbsz, n_q, w), F32),
        compiler_params=_cparams(("parallel", "arbitrary")))(
            pt_flat, qt, *([cache_k] * pp), *([cache_v] * pp), k_new, v_new, bias3,
            *lams, subln_g)


def _block_queries(q, n_heads):
    dbsz, n_q, w = q.shape
    assert n_heads * 2 * n_q <= LANES
    qh = q.reshape(dbsz, n_q, n_heads, KD)
    col_map = (jnp.arange(KD) // DIFF_DH)[None, :] == jnp.arange(2)[:, None]
    t = jnp.transpose(qh, (0, 2, 1, 3))[:, :, None, :, None, :]
    t = t * col_map[None, None, :, None, None, :].astype(q.dtype)
    t = t * jnp.eye(n_heads, dtype=q.dtype)[None, :, None, None, :, None]
    t = t.reshape(dbsz, n_heads * 2 * n_q, w)
    return jnp.pad(t, ((0, 0), (0, LANES - n_heads * 2 * n_q), (0, 0))).astype(BF16)


def _sample_bias(t5_bias, n_heads, n_q, past_len):
    rows = n_heads * 2 * n_q
    qi = jnp.arange(n_q, dtype=jnp.int32)
    key = jnp.arange(LANES, dtype=jnp.int32)

    def tile(rel):
        b = _bias_of_rel(t5_bias, rel)
        b = jnp.broadcast_to(b[:, None], (n_heads, 2, n_q, LANES)).reshape(rows, LANES)
        return jnp.pad(b, ((0, LANES - rows), (0, 0)))

    far = tile(jnp.full((n_q, LANES), MAX_DISTANCE, jnp.int32))
    lastp = tile(PAGE_SIZE + qi[:, None] - key[None, :])
    new = tile(jnp.where(key[None, :] < n_q, qi[:, None] - key[None, :], -1))
    return jnp.stack([far, lastp, new])


def _ssd_body(xbc_ref, z_ref, dt_ref, h0_ref, dtb_ref, alog_ref, dx_ref, gn_ref,
              y_ref, hout_ref, hs, ysc, *, n_heads, valid_len):
    c = pl.program_id(1)
    t = SSD_CHUNK
    di = n_heads * SSD_P
    e_per_g = n_heads // SSD_G

    @pl.when(c == 0)
    def _():
        hs[...] = h0_ref[0]

    xbc = xbc_ref[0]
    xs = xbc[:, :di]
    row = lax.broadcasted_iota(jnp.int32, (t, LANES), 0)
    lane = lax.broadcasted_iota(jnp.int32, (t, LANES), 1)
    raw = dt_ref[0] + dtb_ref[...]
    dt = jnp.maximum(raw, 0.0) + jnp.log1p(jnp.exp(-jnp.abs(raw)))
    dt = jnp.where((row + c * t < valid_len) & (lane < n_heads), dt, 0.0)
    da = dt * (-jnp.exp(alog_ref[...]))
    tri = (lax.broadcasted_iota(jnp.int32, (t, t), 0)
           >= lax.broadcasted_iota(jnp.int32, (t, t), 1))
    cum = jnp.dot(tri.astype(F32), da, precision=lax.Precision.HIGHEST,
                  preferred_element_type=F32)
    cum_t = cum.T
    dt_t = dt.T
    cum_last = cum[t - 1:t, :]
    w_end = dt * jnp.exp(cum_last - cum)
    e_cum = jnp.exp(cum)
    e_last = jnp.exp(cum_last)
    xs_bf = xs.astype(BF16)
    xs_t = xs.T.astype(BF16)

    for g in range(SSD_G):
        b_g = xbc[:, di + g * SSD_N:di + (g + 1) * SSD_N]
        c_g = xbc[:, di + (SSD_G + g) * SSD_N:di + (SSD_G + g + 1) * SSD_N]
        cb = lax.dot_general(c_g.astype(BF16), b_g.astype(BF16), (((1,), (1,)), ((), ())),
                             preferred_element_type=F32)
        for e in range(e_per_g):
            h = g * e_per_g + e
            seg = cum[:, h:h + 1] - cum_t[h:h + 1, :]
            decay = jnp.where(tri, jnp.exp(jnp.where(tri, seg, 0.0)), 0.0)
            mat = (cb * decay * dt_t[h:h + 1, :]).astype(BF16)
            x_e = xs_bf[:, h * SSD_P:(h + 1) * SSD_P]
            y_d = jnp.dot(mat, x_e, preferred_element_type=F32)
            h_e = hs[h]
            c_w = (c_g * e_cum[:, h:h + 1]).astype(BF16)
            y_o = lax.dot_general(c_w, h_e.astype(BF16), (((1,), (1,)), ((), ())),
                                  preferred_element_type=F32)
            b_w = (b_g * w_end[:, h:h + 1]).astype(BF16)
            st = jnp.dot(xs_t[h * SSD_P:(h + 1) * SSD_P, :], b_w,
                         preferred_element_type=F32)
            hs[h] = h_e * e_last[:, h:h + 1] + st
            ysc[:, h * SSD_P:(h + 1) * SSD_P] = y_d + y_o

    y = ysc[...] + xs * dx_ref[...]
    z = z_ref[0]
    y = y * (z * jax.nn.sigmoid(z))
    gw = di // SSD_G
    for g in range(SSD_G):
        seg = y[:, g * gw:(g + 1) * gw]
        seg = seg * lax.rsqrt(jnp.mean(seg * seg, axis=-1, keepdims=True) + EPS)
        y_ref[0, :, g * gw:(g + 1) * gw] = (seg * gn_ref[:, g * gw:(g + 1) * gw]).astype(y_ref.dtype)

    @pl.when(c == pl.num_programs(1) - 1)
    def _():
        hout_ref[0] = hs[...]


def _ssd(xbc_act, zxd, z_blk_w, dt_blk, h0, dtb, alog, dx, gn, *, n_heads, valid_len):
    bsz, l, xbc_w = xbc_act.shape
    di = n_heads * SSD_P
    t = SSD_CHUNK
    assert l % t == 0 and z_blk_w == di
    full = lambda a: pl.BlockSpec(a.shape, lambda b, c: (0,) * a.ndim)
    body = functools.partial(_ssd_body, n_heads=n_heads, valid_len=valid_len)
    return pl.pallas_call(
        body, grid=(bsz, l // t),
        in_specs=[pl.BlockSpec((1, t, xbc_w), lambda b, c: (b, c, 0)),
                  pl.BlockSpec((1, t, di), lambda b, c: (b, c, 0)),
                  pl.BlockSpec((1, t, LANES), lambda b, c: (b, c, dt_blk)),
                  pl.BlockSpec((1, n_heads, SSD_P, SSD_N), lambda b, c: (b, 0, 0, 0)),
                  full(dtb), full(alog), full(dx), full(gn)],
        out_specs=[pl.BlockSpec((1, t, di), lambda b, c: (b, c, 0)),
                   pl.BlockSpec((1, n_heads, SSD_P, SSD_N), lambda b, c: (b, 0, 0, 0))],
        out_shape=[jax.ShapeDtypeStruct((bsz, l, di), BF16),
                   jax.ShapeDtypeStruct((bsz, n_heads, SSD_P, SSD_N), F32)],
        scratch_shapes=[pltpu.VMEM((n_heads, SSD_P, SSD_N), F32), pltpu.VMEM((t, di), F32)],
        compiler_params=_cparams(("parallel", "arbitrary")))(
            xbc_act, zxd, zxd, h0, dtb, alog, dx, gn)


def _epi_glu(accs, tiles, rows, j):
    a, b = accs
    return (a * jax.nn.sigmoid(b),)


def _epi_qkv(accs, tiles, rows, j, *, q_tiles, scale):
    acc = accs[0]
    return acc, acc * jnp.where(j < q_tiles, scale, 1.0)


def _epi_id(accs, tiles, rows, j):
    return (accs[0],)


def _epi_merge(accs, tiles, rows, j):
    ya, yb, yc = accs
    ga, gb, gc = tiles
    return (jax.nn.sigmoid(ga) * ya + jax.nn.sigmoid(gb) * yb + jax.nn.sigmoid(gc) * yc,)


def _epi_resid(accs, tiles, rows, j):
    return (tiles[0] + rows[0] * accs[0],)


def _trunk(x, mod, prev_a, prev_c, prev_h, prev_f, wts, p, dims, *, paged=None):
    bsz, l, d = x.shape
    m = bsz * l
    depth, n_heads, conv_c, ssd_h, d_ff = (dims[k] for k in
                                           ("depth", "n_heads", "conv_c", "ssd_h", "d_ff"))
    att_w = n_heads * KD
    ssd_di = ssd_h * SSD_P
    ssd_xbc = ssd_di + 2 * SSD_G * SSD_N
    conv_k = p["conv_a_w"].shape[1]
    ssd_k = p["ssd_conv_w"].shape[1]
    ffn_k = p["ffn_conv_w"].shape[1]
    prompt = paged is None
    tm = _pick(l, (512, 256, 128)) if prompt else m
    tpb = l // tm if prompt else 1
    t_seq = _pick(l, (128,)) if prompt else l
    lams = (p["lambda_q1"], p["lambda_k1"], p["lambda_q2"], p["lambda_k2"])
    scale = DIFF_DH ** -0.5
    big = (1024, 512, 256, 128)

    xf = x.reshape(m, d)
    out_k, out_v, out_a, out_c, out_h, out_f = [], [], [], [], [], []
    for li in range(depth):
        w = wts[li]
        md = mod[li]
        lam_init = 0.8 - 0.6 * math.exp(-0.3 * li)
        h = _norm_mod(xf, p["norm1_g"][li][None], md, 1, 0, tm=tm, tpb=tpb)

        tn = _pick(conv_c, big)
        (u,) = _mm([h, h], [(w["glu"], 0), (w["glu"], conv_c // tn)], _epi_glu, [F32], conv_c,
                   tm=tm, tn=tn)
        tn = _pick(att_w, big)
        qkv, qkv_bf = _mm([h], [(w["qkv"], 0)],
                          functools.partial(_epi_qkv, q_tiles=att_w // tn, scale=scale),
                          [F32, BF16], 3 * att_w, tm=tm, tn=tn)
        zxd_w = w["zxd"].shape[1]
        tn = _pick(zxd_w, (896, 640, 512, 384, 256, 128))
        (zxd,) = _mm([h], [(w["zxd"], 0)], _epi_id, [F32], zxd_w, tm=tm, tn=tn)
        tn = _pick(3 * d, big)
        (gates,) = _mm([h], [(w["gates"], 0)], _epi_id, [F32], 3 * d, tm=tm, tn=tn)

        u3 = u.reshape(bsz, l, conv_c)
        out_a.append(_conv_state(prev_a[li], u3, conv_k - 1))
        a_act = _dwconv(u3, 0, _pad_prev(prev_a[li], 32), p["conv_a_w"][li],
                        p["conv_a_b"][li][None], _epi_conv_a, BF16, c=conv_c, t=t_seq,
                        tc=conv_c, halo=32,
                        prows=(p["ln_a_g"][li][None], p["ln_a_b"][li][None]))

        k3 = qkv[:, att_w:2 * att_w].reshape(bsz, l, n_heads, KD)
        v3 = qkv[:, 2 * att_w:].reshape(bsz, l, n_heads, KD)
        out_k.append(k3)
        out_v.append(v3)
        if prompt:
            o = _prompt_attn(qkv_bf.reshape(bsz, l, 3 * att_w), dims["bias_tiles"], lams,
                             p["subln_g"], li=li, n_heads=n_heads, lam_init=lam_init)
        else:
            qb = qkv_bf.reshape(bsz, l, 3 * att_w)
            qt = _block_queries(qb[..., :att_w], n_heads)
            pad = ((0, 0), (0, LANES - l), (0, 0))
            k_new = jnp.pad(qb[..., att_w:2 * att_w], pad)
            v_new = jnp.pad(qb[..., 2 * att_w:], pad)
            o = _sample_attn(paged["page_table"], qt, paged["cache_k"], paged["cache_v"],
                             k_new, v_new, dims["bias3"], lams, p["subln_g"], li=li,
                             n_heads=n_heads, n_q=l, lam_init=lam_init).astype(BF16)
        o = o.reshape(m, att_w)

        zxd3 = zxd.reshape(bsz, l, zxd_w)
        xbc_raw = zxd3[..., ssd_di:ssd_di + ssd_xbc]
        out_c.append(_conv_state(prev_c[li], xbc_raw, ssd_k - 1))
        tc = _pick(math.gcd(ssd_di, ssd_xbc), (512, 256, 128))
        xbc_act = _dwconv(zxd3, ssd_di // tc, _pad_prev(prev_c[li], 8), p["ssd_conv_w"][li],
                          p["ssd_conv_b"][li][None], _epi_silu, F32, c=ssd_xbc, t=t_seq, tc=tc,
                          halo=8)
        if prompt:
            xbc_in, zxd_in = xbc_act, zxd3
        else:
            pad = ((0, 0), (0, SSD_CHUNK - l), (0, 0))
            xbc_in, zxd_in = jnp.pad(xbc_act, pad), jnp.pad(zxd3, pad)
        lane_pad = lambda a: jnp.pad(a, (0, LANES - a.shape[0]))[None]
        y_c, h_t = _ssd(xbc_in, zxd_in, ssd_di, (ssd_di + ssd_xbc) // LANES, prev_h[li],
                        lane_pad(p["dt_bias"][li]), lane_pad(p["a_log"][li]),
                        jnp.repeat(p["d_skip"][li], SSD_P)[None], p["ssd_norm_g"][li][None],
                        n_heads=ssd_h, valid_len=l)
        out_h.append(h_t)
        y_c = y_c[:, :l].reshape(m, ssd_di)

        tn = _pick(d, big)
        nb = d // tn
        (merged,) = _mm([a_act.reshape(m, conv_c), o, y_c],
                        [(w["proj_a"], 0), (w["proj_b"], 0), (w["proj_c"], 0)], _epi_merge,
                        [BF16], d, tiles=[(gates, 0), (gates, nb), (gates, 2 * nb)], tm=tm, tn=tn)
        (xf,) = _mm([merged], [(w["out"], 0)], _epi_resid, [F32], d, tiles=[(xf, 0)],
                    rows=[(md, 2 * nb)], tm=tm, tn=tn, tpb=tpb)

        h2 = _norm_mod(xf, p["norm2_g"][li][None], md, 4, 3, tm=tm, tpb=tpb)
        tn = _pick(d_ff, big)
        (up,) = _mm([h2], [(w["up"], 0)], _epi_id, [F32], 2 * d_ff, tm=tm, tn=tn)
        up3 = up.reshape(bsz, l, 2 * d_ff)
        out_f.append(_conv_state(prev_f[li], up3[..., :d_ff], ffn_k - 1))
        tc = _pick(d_ff, (512, 256, 128))
        act = _dwconv(up3, 0, _pad_prev(prev_f[li], 8), p["ffn_conv_w"][li],
                      p["ffn_conv_b"][li][None], _epi_ffn, BF16, c=d_ff, t=t_seq, tc=tc, halo=8,
                      xtiles=[(up3, d_ff // tc)])
        tn = _pick(d, big)
        (xf,) = _mm([act.reshape(m, d_ff)], [(w["down"], 0)], _epi_resid, [F32], d,
                    tiles=[(xf, 0)], rows=[(md, 5 * (d // tn))], tm=tm, tn=tn, tpb=tpb)

    y_out = _rms(xf, p["final_g"][None], tm=tm).reshape(bsz, l, d)
    return (y_out, jnp.stack(out_k), jnp.stack(out_v), jnp.stack(out_a), jnp.stack(out_c),
            jnp.stack(out_h), jnp.stack(out_f))


def _split_weights(p, li, dims):
    conv_c, att_w = dims["conv_c"], dims["n_heads"] * KD
    ssd_di = dims["ssd_h"] * SSD_P
    ssd_xbc = ssd_di + 2 * SSD_G * SSD_N
    w_in = p["w_in"][li]
    c0 = 2 * conv_c
    c1 = c0 + 3 * att_w
    c2 = c1 + ssd_di + ssd_xbc
    c3 = c2 + dims["ssd_h"]
    dt_cols = jnp.pad(w_in[:, c2:c3], ((0, 0), (0, LANES - dims["ssd_h"])))
    bf = lambda a: a.astype(BF16)
    return dict(glu=bf(w_in[:, :c0]), qkv=bf(w_in[:, c0:c1]),
                zxd=bf(jnp.concatenate([w_in[:, c1:c2], dt_cols], axis=1)),
                gates=bf(w_in[:, c3:]), proj_a=bf(p["w_proj_a"][li]), proj_b=bf(p["w_proj_b"][li]),
                proj_c=bf(p["w_proj_c"][li]), out=bf(p["w_out"][li]), up=bf(p["ffn_up"][li]),
                down=bf(p["ffn_down"][li]))


def kernel(x_prompt, x_sample, cache_k, cache_v, page_table, state_conv_a, state_ssd_conv,
           state_ssm, state_ffn_conv, c_prompt, c_sample, w_ada, b_ada, norm1_g, w_in,
           conv_a_w, conv_a_b, ln_a_g, ln_a_b, w_proj_a, t5_bias, lambda_q1, lambda_k1,
           lambda_q2, lambda_k2, subln_g, w_proj_b, ssd_conv_w, ssd_conv_b, dt_bias, a_log,
           d_skip, ssd_norm_g, w_proj_c, w_out, norm2_g, ffn_up, ffn_conv_w, ffn_conv_b,
           ffn_down, final_g):
    p = dict(w_ada=w_ada, b_ada=b_ada, norm1_g=norm1_g, w_in=w_in, conv_a_w=conv_a_w,
             conv_a_b=conv_a_b, ln_a_g=ln_a_g, ln_a_b=ln_a_b, w_proj_a=w_proj_a, t5_bias=t5_bias,
             lambda_q1=lambda_q1, lambda_k1=lambda_k1, lambda_q2=lambda_q2, lambda_k2=lambda_k2,
             subln_g=subln_g, w_proj_b=w_proj_b, ssd_conv_w=ssd_conv_w, ssd_conv_b=ssd_conv_b,
             dt_bias=dt_bias, a_log=a_log, d_skip=d_skip, ssd_norm_g=ssd_norm_g, w_proj_c=w_proj_c,
             w_out=w_out, norm2_g=norm2_g, ffn_up=ffn_up, ffn_conv_w=ffn_conv_w,
             ffn_conv_b=ffn_conv_b, ffn_down=ffn_down, final_g=final_g)
    bp, lp, d = x_prompt.shape
    bs, ls, _ = x_sample.shape
    depth = w_ada.shape[0]
    n_heads = cache_k.shape[3]
    n_pages = page_table.shape[1]
    dims = dict(depth=depth, n_heads=n_heads, conv_c=conv_a_w.shape[2], ssd_h=dt_bias.shape[1],
                d_ff=ffn_conv_w.shape[2])
    f32 = x_prompt.dtype

    tq = _pick(lp, (ATTN_TQ, 128))
    assert tq >= MAX_DISTANCE
    qi = jnp.arange(tq, dtype=jnp.int32)
    rel0 = qi[:, None] - qi[None, :]
    dims["bias_tiles"] = jnp.stack([_bias_of_rel(t5_bias, rel0), _bias_of_rel(t5_bias, rel0 + tq)],
                                   axis=1)
    dims["bias3"] = _sample_bias(t5_bias, n_heads, ls, n_pages * PAGE_SIZE)

    rows = -(-(bp + bs) // 8) * 8
    c_all = jnp.pad(jnp.concatenate([c_prompt, c_sample], axis=0), ((0, rows - bp - bs), (0, 0)))
    mod = _ada(c_all, w_ada, b_ada)
    mod_p = mod[:, :bp, None, :]
    mod_s = jnp.repeat(mod[:, bp:bp + bs], ls, axis=1)[:, None]

    wts = [_split_weights(p, li, dims) for li in range(depth)]
    zero = lambda *s: jnp.zeros((depth, bp) + s, f32)
    res_p = _trunk(x_prompt, mod_p, zero(conv_a_w.shape[1] - 1, dims["conv_c"]),
                   zero(ssd_conv_w.shape[1] - 1, ssd_conv_w.shape[2]),
                   zero(dims["ssd_h"], SSD_P, SSD_N), zero(ffn_conv_w.shape[1] - 1, dims["d_ff"]),
                   wts, p, dims)
    w_att = n_heads * KD
    paged = dict(page_table=page_table,
                 cache_k=cache_k.reshape(depth, -1, PAGE_SIZE, w_att),
                 cache_v=cache_v.reshape(depth, -1, PAGE_SIZE, w_att))
    res_s = _trunk(x_sample, mod_s, state_conv_a, state_ssd_conv, state_ssm, state_ffn_conv,
                   wts, p, dims, paged=paged)
    y_p, k_p, v_p, a_p, c_p, h_p, f_p = res_p
    y_s, k_s, v_s, a_s, c_s, h_s, f_s = res_s
    return (y_p, y_s, k_p, v_p, k_s, v_s, a_p, a_s, c_p, c_s, h_p, h_s, f_p, f_s)
```

```python
import functools
import math

import jax
import jax.numpy as jnp
from jax import lax
from jax.experimental import pallas as pl
from jax.experimental.pallas import tpu as pltpu

F32 = jnp.float32
BF16 = jnp.bfloat16

EPS = 1e-6
LN_EPS = 1e-5
NEG_INF = -1e30
DIFF_DH = 64
KD = 2 * DIFF_DH
NUM_BUCKETS = 32
MAX_DISTANCE = 128
SSD_P = 64
SSD_N = 128
SSD_G = 2
SSD_CHUNK = 128
PAGE_SIZE = 128
LANES = 128
VMEM_LIMIT = 56 * 1024 * 1024

ATTN_TQ = 256
PAGES_PER_STEP = 4


def _cparams(sem):
    return pltpu.CompilerParams(dimension_semantics=sem, vmem_limit_bytes=VMEM_LIMIT)


def _pick(n, cands):
    for c in cands:
        if n % c == 0:
            return c
    return n


def _mm_body(*refs, nl, nt, nr, epi):
    lrefs = refs[:nl]
    wrefs = refs[nl:2 * nl]
    trefs = refs[2 * nl:2 * nl + nt]
    rrefs = refs[2 * nl + nt:2 * nl + nt + nr]
    orefs = refs[2 * nl + nt + nr:]
    accs = [jnp.dot(l[...], w[...], preferred_element_type=F32) for l, w in zip(lrefs, wrefs)]
    res = epi(accs, [t[...] for t in trefs], [r[0] for r in rrefs], pl.program_id(1))
    for o, r in zip(orefs, res):
        o[...] = r.astype(o.dtype)


def _mm(lhs, rhs, epi, out_dtypes, n, *, tiles=(), rows=(), tm, tn, tpb=1):
    m = lhs[0].shape[0]
    assert m % tm == 0 and n % tn == 0
    in_specs, args = [], []
    for a in lhs:
        in_specs.append(pl.BlockSpec((tm, a.shape[1]), lambda i, j: (i, 0)))
        args.append(a)
    for a, off in rhs:
        in_specs.append(pl.BlockSpec((a.shape[0], tn), lambda i, j, off=off: (0, j + off)))
        args.append(a)
    for a, off in tiles:
        in_specs.append(pl.BlockSpec((tm, tn), lambda i, j, off=off: (i, j + off)))
        args.append(a)
    for a, off in rows:
        in_specs.append(pl.BlockSpec((1, a.shape[1], tn),
                                     lambda i, j, off=off: (i // tpb, 0, j + off)))
        args.append(a)
    out_specs = [pl.BlockSpec((tm, tn), lambda i, j: (i, j)) for _ in out_dtypes]
    out_shape = [jax.ShapeDtypeStruct((m, n), d) for d in out_dtypes]
    body = functools.partial(_mm_body, nl=len(lhs), nt=len(tiles), nr=len(rows), epi=epi)
    return pl.pallas_call(
        body, grid=(m // tm, n // tn), in_specs=in_specs, out_specs=out_specs,
        out_shape=out_shape, compiler_params=_cparams(("parallel", "arbitrary")))(*args)


def _qkv_body(h_ref, w_ref, kin_ref, vin_ref, qkv_ref, kst_ref, vst_ref, *, nq, scale):
    del kin_ref, vin_ref
    j = pl.program_id(1)
    acc = jnp.dot(h_ref[...], w_ref[...], preferred_element_type=F32)
    qkv_ref[...] = (acc * jnp.where(j < nq, scale, 1.0)).astype(qkv_ref.dtype)

    @pl.when((j >= nq) & (j < 2 * nq))
    def _():
        kst_ref[0] = acc

    @pl.when(j >= 2 * nq)
    def _():
        vst_ref[0] = acc


def _qkv(h, w, kst, vst, *, li, scale, tm, tn):
    m, d = h.shape
    wdt = w.shape[1] // 3
    nq = wdt // tn
    assert m % tm == 0 and wdt % tn == 0
    k_idx = lambda i, j: (li, i, jnp.clip(j - nq, 0, nq - 1))
    v_idx = lambda i, j: (li, i, jnp.clip(j - 2 * nq, 0, nq - 1))
    body = functools.partial(_qkv_body, nq=nq, scale=scale)
    return pl.pallas_call(
        body, grid=(m // tm, 3 * nq),
        in_specs=[pl.BlockSpec((tm, d), lambda i, j: (i, 0)),
                  pl.BlockSpec((d, tn), lambda i, j: (0, j)),
                  pl.BlockSpec(memory_space=pl.ANY), pl.BlockSpec(memory_space=pl.ANY)],
        out_specs=[pl.BlockSpec((tm, tn), lambda i, j: (i, j)),
                   pl.BlockSpec((1, tm, tn), k_idx), pl.BlockSpec((1, tm, tn), v_idx)],
        out_shape=[jax.ShapeDtypeStruct((m, 3 * wdt), BF16),
                   jax.ShapeDtypeStruct(kst.shape, F32), jax.ShapeDtypeStruct(vst.shape, F32)],
        input_output_aliases={2: 1, 3: 2},
        compiler_params=_cparams(("arbitrary", "arbitrary")))(h, w, kst, vst)


def _ada_body(c_ref, w_ref, b_ref, o_ref):
    c = c_ref[...]
    s = (c * jax.nn.sigmoid(c)).astype(BF16)
    o_ref[0] = jnp.dot(s, w_ref[0].astype(BF16), preferred_element_type=F32) + b_ref[0]


def _ada(c_all, w_ada, b_ada):
    depth, d, n6 = w_ada.shape
    rows = c_all.shape[0]
    tn = _pick(n6, (1024, 512, 256, 128))
    return pl.pallas_call(
        _ada_body, grid=(depth, n6 // tn),
        in_specs=[pl.BlockSpec((rows, d), lambda l, j: (0, 0)),
                  pl.BlockSpec((1, d, tn), lambda l, j: (l, 0, j)),
                  pl.BlockSpec((1, 1, tn), lambda l, j: (l, 0, j))],
        out_specs=pl.BlockSpec((1, rows, tn), lambda l, j: (l, 0, j)),
        out_shape=jax.ShapeDtypeStruct((depth, rows, n6), F32),
        compiler_params=_cparams(("arbitrary", "arbitrary")))(c_all, w_ada, b_ada[:, None, :])


def _norm_mod_body(x_ref, g_ref, sc_ref, sh_ref, o_ref):
    x = x_ref[...]
    y = x * lax.rsqrt(jnp.mean(x * x, axis=-1, keepdims=True) + EPS) * g_ref[...]
    o_ref[...] = (y * (1.0 + sc_ref[0]) + sh_ref[0]).astype(o_ref.dtype)


def _norm_mod(x, g, mod, sc_blk, sh_blk, *, tm, tpb):
    m, d = x.shape
    r = mod.shape[1]
    return pl.pallas_call(
        _norm_mod_body, grid=(m // tm,),
        in_specs=[pl.BlockSpec((tm, d), lambda i: (i, 0)),
                  pl.BlockSpec((1, d), lambda i: (0, 0)),
                  pl.BlockSpec((1, r, d), lambda i: (i // tpb, 0, sc_blk)),
                  pl.BlockSpec((1, r, d), lambda i: (i // tpb, 0, sh_blk))],
        out_specs=pl.BlockSpec((tm, d), lambda i: (i, 0)),
        out_shape=jax.ShapeDtypeStruct((m, d), BF16),
        compiler_params=_cparams(("parallel",)))(x, g, mod, mod)


def _rms_body(x_ref, g_ref, o_ref):
    x = x_ref[...]
    o_ref[...] = x * lax.rsqrt(jnp.mean(x * x, axis=-1, keepdims=True) + EPS) * g_ref[...]


def _rms(x, g, *, tm):
    m, d = x.shape
    return pl.pallas_call(
        _rms_body, grid=(m // tm,),
        in_specs=[pl.BlockSpec((tm, d), lambda i: (i, 0)),
                  pl.BlockSpec((1, d), lambda i: (0, 0))],
        out_specs=pl.BlockSpec((tm, d), lambda i: (i, 0)),
        out_shape=jax.ShapeDtypeStruct((m, d), F32),
        compiler_params=_cparams(("parallel",)))(x, g)


def _dwconv_body(*refs, kc, halo, t, tc, n_seq_tiles, nx, npr, epi):
    u_ref, prev_ref, w_ref, b_ref = refs[:4]
    xrefs = refs[4:4 + nx]
    prefs = refs[4 + nx:4 + nx + npr]
    o_ref = refs[4 + nx + npr]
    win, cv = refs[4 + nx + npr + 1:]

    @pl.when(pl.program_id(2) == 0)
    def _():
        win[0:halo, :] = prev_ref[0]

    win[halo:halo + t, :] = u_ref[0]
    base = halo - (kc - 1)
    rc = min(t, 64)
    cc = min(tc, 256)
    for r0 in range(0, t, rc):
        for c0 in range(0, tc, cc):
            acc = jnp.broadcast_to(b_ref[:, c0:c0 + cc], (rc, cc))
            for r in range(min(8, kc)):
                taps = list(range(r, kc, 8))
                span = rc + 8 * (len(taps) - 1)
                start = base + r0 + r
                sh = win[start:start + span, c0:c0 + cc]
                for ai, j in enumerate(taps):
                    acc = acc + w_ref[j:j + 1, c0:c0 + cc] * sh[8 * ai:8 * ai + rc]
            cv[r0:r0 + rc, c0:c0 + cc] = acc
    o_ref[0] = epi(cv[...], [x[0] for x in xrefs], [p[...] for p in prefs]).astype(o_ref.dtype)
    if n_seq_tiles > 1:
        win[0:halo, :] = win[t:t + halo, :]


def _dwconv(u, u_blk, prev, w, b, epi, out_dtype, *, c, t, tc, halo, xtiles=(), prows=()):
    bsz, l, _ = u.shape
    kc = w.shape[0]
    assert l % t == 0 and c % tc == 0 and halo >= kc - 1 and (t >= halo or l == t)
    nlt = l // t
    in_specs = [pl.BlockSpec((1, t, tc), lambda bi, ci, li: (bi, li, ci + u_blk)),
                pl.BlockSpec((1, halo, tc), lambda bi, ci, li: (bi, 0, ci)),
                pl.BlockSpec((kc, tc), lambda bi, ci, li: (0, ci)),
                pl.BlockSpec((1, tc), lambda bi, ci, li: (0, ci))]
    args = [u, prev, w, b]
    for a, off in xtiles:
        in_specs.append(pl.BlockSpec((1, t, tc), lambda bi, ci, li, off=off: (bi, li, ci + off)))
        args.append(a)
    for a in prows:
        in_specs.append(pl.BlockSpec((1, tc), lambda bi, ci, li: (0, ci)))
        args.append(a)
    body = functools.partial(_dwconv_body, kc=kc, halo=halo, t=t, tc=tc, n_seq_tiles=nlt,
                             nx=len(xtiles), npr=len(prows), epi=epi)
    return pl.pallas_call(
        body, grid=(bsz, c // tc, nlt), in_specs=in_specs,
        out_specs=pl.BlockSpec((1, t, tc), lambda bi, ci, li: (bi, li, ci)),
        out_shape=jax.ShapeDtypeStruct((bsz, l, c), out_dtype),
        scratch_shapes=[pltpu.VMEM((halo + t, tc), F32), pltpu.VMEM((t, tc), F32)],
        compiler_params=_cparams(("parallel", "parallel", "arbitrary")))(*args)


def _epi_conv_a(cv, xs, ps):
    g, b = ps
    mu = jnp.mean(cv, axis=-1, keepdims=True)
    d = cv - mu
    var = jnp.mean(d * d, axis=-1, keepdims=True)
    y = d * lax.rsqrt(var + LN_EPS) * g + b
    return y * jax.nn.sigmoid(y)


def _epi_silu(cv, xs, ps):
    return cv * jax.nn.sigmoid(cv)


def _epi_ffn(cv, xs, ps):
    return jax.nn.gelu(cv) * xs[0]


UP_HALO = 8


def _upconv_body(h_ref, wa_ref, wb_ref, prev_ref, cw_ref, cb_ref, act_ref, st_ref,
                 win, carry, *, tm, tn, tpb, kc):
    i = pl.program_id(0)
    j = pl.program_id(1)
    h = h_ref[...]
    fa = jnp.dot(h, wa_ref[...], preferred_element_type=F32)
    fb = jnp.dot(h, wb_ref[...], preferred_element_type=F32)
    first = i % tpb == 0

    @pl.when(first)
    def _():
        win[0:UP_HALO, :] = prev_ref[0]

    @pl.when(jnp.logical_not(first))
    def _():
        win[0:UP_HALO, :] = carry[j]

    win[UP_HALO:UP_HALO + tm, :] = fa
    base = UP_HALO - (kc - 1)
    rc = min(tm, 64)
    cc = min(tn, 256)
    for r0 in range(0, tm, rc):
        for c0 in range(0, tn, cc):
            acc = jnp.broadcast_to(cb_ref[:, c0:c0 + cc], (rc, cc))
            for t in range(kc):
                start = base + r0 + t
                acc = acc + cw_ref[t:t + 1, c0:c0 + cc] * win[start:start + rc, c0:c0 + cc]
            act_ref[r0:r0 + rc, c0:c0 + cc] = (
                jax.nn.gelu(acc) * fb[r0:r0 + rc, c0:c0 + cc]).astype(act_ref.dtype)
    tail = win[tm:tm + UP_HALO, :]
    carry[j] = tail
    st_ref[0] = tail


def _upconv(h, w_up, prev, cw, cb, *, d_ff, tm, tn, tpb):
    m, d = h.shape
    kc = cw.shape[0]
    nj = d_ff // tn
    assert m % tm == 0 and d_ff % tn == 0 and kc - 1 <= UP_HALO <= tm
    body = functools.partial(_upconv_body, tm=tm, tn=tn, tpb=tpb, kc=kc)
    return pl.pallas_call(
        body, grid=(m // tm, nj),
        in_specs=[pl.BlockSpec((tm, d), lambda i, j: (i, 0)),
                  pl.BlockSpec((d, tn), lambda i, j: (0, j)),
                  pl.BlockSpec((d, tn), lambda i, j: (0, j + nj)),
                  pl.BlockSpec((1, UP_HALO, tn), lambda i, j: (i // tpb, 0, j)),
                  pl.BlockSpec((kc, tn), lambda i, j: (0, j)),
                  pl.BlockSpec((1, tn), lambda i, j: (0, j))],
        out_specs=[pl.BlockSpec((tm, tn), lambda i, j: (i, j)),
                   pl.BlockSpec((1, UP_HALO, tn), lambda i, j: (i // tpb, 0, j))],
        out_shape=[jax.ShapeDtypeStruct((m, d_ff), BF16),
                   jax.ShapeDtypeStruct((m // (tm * tpb), UP_HALO, d_ff), F32)],
        scratch_shapes=[pltpu.VMEM((UP_HALO + tm, tn), F32), pltpu.VMEM((nj, UP_HALO, tn), F32)],
        compiler_params=_cparams(("arbitrary", "arbitrary")))(h, w_up, w_up, prev, cw, cb)


def _conv_state(prev, u, k1):
    l = u.shape[1]
    if l >= k1:
        return u[:, l - k1:]
    return jnp.concatenate([prev[:, l:], u], axis=1)


def _pad_prev(prev, halo):
    return jnp.pad(prev, ((0, 0), (halo - prev.shape[1], 0), (0, 0)))


def _t5_bucket(rel):
    n = jnp.maximum(rel, 0)
    max_exact = NUM_BUCKETS // 2
    nf = jnp.maximum(n, 1).astype(F32)
    large = max_exact + (jnp.log(nf / max_exact) / math.log(MAX_DISTANCE / max_exact)
                         * (NUM_BUCKETS - max_exact)).astype(jnp.int32)
    large = jnp.minimum(large, NUM_BUCKETS - 1)
    return jnp.where(n < max_exact, n, large)


def _bias_of_rel(t5_bias, rel):
    bias = jnp.moveaxis(t5_bias.astype(F32)[_t5_bucket(rel)], -1, 0)
    return jnp.where((rel >= 0)[None], bias, NEG_INF)


def _lambda_full(lq1, lk1, lq2, lk2, li, lam_init):
    a = jnp.sum(lq1[li:li + 1, :] * lk1[li:li + 1, :], axis=-1, keepdims=True)
    b = jnp.sum(lq2[li:li + 1, :] * lk2[li:li + 1, :], axis=-1, keepdims=True)
    return jnp.exp(a) - jnp.exp(b) + lam_init


def _softmax_step(s, v, m_sc, l_sc, acc_sc):
    m_prev = m_sc[...]
    m_new = jnp.maximum(m_prev, jnp.max(s, axis=-1, keepdims=True))
    alpha = jnp.exp(m_prev - m_new)
    p = jnp.exp(s - m_new)
    l_sc[...] = alpha * l_sc[...] + jnp.sum(p, axis=-1, keepdims=True)
    acc_sc[...] = alpha * acc_sc[...] + jnp.dot(p.astype(BF16), v, preferred_element_type=F32)
    m_sc[...] = m_new


def _sub_rms(o, g, lam_init):
    y = o * lax.rsqrt(jnp.mean(o * o, axis=-1, keepdims=True) + EPS) * g
    return y * (1.0 - lam_init)


def _pattn_body(q_ref, k_ref, v_ref, bt_ref, lq1, lk1, lq2, lk2, g_ref, o_ref,
                vp_sc, m_sc, acc_sc, *, li, tq, nh, lam_init):
    i = pl.program_id(2)

    @pl.when(i == 0)
    def _():
        for a in range(nh):
            vp_sc[a, :, :KD] = v_ref[0, :, a * KD:(a + 1) * KD]
            vp_sc[a, :, KD:] = jnp.ones((vp_sc.shape[1], KD), BF16)

    m_sc[...] = jnp.full(m_sc.shape, NEG_INF, F32)
    acc_sc[...] = jnp.zeros(acc_sc.shape, F32)
    lane = lax.broadcasted_iota(jnp.int32, (tq, KD), 1)
    qqs = []
    for a in range(nh):
        q = q_ref[0, :, a * KD:(a + 1) * KD]
        zero = jnp.zeros_like(q)
        qqs.append(jnp.concatenate([jnp.where(lane < DIFF_DH, q, zero),
                                    jnp.where(lane >= DIFF_DH, q, zero)], axis=0))

    def block(off, width, bias_of_head):
        for a in range(nh):
            kb = k_ref[0, pl.ds(off, width), a * KD:(a + 1) * KD]
            vb = vp_sc[a, pl.ds(off, width), :]
            s = lax.dot_general(qqs[a], kb, (((1,), (1,)), ((), ())),
                                preferred_element_type=F32) + bias_of_head(a)
            m_prev = m_sc[a]
            m_new = jnp.maximum(m_prev, jnp.max(s, axis=-1, keepdims=True))
            alpha = jnp.exp(m_prev - m_new)
            p = jnp.exp(s - jnp.concatenate([m_new] * (width // LANES), axis=1))
            acc_sc[a] = (jnp.concatenate([alpha, alpha], axis=1) * acc_sc[a]
                         + jnp.dot(p.astype(BF16), vb, preferred_element_type=F32))
            m_sc[a] = m_new

    n_far = jnp.maximum(i - 1, 0)
    far_of_head = lambda a: bt_ref[a, 0, tq - 1:tq, 0:1]

    def far_body(j, carry):
        block(pl.multiple_of(j * 2 * tq, 2 * tq), 2 * tq, far_of_head)
        return carry

    lax.fori_loop(0, n_far // 2, far_body, 0)

    @pl.when(n_far % 2 == 1)
    def _():
        block(pl.multiple_of((n_far - 1) * tq, tq), tq, far_of_head)

    first = (i == 0).astype(jnp.int32)

    def near_of_head(a):
        b = bt_ref[a, first]
        return jnp.concatenate([b, b], axis=0)

    block(pl.multiple_of(n_far * tq, tq), 2 * tq, near_of_head)

    lam = _lambda_full(lq1, lk1, lq2, lk2, li, lam_init)
    g = g_ref[li:li + 1, :]
    for a in range(nh):
        acc = acc_sc[a]
        o = acc[:, :KD] / acc[:, KD:]
        o = o[:tq] - lam * o[tq:]
        o_ref[0, :, a * KD:(a + 1) * KD] = _sub_rms(o, g, lam_init).astype(o_ref.dtype)


def _prompt_attn(qkv, bt, lams, subln_g, *, li, n_heads, lam_init):
    bsz, l, _ = qkv.shape
    tq = bt.shape[2]
    nh = 2 if n_heads % 2 == 0 else 1
    hb = n_heads // nh
    assert l % (2 * tq) == 0
    full = lambda a: pl.BlockSpec(a.shape, lambda b, hh, i: (0,) * a.ndim)
    body = functools.partial(_pattn_body, li=li, tq=tq, nh=nh, lam_init=lam_init)
    return pl.pallas_call(
        body, grid=(bsz, hb, l // tq),
        in_specs=[pl.BlockSpec((1, tq, nh * KD), lambda b, hh, i: (b, i, hh)),
                  pl.BlockSpec((1, l, nh * KD), lambda b, hh, i: (b, 0, hb + hh)),
                  pl.BlockSpec((1, l, nh * KD), lambda b, hh, i: (b, 0, 2 * hb + hh)),
                  pl.BlockSpec((nh, 2, tq, 2 * tq), lambda b, hh, i: (hh, 0, 0, 0))]
                 + [full(a) for a in lams] + [full(subln_g)],
        out_specs=pl.BlockSpec((1, tq, nh * KD), lambda b, hh, i: (b, i, hh)),
        out_shape=jax.ShapeDtypeStruct((bsz, l, n_heads * KD), BF16),
        scratch_shapes=[pltpu.VMEM((nh, l, 2 * KD), BF16), pltpu.VMEM((nh, 2 * tq, LANES), F32),
                        pltpu.VMEM((nh, 2 * tq, 2 * KD), F32)],
        compiler_params=_cparams(("parallel", "parallel", "arbitrary")))(
            qkv, qkv, qkv, bt, *lams, subln_g)


def _sattn_body(pt_ref, qt_ref, *refs, li, pp, n_heads, n_q, lam_init):
    krefs = refs[:pp]
    vrefs = refs[pp:2 * pp]
    kn_ref, vn_ref, bias_ref, lq1, lk1, lq2, lk2, g_ref, o_ref, m_sc, l_sc, acc_sc = refs[2 * pp:]
    jp = pl.program_id(1)
    last = pl.num_programs(1) - 1

    @pl.when(jp == 0)
    def _():
        m_sc[...] = jnp.full(m_sc.shape, NEG_INF, F32)
        l_sc[...] = jnp.zeros(l_sc.shape, F32)
        acc_sc[...] = jnp.zeros(acc_sc.shape, F32)

    def block(kbs, vbs, bias):
        s = jnp.stack([lax.dot_general(qt_ref[0, h], kbs[h], (((1,), (1,)), ((), ())),
                                       preferred_element_type=F32)
                       for h in range(n_heads)]) + bias
        m_prev = m_sc[...]
        m_new = jnp.maximum(m_prev, jnp.max(s, axis=-1, keepdims=True))
        alpha = jnp.exp(m_prev - m_new)
        p = jnp.exp(s - m_new)
        l_sc[...] = alpha * l_sc[...] + jnp.sum(p, axis=-1, keepdims=True)
        pb = p.astype(BF16)
        pv = jnp.stack([jnp.dot(pb[h], vbs[h], preferred_element_type=F32)
                        for h in range(n_heads)])
        acc_sc[...] = alpha * acc_sc[...] + pv
        m_sc[...] = m_new

    def head_rows(page_refs, h):
        rows_h = pl.ds(h, PAGE_SIZE, stride=n_heads)
        return jnp.concatenate([r[0, 0, rows_h, :] for r in page_refs], axis=0).astype(BF16)

    far = bias_ref[0]
    near = jnp.where(jp == last, bias_ref[1], far)
    block([head_rows(krefs, h) for h in range(n_heads)],
          [head_rows(vrefs, h) for h in range(n_heads)],
          jnp.concatenate([far] * (pp - 1) + [near], axis=-1))

    @pl.when(jp == last)
    def _():
        cols = [slice(h * KD, (h + 1) * KD) for h in range(n_heads)]
        block([kn_ref[0, :, c] for c in cols], [vn_ref[0, :, c] for c in cols], bias_ref[2])
        lam = _lambda_full(lq1, lk1, lq2, lk2, li, lam_init)
        g = g_ref[li:li + 1, :]
        o = acc_sc[...] / l_sc[...]
        o = _sub_rms(o[:, :n_q] - lam * o[:, n_q:], g, lam_init)
        for h in range(n_heads):
            o_ref[0, :, cols[h]] = o[h]


def _sample_attn(page_table, qt, cache_k, cache_v, k_new, v_new, bias3, lams, subln_g,
                 *, li, n_heads, n_q, lam_init):
    dbsz, n_pages = page_table.shape
    pp = _pick(n_pages, (PAGES_PER_STEP, 2, 1))
    w = n_heads * KD
    pt_flat = page_table.reshape(-1)

    def page_spec(g):
        return pl.BlockSpec((1, 1, PAGE_SIZE * n_heads, KD),
                            lambda b, j, pt, g=g: (li, pt[b * n_pages + j * pp + g], 0, 0))

    full = lambda a: pl.BlockSpec(a.shape, lambda b, j, pt: (0,) * a.ndim)
    per_b = pl.BlockSpec((1, LANES, w), lambda b, j, pt: (b, 0, 0))
    grid_spec = pltpu.PrefetchScalarGridSpec(
        num_scalar_prefetch=1, grid=(dbsz, n_pages // pp),
        in_specs=[pl.BlockSpec((1, n_heads, 2 * n_q, KD), lambda b, j, pt: (b, 0, 0, 0))]
                 + [page_spec(g) for g in range(pp)] * 2 + [per_b, per_b, full(bias3)]
                 + [full(a) for a in lams] + [full(subln_g)],
        out_specs=pl.BlockSpec((1, n_q, w), lambda b, j, pt: (b, 0, 0)),
        scratch_shapes=[pltpu.VMEM((n_heads, 2 * n_q, 1), F32),
                        pltpu.VMEM((n_heads, 2 * n_q, 1), F32),
                        pltpu.VMEM((n_heads, 2 * n_q, KD), F32)])
    body = functools.partial(_sattn_body, li=li, pp=pp, n_heads=n_heads, n_q=n_q,
                             lam_init=lam_init)
    return pl.pallas_call(
        body, grid_spec=grid_spec,
        out_shape=jax.ShapeDtypeStruct((dbsz, n_q, w), F32),
        compiler_params=_cparams(("parallel", "arbitrary")))(
            pt_flat, qt, *([cache_k] * pp), *([cache_v] * pp), k_new, v_new, bias3,
            *lams, subln_g)


def _block_queries(q, n_heads):
    dbsz, n_q, _ = q.shape
    qh = jnp.transpose(q.reshape(dbsz, n_q, n_heads, KD), (0, 2, 1, 3))
    col_map = (jnp.arange(KD) // DIFF_DH)[None, :] == jnp.arange(2)[:, None]
    t = qh[:, :, None] * col_map[None, None, :, None, :].astype(q.dtype)
    return t.reshape(dbsz, n_heads, 2 * n_q, KD)


def _sample_bias(t5_bias, n_heads, n_q):
    qi = jnp.arange(n_q, dtype=jnp.int32)
    key = jnp.arange(LANES, dtype=jnp.int32)

    def tile(rel):
        b = _bias_of_rel(t5_bias, rel)
        return jnp.concatenate([b, b], axis=1)

    far = tile(jnp.full((n_q, LANES), MAX_DISTANCE, jnp.int32))
    lastp = tile(PAGE_SIZE + qi[:, None] - key[None, :])
    new = tile(jnp.where(key[None, :] < n_q, qi[:, None] - key[None, :], -1))
    return jnp.stack([far, lastp, new])


def _ssd_body(xbc_ref, z_ref, dt_ref, h0_ref, dtb_ref, alog_ref, dx_ref, gn_ref,
              y_ref, hout_ref, hs, ysc, *, n_heads, valid_len):
    c = pl.program_id(1)
    t = SSD_CHUNK
    di = n_heads * SSD_P
    e_per_g = n_heads // SSD_G

    @pl.when(c == 0)
    def _():
        hs[...] = h0_ref[0]

    xbc = xbc_ref[0]
    xs = xbc[:, :di]
    row = lax.broadcasted_iota(jnp.int32, (t, LANES), 0)
    lane = lax.broadcasted_iota(jnp.int32, (t, LANES), 1)
    raw = dt_ref[0] + dtb_ref[...]
    dt = jnp.maximum(raw, 0.0) + jnp.log1p(jnp.exp(-jnp.abs(raw)))
    dt = jnp.where((row + c * t < valid_len) & (lane < n_heads), dt, 0.0)
    da = dt * (-jnp.exp(alog_ref[...]))
    tri = (lax.broadcasted_iota(jnp.int32, (t, t), 0)
           >= lax.broadcasted_iota(jnp.int32, (t, t), 1))
    cum = jnp.dot(tri.astype(F32), da, precision=lax.Precision.HIGHEST,
                  preferred_element_type=F32)
    cum_t = cum.T
    dt_t = dt.T
    cum_last = cum[t - 1:t, :]
    w_end = dt * jnp.exp(cum_last - cum)
    e_cum = jnp.exp(cum)
    e_last = jnp.exp(cum_last)
    xs_bf = xs.astype(BF16)
    xs_t = xs.T.astype(BF16)

    for g in range(SSD_G):
        b_g = xbc[:, di + g * SSD_N:di + (g + 1) * SSD_N]
        c_g = xbc[:, di + (SSD_G + g) * SSD_N:di + (SSD_G + g + 1) * SSD_N]
        cb = lax.dot_general(c_g.astype(BF16), b_g.astype(BF16), (((1,), (1,)), ((), ())),
                             preferred_element_type=F32)
        for e in range(e_per_g):
            h = g * e_per_g + e
            seg = cum[:, h:h + 1] - cum_t[h:h + 1, :]
            decay = jnp.where(tri, jnp.exp(jnp.where(tri, seg, 0.0)), 0.0)
            mat = (cb * decay * dt_t[h:h + 1, :]).astype(BF16)
            x_e = xs_bf[:, h * SSD_P:(h + 1) * SSD_P]
            y_d = jnp.dot(mat, x_e, preferred_element_type=F32)
            h_e = hs[h]
            c_w = (c_g * e_cum[:, h:h + 1]).astype(BF16)
            y_o = lax.dot_general(c_w, h_e.astype(BF16), (((1,), (1,)), ((), ())),
                                  preferred_element_type=F32)
            b_w = (b_g * w_end[:, h:h + 1]).astype(BF16)
            st = jnp.dot(xs_t[h * SSD_P:(h + 1) * SSD_P, :], b_w,
                         preferred_element_type=F32)
            hs[h] = h_e * e_last[:, h:h + 1] + st
            ysc[:, h * SSD_P:(h + 1) * SSD_P] = y_d + y_o

    y = ysc[...] + xs * dx_ref[...]
    z = z_ref[0]
    y = y * (z * jax.nn.sigmoid(z))
    gw = di // SSD_G
    for g in range(SSD_G):
        seg = y[:, g * gw:(g + 1) * gw]
        seg = seg * lax.rsqrt(jnp.mean(seg * seg, axis=-1, keepdims=True) + EPS)
        y_ref[0, :, g * gw:(g + 1) * gw] = (seg * gn_ref[:, g * gw:(g + 1) * gw]).astype(y_ref.dtype)

    @pl.when(c == pl.num_programs(1) - 1)
    def _():
        hout_ref[0] = hs[...]


def _ssd(xbc_act, zxd, z_blk_w, dt_blk, h0, dtb, alog, dx, gn, *, n_heads, valid_len):
    bsz, l, xbc_w = xbc_act.shape
    di = n_heads * SSD_P
    t = SSD_CHUNK
    assert l % t == 0 and z_blk_w == di
    full = lambda a: pl.BlockSpec(a.shape, lambda b, c: (0,) * a.ndim)
    body = functools.partial(_ssd_body, n_heads=n_heads, valid_len=valid_len)
    return pl.pallas_call(
        body, grid=(bsz, l // t),
        in_specs=[pl.BlockSpec((1, t, xbc_w), lambda b, c: (b, c, 0)),
                  pl.BlockSpec((1, t, di), lambda b, c: (b, c, 0)),
                  pl.BlockSpec((1, t, LANES), lambda b, c: (b, c, dt_blk)),
                  pl.BlockSpec((1, n_heads, SSD_P, SSD_N), lambda b, c: (b, 0, 0, 0)),
                  full(dtb), full(alog), full(dx), full(gn)],
        out_specs=[pl.BlockSpec((1, t, di), lambda b, c: (b, c, 0)),
                   pl.BlockSpec((1, n_heads, SSD_P, SSD_N), lambda b, c: (b, 0, 0, 0))],
        out_shape=[jax.ShapeDtypeStruct((bsz, l, di), BF16),
                   jax.ShapeDtypeStruct((bsz, n_heads, SSD_P, SSD_N), F32)],
        scratch_shapes=[pltpu.VMEM((n_heads, SSD_P, SSD_N), F32), pltpu.VMEM((t, di), F32)],
        compiler_params=_cparams(("parallel", "arbitrary")))(
            xbc_act, zxd, zxd, h0, dtb, alog, dx, gn)


def _epi_glu(accs, tiles, rows, j):
    a, b = accs
    return (a * jax.nn.sigmoid(b),)


def _epi_id(accs, tiles, rows, j):
    return (accs[0],)


def _epi_merge(accs, tiles, rows, j):
    ya, yb, yc = accs
    ga, gb, gc = tiles
    return (jax.nn.sigmoid(ga) * ya + jax.nn.sigmoid(gb) * yb + jax.nn.sigmoid(gc) * yc,)


def _epi_resid(accs, tiles, rows, j):
    return (tiles[0] + rows[0] * accs[0],)


def _trunk(x, mod, prev_a, prev_c, prev_h, prev_f, wts, p, dims, *, paged=None):
    bsz, l, d = x.shape
    m = bsz * l
    depth, n_heads, conv_c, ssd_h, d_ff = (dims[k] for k in
                                           ("depth", "n_heads", "conv_c", "ssd_h", "d_ff"))
    att_w = n_heads * KD
    ssd_di = ssd_h * SSD_P
    ssd_xbc = ssd_di + 2 * SSD_G * SSD_N
    conv_k = p["conv_a_w"].shape[1]
    ssd_k = p["ssd_conv_w"].shape[1]
    ffn_k = p["ffn_conv_w"].shape[1]
    prompt = paged is None
    tm = _pick(l, (512, 256, 128)) if prompt else m
    tpb = l // tm if prompt else 1
    t_seq = _pick(l, (128,)) if prompt else l
    lams = (p["lambda_q1"], p["lambda_k1"], p["lambda_q2"], p["lambda_k2"])
    scale = DIFF_DH ** -0.5
    big = (1024, 512, 256, 128)

    xf = x.reshape(m, d)
    out_a, out_c, out_h, out_f = [], [], [], []
    kst = jnp.zeros((depth, m, att_w), F32)
    vst = jnp.zeros((depth, m, att_w), F32)
    for li in range(depth):
        w = wts[li]
        md = mod[li]
        lam_init = 0.8 - 0.6 * math.exp(-0.3 * li)
        h = _norm_mod(xf, p["norm1_g"][li][None], md, 1, 0, tm=tm, tpb=tpb)

        tn = _pick(conv_c, big)
        (u,) = _mm([h, h], [(w["glu"], 0), (w["glu"], conv_c // tn)], _epi_glu, [F32], conv_c,
                   tm=tm, tn=tn)
        tn = _pick(att_w, big)
        qkv_bf, kst, vst = _qkv(h, w["qkv"], kst, vst, li=li, scale=scale, tm=tm, tn=tn)
        zxd_w = w["zxd"].shape[1]
        tn = _pick(zxd_w, (896, 640, 512, 384, 256, 128))
        (zxd,) = _mm([h], [(w["zxd"], 0)], _epi_id, [F32], zxd_w, tm=tm, tn=tn)
        tn = _pick(3 * d, big)
        (gates,) = _mm([h], [(w["gates"], 0)], _epi_id, [F32], 3 * d, tm=tm, tn=tn)

        u3 = u.reshape(bsz, l, conv_c)
        out_a.append(_conv_state(prev_a[li], u3, conv_k - 1))
        a_act = _dwconv(u3, 0, _pad_prev(prev_a[li], 32), p["conv_a_w"][li],
                        p["conv_a_b"][li][None], _epi_conv_a, BF16, c=conv_c, t=t_seq,
                        tc=conv_c, halo=32,
                        prows=(p["ln_a_g"][li][None], p["ln_a_b"][li][None]))

        if prompt:
            o = _prompt_attn(qkv_bf.reshape(bsz, l, 3 * att_w), dims["bias_tiles"], lams,
                             p["subln_g"], li=li, n_heads=n_heads, lam_init=lam_init)
        else:
            qb = qkv_bf.reshape(bsz, l, 3 * att_w)
            qt = _block_queries(qb[..., :att_w], n_heads)
            pad = ((0, 0), (0, LANES - l), (0, 0))
            k_new = jnp.pad(qb[..., att_w:2 * att_w], pad)
            v_new = jnp.pad(qb[..., 2 * att_w:], pad)
            o = _sample_attn(paged["page_table"], qt, paged["cache_k"], paged["cache_v"],
                             k_new, v_new, dims["bias3"], lams, p["subln_g"], li=li,
                             n_heads=n_heads, n_q=l, lam_init=lam_init).astype(BF16)
        o = o.reshape(m, att_w)

        zxd3 = zxd.reshape(bsz, l, zxd_w)
        xbc_raw = zxd3[..., ssd_di:ssd_di + ssd_xbc]
        out_c.append(_conv_state(prev_c[li], xbc_raw, ssd_k - 1))
        tc = _pick(math.gcd(ssd_di, ssd_xbc), (512, 256, 128))
        xbc_act = _dwconv(zxd3, ssd_di // tc, _pad_prev(prev_c[li], 8), p["ssd_conv_w"][li],
                          p["ssd_conv_b"][li][None], _epi_silu, F32, c=ssd_xbc, t=t_seq, tc=tc,
                          halo=8)
        if prompt:
            xbc_in, zxd_in = xbc_act, zxd3
        else:
            pad = ((0, 0), (0, SSD_CHUNK - l), (0, 0))
            xbc_in, zxd_in = jnp.pad(xbc_act, pad), jnp.pad(zxd3, pad)
        lane_pad = lambda a: jnp.pad(a, (0, LANES - a.shape[0]))[None]
        y_c, h_t = _ssd(xbc_in, zxd_in, ssd_di, (ssd_di + ssd_xbc) // LANES, prev_h[li],
                        lane_pad(p["dt_bias"][li]), lane_pad(p["a_log"][li]),
                        jnp.repeat(p["d_skip"][li], SSD_P)[None], p["ssd_norm_g"][li][None],
                        n_heads=ssd_h, valid_len=l)
        out_h.append(h_t)
        y_c = y_c[:, :l].reshape(m, ssd_di)

        tn = _pick(d, big)
        nb = d // tn
        (merged,) = _mm([a_act.reshape(m, conv_c), o, y_c],
                        [(w["proj_a"], 0), (w["proj_b"], 0), (w["proj_c"], 0)], _epi_merge,
                        [BF16], d, tiles=[(gates, 0), (gates, nb), (gates, 2 * nb)], tm=tm, tn=tn)
        (xf,) = _mm([merged], [(w["out"], 0)], _epi_resid, [F32], d, tiles=[(xf, 0)],
                    rows=[(md, 2 * nb)], tm=tm, tn=tn, tpb=tpb)

        h2 = _norm_mod(xf, p["norm2_g"][li][None], md, 4, 3, tm=tm, tpb=tpb)
        if prompt:
            act, st = _upconv(h2, w["up"], _pad_prev(prev_f[li], UP_HALO), p["ffn_conv_w"][li],
                              p["ffn_conv_b"][li][None], d_ff=d_ff, tm=tm,
                              tn=_pick(d_ff, (512, 256, 128)), tpb=tpb)
            out_f.append(st[:, UP_HALO - (ffn_k - 1):])
        else:
            tn = _pick(d_ff, big)
            (up,) = _mm([h2], [(w["up"], 0)], _epi_id, [F32], 2 * d_ff, tm=tm, tn=tn)
            up3 = up.reshape(bsz, l, 2 * d_ff)
            out_f.append(_conv_state(prev_f[li], up3[..., :d_ff], ffn_k - 1))
            tc = _pick(d_ff, (512, 256, 128))
            act = _dwconv(up3, 0, _pad_prev(prev_f[li], 8), p["ffn_conv_w"][li],
                          p["ffn_conv_b"][li][None], _epi_ffn, BF16, c=d_ff, t=t_seq, tc=tc,
                          halo=8, xtiles=[(up3, d_ff // tc)])
        tn = _pick(d, big)
        (xf,) = _mm([act.reshape(m, d_ff)], [(w["down"], 0)], _epi_resid, [F32], d,
                    tiles=[(xf, 0)], rows=[(md, 5 * (d // tn))], tm=tm, tn=tn, tpb=tpb)

    y_out = _rms(xf, p["final_g"][None], tm=tm).reshape(bsz, l, d)
    kv_shape = (depth, bsz, l, n_heads, KD)
    return (y_out, kst.reshape(kv_shape), vst.reshape(kv_shape), jnp.stack(out_a), jnp.stack(out_c),
            jnp.stack(out_h), jnp.stack(out_f))


def _split_weights(p, li, dims):
    conv_c, att_w = dims["conv_c"], dims["n_heads"] * KD
    ssd_di = dims["ssd_h"] * SSD_P
    ssd_xbc = ssd_di + 2 * SSD_G * SSD_N
    w_in = p["w_in"][li]
    c0 = 2 * conv_c
    c1 = c0 + 3 * att_w
    c2 = c1 + ssd_di + ssd_xbc
    c3 = c2 + dims["ssd_h"]
    dt_cols = jnp.pad(w_in[:, c2:c3], ((0, 0), (0, LANES - dims["ssd_h"])))
    bf = lambda a: a.astype(BF16)
    return dict(glu=bf(w_in[:, :c0]), qkv=bf(w_in[:, c0:c1]),
                zxd=bf(jnp.concatenate([w_in[:, c1:c2], dt_cols], axis=1)),
                gates=bf(w_in[:, c3:]), proj_a=bf(p["w_proj_a"][li]), proj_b=bf(p["w_proj_b"][li]),
                proj_c=bf(p["w_proj_c"][li]), out=bf(p["w_out"][li]), up=bf(p["ffn_up"][li]),
                down=bf(p["ffn_down"][li]))


def kernel(x_prompt, x_sample, cache_k, cache_v, page_table, state_conv_a, state_ssd_conv,
           state_ssm, state_ffn_conv, c_prompt, c_sample, w_ada, b_ada, norm1_g, w_in,
           conv_a_w, conv_a_b, ln_a_g, ln_a_b, w_proj_a, t5_bias, lambda_q1, lambda_k1,
           lambda_q2, lambda_k2, subln_g, w_proj_b, ssd_conv_w, ssd_conv_b, dt_bias, a_log,
           d_skip, ssd_norm_g, w_proj_c, w_out, norm2_g, ffn_up, ffn_conv_w, ffn_conv_b,
           ffn_down, final_g):
    p = dict(w_ada=w_ada, b_ada=b_ada, norm1_g=norm1_g, w_in=w_in, conv_a_w=conv_a_w,
             conv_a_b=conv_a_b, ln_a_g=ln_a_g, ln_a_b=ln_a_b, w_proj_a=w_proj_a, t5_bias=t5_bias,
             lambda_q1=lambda_q1, lambda_k1=lambda_k1, lambda_q2=lambda_q2, lambda_k2=lambda_k2,
             subln_g=subln_g, w_proj_b=w_proj_b, ssd_conv_w=ssd_conv_w, ssd_conv_b=ssd_conv_b,
             dt_bias=dt_bias, a_log=a_log, d_skip=d_skip, ssd_norm_g=ssd_norm_g, w_proj_c=w_proj_c,
             w_out=w_out, norm2_g=norm2_g, ffn_up=ffn_up, ffn_conv_w=ffn_conv_w,
             ffn_conv_b=ffn_conv_b, ffn_down=ffn_down, final_g=final_g)
    bp, lp, d = x_prompt.shape
    bs, ls, _ = x_sample.shape
    depth = w_ada.shape[0]
    n_heads = cache_k.shape[3]
    n_pages = page_table.shape[1]
    dims = dict(depth=depth, n_heads=n_heads, conv_c=conv_a_w.shape[2], ssd_h=dt_bias.shape[1],
                d_ff=ffn_conv_w.shape[2])
    f32 = x_prompt.dtype

    tq = _pick(lp, (ATTN_TQ, 128))
    assert tq >= MAX_DISTANCE
    qi = jnp.arange(tq, dtype=jnp.int32)
    rel0 = qi[:, None] - qi[None, :]
    b_diag, b_prev = _bias_of_rel(t5_bias, rel0), _bias_of_rel(t5_bias, rel0 + tq)
    dims["bias_tiles"] = jnp.stack(
        [jnp.concatenate([b_prev, b_diag], axis=-1),
         jnp.concatenate([b_diag, jnp.full_like(b_diag, NEG_INF)], axis=-1)], axis=1)
    dims["bias3"] = _sample_bias(t5_bias, n_heads, ls)

    rows = -(-(bp + bs) // 8) * 8
    c_all = jnp.pad(jnp.concatenate([c_prompt, c_sample], axis=0), ((0, rows - bp - bs), (0, 0)))
    mod = _ada(c_all, w_ada, b_ada)
    mod_p = mod[:, :bp, None, :]
    mod_s = jnp.repeat(mod[:, bp:bp + bs], ls, axis=1)[:, None]

    wts = [_split_weights(p, li, dims) for li in range(depth)]
    zero = lambda *s: jnp.zeros((depth, bp) + s, f32)
    res_p = _trunk(x_prompt, mod_p, zero(conv_a_w.shape[1] - 1, dims["conv_c"]),
                   zero(ssd_conv_w.shape[1] - 1, ssd_conv_w.shape[2]),
                   zero(dims["ssd_h"], SSD_P, SSD_N), zero(ffn_conv_w.shape[1] - 1, dims["d_ff"]),
                   wts, p, dims)
    paged = dict(page_table=page_table,
                 cache_k=cache_k.reshape(depth, -1, PAGE_SIZE * n_heads, KD),
                 cache_v=cache_v.reshape(depth, -1, PAGE_SIZE * n_heads, KD))
    res_s = _trunk(x_sample, mod_s, state_conv_a, state_ssd_conv, state_ssm, state_ffn_conv,
                   wts, p, dims, paged=paged)
    y_p, k_p, v_p, a_p, c_p, h_p, f_p = res_p
    y_s, k_s, v_s, a_s, c_s, h_s, f_s = res_s
    return (y_p, y_s, k_p, v_p, k_s, v_s, a_p, a_s, c_p, c_s, h_p, h_s, f_p, f_s)
```

```python
import functools
import math

import jax
import jax.numpy as jnp
from jax import lax
from jax.experimental import pallas as pl
from jax.experimental.pallas import tpu as pltpu

F32 = jnp.float32
BF16 = jnp.bfloat16

EPS = 1e-6
LN_EPS = 1e-5
NEG_INF = -1e30
DIFF_DH = 64
KD = 2 * DIFF_DH
NUM_BUCKETS = 32
MAX_DISTANCE = 128
SSD_P = 64
SSD_N = 128
SSD_G = 2
SSD_CHUNK = 128
PAGE_SIZE = 128
LANES = 128
VMEM_LIMIT = 56 * 1024 * 1024

ATTN_TQ = 256
PAGES_PER_STEP = 8


def _cparams(sem):
    return pltpu.CompilerParams(dimension_semantics=sem, vmem_limit_bytes=VMEM_LIMIT)


def _pick(n, cands):
    for c in cands:
        if n % c == 0:
            return c
    return n


def _mm_body(*refs, nl, nt, nr, epi):
    lrefs = refs[:nl]
    wrefs = refs[nl:2 * nl]
    trefs = refs[2 * nl:2 * nl + nt]
    rrefs = refs[2 * nl + nt:2 * nl + nt + nr]
    orefs = refs[2 * nl + nt + nr:]
    accs = [jnp.dot(l[...], w[...], preferred_element_type=F32) for l, w in zip(lrefs, wrefs)]
    res = epi(accs, [t[...] for t in trefs], [r[0] for r in rrefs], pl.program_id(1))
    for o, r in zip(orefs, res):
        o[...] = r.astype(o.dtype)


def _mm(lhs, rhs, epi, out_dtypes, n, *, tiles=(), rows=(), tm, tn, tpb=1):
    m = lhs[0].shape[0]
    assert m % tm == 0 and n % tn == 0
    in_specs, args = [], []
    for a in lhs:
        in_specs.append(pl.BlockSpec((tm, a.shape[1]), lambda i, j: (i, 0)))
        args.append(a)
    for a, off in rhs:
        in_specs.append(pl.BlockSpec((a.shape[0], tn), lambda i, j, off=off: (0, j + off)))
        args.append(a)
    for a, off in tiles:
        in_specs.append(pl.BlockSpec((tm, tn), lambda i, j, off=off: (i, j + off)))
        args.append(a)
    for a, off in rows:
        in_specs.append(pl.BlockSpec((1, a.shape[1], tn),
                                     lambda i, j, off=off: (i // tpb, 0, j + off)))
        args.append(a)
    out_specs = [pl.BlockSpec((tm, tn), lambda i, j: (i, j)) for _ in out_dtypes]
    out_shape = [jax.ShapeDtypeStruct((m, n), d) for d in out_dtypes]
    body = functools.partial(_mm_body, nl=len(lhs), nt=len(tiles), nr=len(rows), epi=epi)
    return pl.pallas_call(
        body, grid=(m // tm, n // tn), in_specs=in_specs, out_specs=out_specs,
        out_shape=out_shape, compiler_params=_cparams(("parallel", "arbitrary")))(*args)


def _qkv_body(h_ref, w_ref, kin_ref, vin_ref, qkv_ref, kst_ref, vst_ref, *, nq, scale):
    del kin_ref, vin_ref
    j = pl.program_id(1)
    acc = jnp.dot(h_ref[...], w_ref[...], preferred_element_type=F32)
    qkv_ref[...] = (acc * jnp.where(j < nq, scale, 1.0)).astype(qkv_ref.dtype)

    @pl.when((j >= nq) & (j < 2 * nq))
    def _():
        kst_ref[0] = acc

    @pl.when(j >= 2 * nq)
    def _():
        vst_ref[0] = acc


def _qkv(h, w, kst, vst, *, li, scale, tm, tn):
    m, d = h.shape
    wdt = w.shape[1] // 3
    nq = wdt // tn
    assert m % tm == 0 and wdt % tn == 0
    k_idx = lambda i, j: (li, i, jnp.clip(j - nq, 0, nq - 1))
    v_idx = lambda i, j: (li, i, jnp.clip(j - 2 * nq, 0, nq - 1))
    body = functools.partial(_qkv_body, nq=nq, scale=scale)
    return pl.pallas_call(
        body, grid=(m // tm, 3 * nq),
        in_specs=[pl.BlockSpec((tm, d), lambda i, j: (i, 0)),
                  pl.BlockSpec((d, tn), lambda i, j: (0, j)),
                  pl.BlockSpec(memory_space=pl.ANY), pl.BlockSpec(memory_space=pl.ANY)],
        out_specs=[pl.BlockSpec((tm, tn), lambda i, j: (i, j)),
                   pl.BlockSpec((1, tm, tn), k_idx), pl.BlockSpec((1, tm, tn), v_idx)],
        out_shape=[jax.ShapeDtypeStruct((m, 3 * wdt), BF16),
                   jax.ShapeDtypeStruct(kst.shape, F32), jax.ShapeDtypeStruct(vst.shape, F32)],
        input_output_aliases={2: 1, 3: 2},
        compiler_params=_cparams(("arbitrary", "arbitrary")))(h, w, kst, vst)


def _ada_body(c_ref, w_ref, b_ref, o_ref):
    c = c_ref[...]
    s = (c * jax.nn.sigmoid(c)).astype(BF16)
    o_ref[0] = jnp.dot(s, w_ref[0].astype(BF16), preferred_element_type=F32) + b_ref[0]


def _ada(c_all, w_ada, b_ada):
    depth, d, n6 = w_ada.shape
    rows = c_all.shape[0]
    tn = _pick(n6, (1024, 512, 256, 128))
    return pl.pallas_call(
        _ada_body, grid=(depth, n6 // tn),
        in_specs=[pl.BlockSpec((rows, d), lambda l, j: (0, 0)),
                  pl.BlockSpec((1, d, tn), lambda l, j: (l, 0, j)),
                  pl.BlockSpec((1, 1, tn), lambda l, j: (l, 0, j))],
        out_specs=pl.BlockSpec((1, rows, tn), lambda l, j: (l, 0, j)),
        out_shape=jax.ShapeDtypeStruct((depth, rows, n6), F32),
        compiler_params=_cparams(("arbitrary", "arbitrary")))(c_all, w_ada, b_ada[:, None, :])


def _norm_mod_body(x_ref, g_ref, sc_ref, sh_ref, o_ref):
    x = x_ref[...]
    y = x * lax.rsqrt(jnp.mean(x * x, axis=-1, keepdims=True) + EPS) * g_ref[...]
    o_ref[...] = (y * (1.0 + sc_ref[0]) + sh_ref[0]).astype(o_ref.dtype)


def _norm_mod(x, g, mod, sc_blk, sh_blk, *, tm, tpb):
    m, d = x.shape
    r = mod.shape[1]
    return pl.pallas_call(
        _norm_mod_body, grid=(m // tm,),
        in_specs=[pl.BlockSpec((tm, d), lambda i: (i, 0)),
                  pl.BlockSpec((1, d), lambda i: (0, 0)),
                  pl.BlockSpec((1, r, d), lambda i: (i // tpb, 0, sc_blk)),
                  pl.BlockSpec((1, r, d), lambda i: (i // tpb, 0, sh_blk))],
        out_specs=pl.BlockSpec((tm, d), lambda i: (i, 0)),
        out_shape=jax.ShapeDtypeStruct((m, d), BF16),
        compiler_params=_cparams(("parallel",)))(x, g, mod, mod)


def _rms_body(x_ref, g_ref, o_ref):
    x = x_ref[...]
    o_ref[...] = x * lax.rsqrt(jnp.mean(x * x, axis=-1, keepdims=True) + EPS) * g_ref[...]


def _rms(x, g, *, tm):
    m, d = x.shape
    return pl.pallas_call(
        _rms_body, grid=(m // tm,),
        in_specs=[pl.BlockSpec((tm, d), lambda i: (i, 0)),
                  pl.BlockSpec((1, d), lambda i: (0, 0))],
        out_specs=pl.BlockSpec((tm, d), lambda i: (i, 0)),
        out_shape=jax.ShapeDtypeStruct((m, d), F32),
        compiler_params=_cparams(("parallel",)))(x, g)


def _dwconv_body(*refs, kc, halo, t, tc, n_seq_tiles, nx, npr, epi):
    u_ref, prev_ref, w_ref, b_ref = refs[:4]
    xrefs = refs[4:4 + nx]
    prefs = refs[4 + nx:4 + nx + npr]
    o_ref = refs[4 + nx + npr]
    win, cv = refs[4 + nx + npr + 1:]

    @pl.when(pl.program_id(2) == 0)
    def _():
        win[0:halo, :] = prev_ref[0]

    win[halo:halo + t, :] = u_ref[0]
    base = halo - (kc - 1)
    rc = min(t, 64)
    cc = min(tc, 256)
    for r0 in range(0, t, rc):
        for c0 in range(0, tc, cc):
            acc = jnp.broadcast_to(b_ref[:, c0:c0 + cc], (rc, cc))
            for r in range(min(8, kc)):
                taps = list(range(r, kc, 8))
                span = rc + 8 * (len(taps) - 1)
                start = base + r0 + r
                sh = win[start:start + span, c0:c0 + cc]
                for ai, j in enumerate(taps):
                    acc = acc + w_ref[j:j + 1, c0:c0 + cc] * sh[8 * ai:8 * ai + rc]
            cv[r0:r0 + rc, c0:c0 + cc] = acc
    o_ref[0] = epi(cv[...], [x[0] for x in xrefs], [p[...] for p in prefs]).astype(o_ref.dtype)
    if n_seq_tiles > 1:
        win[0:halo, :] = win[t:t + halo, :]


def _dwconv(u, u_blk, prev, w, b, epi, out_dtype, *, c, t, tc, halo, xtiles=(), prows=()):
    bsz, l, _ = u.shape
    kc = w.shape[0]
    assert l % t == 0 and c % tc == 0 and halo >= kc - 1 and (t >= halo or l == t)
    nlt = l // t
    in_specs = [pl.BlockSpec((1, t, tc), lambda bi, ci, li: (bi, li, ci + u_blk)),
                pl.BlockSpec((1, halo, tc), lambda bi, ci, li: (bi, 0, ci)),
                pl.BlockSpec((kc, tc), lambda bi, ci, li: (0, ci)),
                pl.BlockSpec((1, tc), lambda bi, ci, li: (0, ci))]
    args = [u, prev, w, b]
    for a, off in xtiles:
        in_specs.append(pl.BlockSpec((1, t, tc), lambda bi, ci, li, off=off: (bi, li, ci + off)))
        args.append(a)
    for a in prows:
        in_specs.append(pl.BlockSpec((1, tc), lambda bi, ci, li: (0, ci)))
        args.append(a)
    body = functools.partial(_dwconv_body, kc=kc, halo=halo, t=t, tc=tc, n_seq_tiles=nlt,
                             nx=len(xtiles), npr=len(prows), epi=epi)
    return pl.pallas_call(
        body, grid=(bsz, c // tc, nlt), in_specs=in_specs,
        out_specs=pl.BlockSpec((1, t, tc), lambda bi, ci, li: (bi, li, ci)),
        out_shape=jax.ShapeDtypeStruct((bsz, l, c), out_dtype),
        scratch_shapes=[pltpu.VMEM((halo + t, tc), F32), pltpu.VMEM((t, tc), F32)],
        compiler_params=_cparams(("parallel", "parallel", "arbitrary")))(*args)


def _epi_conv_a(cv, xs, ps):
    g, b = ps
    mu = jnp.mean(cv, axis=-1, keepdims=True)
    d = cv - mu
    var = jnp.mean(d * d, axis=-1, keepdims=True)
    y = d * lax.rsqrt(var + LN_EPS) * g + b
    return y * jax.nn.sigmoid(y)


def _epi_silu(cv, xs, ps):
    return cv * jax.nn.sigmoid(cv)


def _epi_ffn(cv, xs, ps):
    return jax.nn.gelu(cv) * xs[0]


UP_HALO = 8


def _upconv_body(h_ref, wa_ref, wb_ref, prev_ref, cw_ref, cb_ref, act_ref,
                 win, carry, *, tm, tn, tpb, kc):
    i = pl.program_id(0)
    j = pl.program_id(1)
    first = i % tpb == 0

    @pl.when(first)
    def _():
        win[0:UP_HALO, :] = prev_ref[0]

    @pl.when(jnp.logical_not(first))
    def _():
        win[0:UP_HALO, :] = carry[j]

    base = UP_HALO - (kc - 1)
    rm = min(tm, 256)
    rc = min(rm, 64)
    cc = min(tn, 256)
    for m0 in range(0, tm, rm):
        h = h_ref[m0:m0 + rm, :]
        fa = jnp.dot(h, wa_ref[...], preferred_element_type=F32)
        fb = jnp.dot(h, wb_ref[...], preferred_element_type=F32)
        win[UP_HALO + m0:UP_HALO + m0 + rm, :] = fa
        for r0 in range(0, rm, rc):
            for c0 in range(0, tn, cc):
                acc = jnp.broadcast_to(cb_ref[:, c0:c0 + cc], (rc, cc))
                for t in range(kc):
                    start = base + m0 + r0 + t
                    acc = acc + cw_ref[t:t + 1, c0:c0 + cc] * win[start:start + rc, c0:c0 + cc]
                act_ref[m0 + r0:m0 + r0 + rc, c0:c0 + cc] = (
                    jax.nn.gelu(acc) * fb[r0:r0 + rc, c0:c0 + cc]).astype(act_ref.dtype)
    carry[j] = win[tm:tm + UP_HALO, :]


def _upconv(h, w_up, prev, cw, cb, *, d_ff, tm, tn, tpb):
    m, d = h.shape
    kc = cw.shape[0]
    nj = d_ff // tn
    assert m % tm == 0 and d_ff % tn == 0 and kc - 1 <= UP_HALO <= tm
    body = functools.partial(_upconv_body, tm=tm, tn=tn, tpb=tpb, kc=kc)
    return pl.pallas_call(
        body, grid=(m // tm, nj),
        in_specs=[pl.BlockSpec((tm, d), lambda i, j: (i, 0)),
                  pl.BlockSpec((d, tn), lambda i, j: (0, j)),
                  pl.BlockSpec((d, tn), lambda i, j: (0, j + nj)),
                  pl.BlockSpec((1, UP_HALO, tn), lambda i, j: (i // tpb, 0, j)),
                  pl.BlockSpec((kc, tn), lambda i, j: (0, j)),
                  pl.BlockSpec((1, tn), lambda i, j: (0, j))],
        out_specs=pl.BlockSpec((tm, tn), lambda i, j: (i, j)),
        out_shape=jax.ShapeDtypeStruct((m, d_ff), BF16),
        scratch_shapes=[pltpu.VMEM((UP_HALO + tm, tn), F32), pltpu.VMEM((nj, UP_HALO, tn), F32)],
        compiler_params=_cparams(("arbitrary", "arbitrary")))(h, w_up, w_up, prev, cw, cb)


def _conv_state(prev, u, k1):
    l = u.shape[1]
    if l >= k1:
        return u[:, l - k1:]
    return jnp.concatenate([prev[:, l:], u], axis=1)


def _pad_prev(prev, halo):
    return jnp.pad(prev, ((0, 0), (halo - prev.shape[1], 0), (0, 0)))


def _t5_bucket(rel):
    n = jnp.maximum(rel, 0)
    max_exact = NUM_BUCKETS // 2
    nf = jnp.maximum(n, 1).astype(F32)
    large = max_exact + (jnp.log(nf / max_exact) / math.log(MAX_DISTANCE / max_exact)
                         * (NUM_BUCKETS - max_exact)).astype(jnp.int32)
    large = jnp.minimum(large, NUM_BUCKETS - 1)
    return jnp.where(n < max_exact, n, large)


def _bias_of_rel(t5_bias, rel):
    bucket = _t5_bucket(rel)
    table = t5_bias.astype(F32)
    expand = (slice(None),) + (None,) * rel.ndim
    bias = jnp.zeros((table.shape[1],) + rel.shape, F32)
    for b in range(NUM_BUCKETS):
        bias = jnp.where((bucket == b)[None], table[b][expand], bias)
    return jnp.where((rel >= 0)[None], bias, NEG_INF)


def _lambda_full(lq1, lk1, lq2, lk2, li, lam_init):
    a = jnp.sum(lq1[li:li + 1, :] * lk1[li:li + 1, :], axis=-1, keepdims=True)
    b = jnp.sum(lq2[li:li + 1, :] * lk2[li:li + 1, :], axis=-1, keepdims=True)
    return jnp.exp(a) - jnp.exp(b) + lam_init


def _softmax_step(s, v, m_sc, l_sc, acc_sc):
    m_prev = m_sc[...]
    m_new = jnp.maximum(m_prev, jnp.max(s, axis=-1, keepdims=True))
    alpha = jnp.exp(m_prev - m_new)
    p = jnp.exp(s - m_new)
    l_sc[...] = alpha * l_sc[...] + jnp.sum(p, axis=-1, keepdims=True)
    acc_sc[...] = alpha * acc_sc[...] + jnp.dot(p.astype(BF16), v, preferred_element_type=F32)
    m_sc[...] = m_new


def _sub_rms(o, g, lam_init):
    y = o * lax.rsqrt(jnp.mean(o * o, axis=-1, keepdims=True) + EPS) * g
    return y * (1.0 - lam_init)


def _pattn_body(q_ref, k_ref, v_ref, bt_ref, lq1, lk1, lq2, lk2, g_ref, o_ref,
                vp_sc, m_sc, acc_sc, *, li, tq, nh, lam_init):
    i = pl.program_id(2)

    @pl.when(i == 0)
    def _():
        for a in range(nh):
            vp_sc[a, :, :KD] = v_ref[0, :, a * KD:(a + 1) * KD]
            vp_sc[a, :, KD:] = jnp.ones((vp_sc.shape[1], KD), BF16)

    m_sc[...] = jnp.full(m_sc.shape, NEG_INF, F32)
    acc_sc[...] = jnp.zeros(acc_sc.shape, F32)
    lane = lax.broadcasted_iota(jnp.int32, (tq, KD), 1)
    qqs = []
    for a in range(nh):
        q = q_ref[0, :, a * KD:(a + 1) * KD]
        zero = jnp.zeros_like(q)
        qqs.append(jnp.concatenate([jnp.where(lane < DIFF_DH, q, zero),
                                    jnp.where(lane >= DIFF_DH, q, zero)], axis=0))

    def block(off, width, bias_of_head):
        for a in range(nh):
            kb = k_ref[0, pl.ds(off, width), a * KD:(a + 1) * KD]
            vb = vp_sc[a, pl.ds(off, width), :]
            s = lax.dot_general(qqs[a], kb, (((1,), (1,)), ((), ())),
                                preferred_element_type=F32) + bias_of_head(a)
            m_prev = m_sc[a]
            m_new = jnp.maximum(m_prev, jnp.max(s, axis=-1, keepdims=True))
            alpha = jnp.exp(m_prev - m_new)
            p = jnp.exp(s - jnp.concatenate([m_new] * (width // LANES), axis=1))
            acc_sc[a] = (jnp.concatenate([alpha, alpha], axis=1) * acc_sc[a]
                         + jnp.dot(p.astype(BF16), vb, preferred_element_type=F32))
            m_sc[a] = m_new

    n_far = jnp.maximum(i - 1, 0)
    far_of_head = lambda a: bt_ref[a, 0, tq - 1:tq, 0:1]

    def far_body(j, carry):
        block(pl.multiple_of(j * 2 * tq, 2 * tq), 2 * tq, far_of_head)
        return carry

    lax.fori_loop(0, n_far // 2, far_body, 0)

    @pl.when(n_far % 2 == 1)
    def _():
        block(pl.multiple_of((n_far - 1) * tq, tq), tq, far_of_head)

    first = (i == 0).astype(jnp.int32)

    def near_of_head(a):
        b = bt_ref[a, first]
        return jnp.concatenate([b, b], axis=0)

    block(pl.multiple_of(n_far * tq, tq), 2 * tq, near_of_head)

    lam = _lambda_full(lq1, lk1, lq2, lk2, li, lam_init)
    g = g_ref[li:li + 1, :]
    for a in range(nh):
        acc = acc_sc[a]
        o = acc[:, :KD] / acc[:, KD:]
        o = o[:tq] - lam * o[tq:]
        o_ref[0, :, a * KD:(a + 1) * KD] = _sub_rms(o, g, lam_init).astype(o_ref.dtype)


def _prompt_attn(qkv, bt, lams, subln_g, *, li, n_heads, lam_init):
    bsz, l, _ = qkv.shape
    tq = bt.shape[2]
    nh = _pick(n_heads, (4, 2, 1))
    hb = n_heads // nh
    assert l % (2 * tq) == 0
    full = lambda a: pl.BlockSpec(a.shape, lambda b, hh, i: (0,) * a.ndim)
    body = functools.partial(_pattn_body, li=li, tq=tq, nh=nh, lam_init=lam_init)
    return pl.pallas_call(
        body, grid=(bsz, hb, l // tq),
        in_specs=[pl.BlockSpec((1, tq, nh * KD), lambda b, hh, i: (b, i, hh)),
                  pl.BlockSpec((1, l, nh * KD), lambda b, hh, i: (b, 0, hb + hh)),
                  pl.BlockSpec((1, l, nh * KD), lambda b, hh, i: (b, 0, 2 * hb + hh)),
                  pl.BlockSpec((nh, 2, tq, 2 * tq), lambda b, hh, i: (hh, 0, 0, 0))]
                 + [full(a) for a in lams] + [full(subln_g)],
        out_specs=pl.BlockSpec((1, tq, nh * KD), lambda b, hh, i: (b, i, hh)),
        out_shape=jax.ShapeDtypeStruct((bsz, l, n_heads * KD), BF16),
        scratch_shapes=[pltpu.VMEM((nh, l, 2 * KD), BF16), pltpu.VMEM((nh, 2 * tq, LANES), F32),
                        pltpu.VMEM((nh, 2 * tq, 2 * KD), F32)],
        compiler_params=_cparams(("parallel", "parallel", "arbitrary")))(
            qkv, qkv, qkv, bt, *lams, subln_g)


def _sattn_body(pt_ref, qt_ref, *refs, li, pp, n_heads, n_q, lam_init):
    krefs = refs[:pp]
    vrefs = refs[pp:2 * pp]
    kn_ref, vn_ref, bias_ref, lq1, lk1, lq2, lk2, g_ref, o_ref, m_sc, l_sc, acc_sc = refs[2 * pp:]
    jp = pl.program_id(1)
    last = pl.num_programs(1) - 1

    @pl.when(jp == 0)
    def _():
        m_sc[...] = jnp.full(m_sc.shape, NEG_INF, F32)
        l_sc[...] = jnp.zeros(l_sc.shape, F32)
        acc_sc[...] = jnp.zeros(acc_sc.shape, F32)

    def block(kbs, vbs, bias):
        s = jnp.stack([lax.dot_general(qt_ref[0, h], kbs[h], (((1,), (1,)), ((), ())),
                                       preferred_element_type=F32)
                       for h in range(n_heads)]) + bias
        m_prev = m_sc[...]
        m_new = jnp.maximum(m_prev, jnp.max(s, axis=-1, keepdims=True))
        alpha = jnp.exp(m_prev - m_new)
        p = jnp.exp(s - m_new)
        l_sc[...] = alpha * l_sc[...] + jnp.sum(p, axis=-1, keepdims=True)
        pb = p.astype(BF16)
        pv = jnp.stack([jnp.dot(pb[h], vbs[h], preferred_element_type=F32)
                        for h in range(n_heads)])
        acc_sc[...] = alpha * acc_sc[...] + pv
        m_sc[...] = m_new

    def head_rows(page_refs, h):
        rows_h = pl.ds(h, PAGE_SIZE, stride=n_heads)
        return jnp.concatenate([r[0, 0, rows_h, :] for r in page_refs], axis=0).astype(BF16)

    far = bias_ref[0]
    near = jnp.where(jp == last, bias_ref[1], far)
    block([head_rows(krefs, h) for h in range(n_heads)],
          [head_rows(vrefs, h) for h in range(n_heads)],
          jnp.concatenate([far] * (pp - 1) + [near], axis=-1))

    @pl.when(jp == last)
    def _():
        cols = [slice(h * KD, (h + 1) * KD) for h in range(n_heads)]
        block([kn_ref[0, :, c] for c in cols], [vn_ref[0, :, c] for c in cols], bias_ref[2])
        lam = _lambda_full(lq1, lk1, lq2, lk2, li, lam_init)
        g = g_ref[li:li + 1, :]
        o = acc_sc[...] / l_sc[...]
        o = _sub_rms(o[:, :n_q] - lam * o[:, n_q:], g, lam_init)
        for h in range(n_heads):
            o_ref[0, :, cols[h]] = o[h]


def _sample_attn(page_table, qt, cache_k, cache_v, k_new, v_new, bias3, lams, subln_g,
                 *, li, n_heads, n_q, lam_init):
    dbsz, n_pages = page_table.shape
    pp = _pick(n_pages, (PAGES_PER_STEP, 2, 1))
    w = n_heads * KD
    pt_flat = page_table.reshape(-1)

    def page_spec(g):
        return pl.BlockSpec((1, 1, PAGE_SIZE * n_heads, KD),
                            lambda b, j, pt, g=g: (li, pt[b * n_pages + j * pp + g], 0, 0))

    full = lambda a: pl.BlockSpec(a.shape, lambda b, j, pt: (0,) * a.ndim)
    per_b = pl.BlockSpec((1, LANES, w), lambda b, j, pt: (b, 0, 0))
    grid_spec = pltpu.PrefetchScalarGridSpec(
        num_scalar_prefetch=1, grid=(dbsz, n_pages // pp),
        in_specs=[pl.BlockSpec((1, n_heads, 2 * n_q, KD), lambda b, j, pt: (b, 0, 0, 0))]
                 + [page_spec(g) for g in range(pp)] * 2 + [per_b, per_b, full(bias3)]
                 + [full(a) for a in lams] + [full(subln_g)],
        out_specs=pl.BlockSpec((1, n_q, w), lambda b, j, pt: (b, 0, 0)),
        scratch_shapes=[pltpu.VMEM((n_heads, 2 * n_q, 1), F32),
                        pltpu.VMEM((n_heads, 2 * n_q, 1), F32),
                        pltpu.VMEM((n_heads, 2 * n_q, KD), F32)])
    body = functools.partial(_sattn_body, li=li, pp=pp, n_heads=n_heads, n_q=n_q,
                             lam_init=lam_init)
    return pl.pallas_call(
        body, grid_spec=grid_spec,
        out_shape=jax.ShapeDtypeStruct((dbsz, n_q, w), F32),
        compiler_params=_cparams(("parallel", "arbitrary")))(
            pt_flat, qt, *([cache_k] * pp), *([cache_v] * pp), k_new, v_new, bias3,
            *lams, subln_g)


def _block_queries(q, n_heads):
    dbsz, n_q, _ = q.shape
    qh = jnp.transpose(q.reshape(dbsz, n_q, n_heads, KD), (0, 2, 1, 3))
    col_map = (jnp.arange(KD) // DIFF_DH)[None, :] == jnp.arange(2)[:, None]
    t = qh[:, :, None] * col_map[None, None, :, None, :].astype(q.dtype)
    return t.reshape(dbsz, n_heads, 2 * n_q, KD)


def _sample_bias(t5_bias, n_heads, n_q):
    qi = jnp.arange(n_q, dtype=jnp.int32)
    key = jnp.arange(LANES, dtype=jnp.int32)

    def tile(rel):
        b = _bias_of_rel(t5_bias, rel)
        return jnp.concatenate([b, b], axis=1)

    far = tile(jnp.full((n_q, LANES), MAX_DISTANCE, jnp.int32))
    lastp = tile(PAGE_SIZE + qi[:, None] - key[None, :])
    new = tile(jnp.where(key[None, :] < n_q, qi[:, None] - key[None, :], -1))
    return jnp.stack([far, lastp, new])


def _ssd_body(xbc_ref, z_ref, dt_ref, h0_ref, dtb_ref, alog_ref, dx_ref, gn_ref,
              y_ref, hout_ref, hs, ysc, *, n_heads, valid_len):
    c = pl.program_id(1)
    t = SSD_CHUNK
    di = n_heads * SSD_P
    e_per_g = n_heads // SSD_G

    @pl.when(c == 0)
    def _():
        hs[...] = h0_ref[0]

    xbc = xbc_ref[0]
    xs = xbc[:, :di]
    row = lax.broadcasted_iota(jnp.int32, (t, LANES), 0)
    lane = lax.broadcasted_iota(jnp.int32, (t, LANES), 1)
    raw = dt_ref[0] + dtb_ref[...]
    dt = jnp.maximum(raw, 0.0) + jnp.log1p(jnp.exp(-jnp.abs(raw)))
    dt = jnp.where((row + c * t < valid_len) & (lane < n_heads), dt, 0.0)
    da = dt * (-jnp.exp(alog_ref[...]))
    tri = (lax.broadcasted_iota(jnp.int32, (t, t), 0)
           >= lax.broadcasted_iota(jnp.int32, (t, t), 1))
    cum = jnp.dot(tri.astype(F32), da, precision=lax.Precision.HIGHEST,
                  preferred_element_type=F32)
    cum_t = cum.T
    dt_t = dt.T
    cum_last = cum[t - 1:t, :]
    w_end = dt * jnp.exp(cum_last - cum)
    e_cum = jnp.exp(cum)
    e_last = jnp.exp(cum_last)
    xs_bf = xs.astype(BF16)
    xs_t = xs.T.astype(BF16)

    for g in range(SSD_G):
        b_g = xbc[:, di + g * SSD_N:di + (g + 1) * SSD_N]
        c_g = xbc[:, di + (SSD_G + g) * SSD_N:di + (SSD_G + g + 1) * SSD_N]
        cb = lax.dot_general(c_g.astype(BF16), b_g.astype(BF16), (((1,), (1,)), ((), ())),
                             preferred_element_type=F32)
        for e in range(e_per_g):
            h = g * e_per_g + e
            seg = cum[:, h:h + 1] - cum_t[h:h + 1, :]
            decay = jnp.where(tri, jnp.exp(jnp.where(tri, seg, 0.0)), 0.0)
            mat = (cb * decay * dt_t[h:h + 1, :]).astype(BF16)
            x_e = xs_bf[:, h * SSD_P:(h + 1) * SSD_P]
            y_d = jnp.dot(mat, x_e, preferred_element_type=F32)
            h_e = hs[h]
            c_w = (c_g * e_cum[:, h:h + 1]).astype(BF16)
            y_o = lax.dot_general(c_w, h_e.astype(BF16), (((1,), (1,)), ((), ())),
                                  preferred_element_type=F32)
            b_w = (b_g * w_end[:, h:h + 1]).astype(BF16)
            st = jnp.dot(xs_t[h * SSD_P:(h + 1) * SSD_P, :], b_w,
                         preferred_element_type=F32)
            hs[h] = h_e * e_last[:, h:h + 1] + st
            ysc[:, h * SSD_P:(h + 1) * SSD_P] = y_d + y_o

    y = ysc[...] + xs * dx_ref[...]
    z = z_ref[0]
    y = y * (z * jax.nn.sigmoid(z))
    gw = di // SSD_G
    for g in range(SSD_G):
        seg = y[:, g * gw:(g + 1) * gw]
        seg = seg * lax.rsqrt(jnp.mean(seg * seg, axis=-1, keepdims=True) + EPS)
        y_ref[0, :, g * gw:(g + 1) * gw] = (seg * gn_ref[:, g * gw:(g + 1) * gw]).astype(y_ref.dtype)

    @pl.when(c == pl.num_programs(1) - 1)
    def _():
        hout_ref[0] = hs[...]


def _ssd(xbc_act, zxd, z_blk_w, dt_blk, h0, dtb, alog, dx, gn, *, n_heads, valid_len):
    bsz, l, xbc_w = xbc_act.shape
    di = n_heads * SSD_P
    t = SSD_CHUNK
    assert l % t == 0 and z_blk_w == di
    full = lambda a: pl.BlockSpec(a.shape, lambda b, c: (0,) * a.ndim)
    body = functools.partial(_ssd_body, n_heads=n_heads, valid_len=valid_len)
    return pl.pallas_call(
        body, grid=(bsz, l // t),
        in_specs=[pl.BlockSpec((1, t, xbc_w), lambda b, c: (b, c, 0)),
                  pl.BlockSpec((1, t, di), lambda b, c: (b, c, 0)),
                  pl.BlockSpec((1, t, LANES), lambda b, c: (b, c, dt_blk)),
                  pl.BlockSpec((1, n_heads, SSD_P, SSD_N), lambda b, c: (b, 0, 0, 0)),
                  full(dtb), full(alog), full(dx), full(gn)],
        out_specs=[pl.BlockSpec((1, t, di), lambda b, c: (b, c, 0)),
                   pl.BlockSpec((1, n_heads, SSD_P, SSD_N), lambda b, c: (b, 0, 0, 0))],
        out_shape=[jax.ShapeDtypeStruct((bsz, l, di), BF16),
                   jax.ShapeDtypeStruct((bsz, n_heads, SSD_P, SSD_N), F32)],
        scratch_shapes=[pltpu.VMEM((n_heads, SSD_P, SSD_N), F32), pltpu.VMEM((t, di), F32)],
        compiler_params=_cparams(("parallel", "arbitrary")))(
            xbc_act, zxd, zxd, h0, dtb, alog, dx, gn)


def _epi_glu(accs, tiles, rows, j):
    a, b = accs
    return (a * jax.nn.sigmoid(b),)


def _epi_id(accs, tiles, rows, j):
    return (accs[0],)


def _epi_merge(accs, tiles, rows, j):
    ya, yb, yc = accs
    ga, gb, gc = tiles
    return (jax.nn.sigmoid(ga) * ya + jax.nn.sigmoid(gb) * yb + jax.nn.sigmoid(gc) * yc,)


def _epi_resid(accs, tiles, rows, j):
    return (tiles[0] + rows[0] * accs[0],)


def _trunk(x, mod, prev_a, prev_c, prev_h, prev_f, wts, p, dims, *, paged=None):
    bsz, l, d = x.shape
    m = bsz * l
    depth, n_heads, conv_c, ssd_h, d_ff = (dims[k] for k in
                                           ("depth", "n_heads", "conv_c", "ssd_h", "d_ff"))
    att_w = n_heads * KD
    ssd_di = ssd_h * SSD_P
    ssd_xbc = ssd_di + 2 * SSD_G * SSD_N
    conv_k = p["conv_a_w"].shape[1]
    ssd_k = p["ssd_conv_w"].shape[1]
    ffn_k = p["ffn_conv_w"].shape[1]
    prompt = paged is None
    tm = _pick(l, (512, 256, 128)) if prompt else m
    tpb = l // tm if prompt else 1
    t_seq = _pick(l, (128,)) if prompt else l
    lams = (p["lambda_q1"], p["lambda_k1"], p["lambda_q2"], p["lambda_k2"])
    scale = DIFF_DH ** -0.5
    big = (1024, 512, 256, 128)

    xf = x.reshape(m, d)
    out_a, out_c, out_h, out_f = [], [], [], []
    kst = jnp.zeros((depth, m, att_w), F32)
    vst = jnp.zeros((depth, m, att_w), F32)
    for li in range(depth):
        w = wts[li]
        md = mod[li]
        lam_init = 0.8 - 0.6 * math.exp(-0.3 * li)
        h = _norm_mod(xf, p["norm1_g"][li][None], md, 1, 0, tm=tm, tpb=tpb)

        tn = _pick(conv_c, big)
        (u,) = _mm([h, h], [(w["glu"], 0), (w["glu"], conv_c // tn)], _epi_glu, [F32], conv_c,
                   tm=tm, tn=tn)
        tn = _pick(att_w, big)
        qkv_bf, kst, vst = _qkv(h, w["qkv"], kst, vst, li=li, scale=scale, tm=tm, tn=tn)
        zxd_w = w["zxd"].shape[1]
        tn = _pick(zxd_w, (896, 640, 512, 384, 256, 128))
        (zxd,) = _mm([h], [(w["zxd"], 0)], _epi_id, [F32], zxd_w, tm=tm, tn=tn)
        tn = _pick(3 * d, big)
        (gates,) = _mm([h], [(w["gates"], 0)], _epi_id, [F32], 3 * d, tm=tm, tn=tn)

        u3 = u.reshape(bsz, l, conv_c)
        out_a.append(_conv_state(prev_a[li], u3, conv_k - 1))
        a_act = _dwconv(u3, 0, _pad_prev(prev_a[li], 32), p["conv_a_w"][li],
                        p["conv_a_b"][li][None], _epi_conv_a, BF16, c=conv_c, t=t_seq,
                        tc=conv_c, halo=32,
                        prows=(p["ln_a_g"][li][None], p["ln_a_b"][li][None]))

        if prompt:
            o = _prompt_attn(qkv_bf.reshape(bsz, l, 3 * att_w), dims["bias_tiles"], lams,
                             p["subln_g"], li=li, n_heads=n_heads, lam_init=lam_init)
        else:
            qb = qkv_bf.reshape(bsz, l, 3 * att_w)
            qt = _block_queries(qb[..., :att_w], n_heads)
            pad = ((0, 0), (0, LANES - l), (0, 0))
            k_new = jnp.pad(qb[..., att_w:2 * att_w], pad)
            v_new = jnp.pad(qb[..., 2 * att_w:], pad)
            o = _sample_attn(paged["page_table"], qt, paged["cache_k"], paged["cache_v"],
                             k_new, v_new, dims["bias3"], lams, p["subln_g"], li=li,
                             n_heads=n_heads, n_q=l, lam_init=lam_init).astype(BF16)
        o = o.reshape(m, att_w)

        zxd3 = zxd.reshape(bsz, l, zxd_w)
        xbc_raw = zxd3[..., ssd_di:ssd_di + ssd_xbc]
        out_c.append(_conv_state(prev_c[li], xbc_raw, ssd_k - 1))
        tc = _pick(math.gcd(ssd_di, ssd_xbc), (512, 256, 128))
        xbc_act = _dwconv(zxd3, ssd_di // tc, _pad_prev(prev_c[li], 8), p["ssd_conv_w"][li],
                          p["ssd_conv_b"][li][None], _epi_silu, F32, c=ssd_xbc,
                          t=_pick(l, (512, 256, 128)) if prompt else l, tc=tc, halo=8)
        if prompt:
            xbc_in, zxd_in = xbc_act, zxd3
        else:
            pad = ((0, 0), (0, SSD_CHUNK - l), (0, 0))
            xbc_in, zxd_in = jnp.pad(xbc_act, pad), jnp.pad(zxd3, pad)
        lane_pad = lambda a: jnp.pad(a, (0, LANES - a.shape[0]))[None]
        y_c, h_t = _ssd(xbc_in, zxd_in, ssd_di, (ssd_di + ssd_xbc) // LANES, prev_h[li],
                        lane_pad(p["dt_bias"][li]), lane_pad(p["a_log"][li]),
                        jnp.repeat(p["d_skip"][li], SSD_P)[None], p["ssd_norm_g"][li][None],
                        n_heads=ssd_h, valid_len=l)
        out_h.append(h_t)
        y_c = y_c[:, :l].reshape(m, ssd_di)

        tn = _pick(d, big)
        nb = d // tn
        (merged,) = _mm([a_act.reshape(m, conv_c), o, y_c],
                        [(w["proj_a"], 0), (w["proj_b"], 0), (w["proj_c"], 0)], _epi_merge,
                        [BF16], d, tiles=[(gates, 0), (gates, nb), (gates, 2 * nb)], tm=tm, tn=tn)
        (xf,) = _mm([merged], [(w["out"], 0)], _epi_resid, [F32], d, tiles=[(xf, 0)],
                    rows=[(md, 2 * nb)], tm=tm, tn=tn, tpb=tpb)

        h2 = _norm_mod(xf, p["norm2_g"][li][None], md, 4, 3, tm=tm, tpb=tpb)
        if prompt:
            act = _upconv(h2, w["up"], _pad_prev(prev_f[li], UP_HALO), p["ffn_conv_w"][li],
                          p["ffn_conv_b"][li][None], d_ff=d_ff, tm=tm,
                          tn=_pick(d_ff, (512, 256, 128)), tpb=tpb)
            assert l >= UP_HALO
            h2_tail = h2.reshape(bsz, l, d)[:, l - UP_HALO:].reshape(bsz * UP_HALO, d)
            (fa_tail,) = _mm([h2_tail], [(w["up"], 0)], _epi_id, [F32], d_ff,
                             tm=bsz * UP_HALO, tn=_pick(d_ff, (512, 256, 128)))
            out_f.append(fa_tail.reshape(bsz, UP_HALO, d_ff)[:, UP_HALO - (ffn_k - 1):])
        else:
            tn = _pick(d_ff, big)
            (up,) = _mm([h2], [(w["up"], 0)], _epi_id, [F32], 2 * d_ff, tm=tm, tn=tn)
            up3 = up.reshape(bsz, l, 2 * d_ff)
            out_f.append(_conv_state(prev_f[li], up3[..., :d_ff], ffn_k - 1))
            tc = _pick(d_ff, (512, 256, 128))
            act = _dwconv(up3, 0, _pad_prev(prev_f[li], 8), p["ffn_conv_w"][li],
                          p["ffn_conv_b"][li][None], _epi_ffn, BF16, c=d_ff, t=t_seq, tc=tc,
                          halo=8, xtiles=[(up3, d_ff // tc)])
        tn = _pick(d, big)
        (xf,) = _mm([act.reshape(m, d_ff)], [(w["down"], 0)], _epi_resid, [F32], d,
                    tiles=[(xf, 0)], rows=[(md, 5 * (d // tn))], tm=tm, tn=tn, tpb=tpb)

    y_out = _rms(xf, p["final_g"][None], tm=tm).reshape(bsz, l, d)
    kv_shape = (depth, bsz, l, n_heads, KD)
    return (y_out, kst.reshape(kv_shape), vst.reshape(kv_shape), jnp.stack(out_a), jnp.stack(out_c),
            jnp.stack(out_h), jnp.stack(out_f))


def _split_weights(p, li, dims):
    conv_c, att_w = dims["conv_c"], dims["n_heads"] * KD
    ssd_di = dims["ssd_h"] * SSD_P
    ssd_xbc = ssd_di + 2 * SSD_G * SSD_N
    w_in = p["w_in"][li]
    c0 = 2 * conv_c
    c1 = c0 + 3 * att_w
    c2 = c1 + ssd_di + ssd_xbc
    c3 = c2 + dims["ssd_h"]
    dt_cols = jnp.pad(w_in[:, c2:c3], ((0, 0), (0, LANES - dims["ssd_h"])))
    bf = lambda a: a.astype(BF16)
    return dict(glu=bf(w_in[:, :c0]), qkv=bf(w_in[:, c0:c1]),
                zxd=bf(jnp.concatenate([w_in[:, c1:c2], dt_cols], axis=1)),
                gates=bf(w_in[:, c3:]), proj_a=bf(p["w_proj_a"][li]), proj_b=bf(p["w_proj_b"][li]),
                proj_c=bf(p["w_proj_c"][li]), out=bf(p["w_out"][li]), up=bf(p["ffn_up"][li]),
                down=bf(p["ffn_down"][li]))


def kernel(x_prompt, x_sample, cache_k, cache_v, page_table, state_conv_a, state_ssd_conv,
           state_ssm, state_ffn_conv, c_prompt, c_sample, w_ada, b_ada, norm1_g, w_in,
           conv_a_w, conv_a_b, ln_a_g, ln_a_b, w_proj_a, t5_bias, lambda_q1, lambda_k1,
           lambda_q2, lambda_k2, subln_g, w_proj_b, ssd_conv_w, ssd_conv_b, dt_bias, a_log,
           d_skip, ssd_norm_g, w_proj_c, w_out, norm2_g, ffn_up, ffn_conv_w, ffn_conv_b,
           ffn_down, final_g):
    p = dict(w_ada=w_ada, b_ada=b_ada, norm1_g=norm1_g, w_in=w_in, conv_a_w=conv_a_w,
             conv_a_b=conv_a_b, ln_a_g=ln_a_g, ln_a_b=ln_a_b, w_proj_a=w_proj_a, t5_bias=t5_bias,
             lambda_q1=lambda_q1, lambda_k1=lambda_k1, lambda_q2=lambda_q2, lambda_k2=lambda_k2,
             subln_g=subln_g, w_proj_b=w_proj_b, ssd_conv_w=ssd_conv_w, ssd_conv_b=ssd_conv_b,
             dt_bias=dt_bias, a_log=a_log, d_skip=d_skip, ssd_norm_g=ssd_norm_g, w_proj_c=w_proj_c,
             w_out=w_out, norm2_g=norm2_g, ffn_up=ffn_up, ffn_conv_w=ffn_conv_w,
             ffn_conv_b=ffn_conv_b, ffn_down=ffn_down, final_g=final_g)
    bp, lp, d = x_prompt.shape
    bs, ls, _ = x_sample.shape
    depth = w_ada.shape[0]
    n_heads = cache_k.shape[3]
    n_pages = page_table.shape[1]
    dims = dict(depth=depth, n_heads=n_heads, conv_c=conv_a_w.shape[2], ssd_h=dt_bias.shape[1],
                d_ff=ffn_conv_w.shape[2])
    f32 = x_prompt.dtype

    tq = _pick(lp, (ATTN_TQ, 128))
    assert tq >= MAX_DISTANCE
    qi = jnp.arange(tq, dtype=jnp.int32)
    rel0 = qi[:, None] - qi[None, :]
    b_diag, b_prev = _bias_of_rel(t5_bias, rel0), _bias_of_rel(t5_bias, rel0 + tq)
    dims["bias_tiles"] = jnp.stack(
        [jnp.concatenate([b_prev, b_diag], axis=-1),
         jnp.concatenate([b_diag, jnp.full_like(b_diag, NEG_INF)], axis=-1)], axis=1)
    dims["bias3"] = _sample_bias(t5_bias, n_heads, ls)

    rows = -(-(bp + bs) // 8) * 8
    c_all = jnp.pad(jnp.concatenate([c_prompt, c_sample], axis=0), ((0, rows - bp - bs), (0, 0)))
    mod = _ada(c_all, w_ada, b_ada)
    mod_p = mod[:, :bp, None, :]
    mod_s = jnp.repeat(mod[:, bp:bp + bs], ls, axis=1)[:, None]

    wts = [_split_weights(p, li, dims) for li in range(depth)]
    zero = lambda *s: jnp.zeros((depth, bp) + s, f32)
    res_p = _trunk(x_prompt, mod_p, zero(conv_a_w.shape[1] - 1, dims["conv_c"]),
                   zero(ssd_conv_w.shape[1] - 1, ssd_conv_w.shape[2]),
                   zero(dims["ssd_h"], SSD_P, SSD_N), zero(ffn_conv_w.shape[1] - 1, dims["d_ff"]),
                   wts, p, dims)
    paged = dict(page_table=page_table,
                 cache_k=cache_k.reshape(depth, -1, PAGE_SIZE * n_heads, KD),
                 cache_v=cache_v.reshape(depth, -1, PAGE_SIZE * n_heads, KD))
    res_s = _trunk(x_sample, mod_s, state_conv_a, state_ssd_conv, state_ssm, state_ffn_conv,
                   wts, p, dims, paged=paged)
    y_p, k_p, v_p, a_p, c_p, h_p, f_p = res_p
    y_s, k_s, v_s, a_s, c_s, h_s, f_s = res_s
    return (y_p, y_s, k_p, v_p, k_s, v_s, a_p, a_s, c_p, c_s, h_p, h_s, f_p, f_s)
```

```python
import functools
import math

import jax
import jax.numpy as jnp
from jax import lax
from jax.experimental import pallas as pl
from jax.experimental.pallas import tpu as pltpu

F32 = jnp.float32
BF16 = jnp.bfloat16

EPS = 1e-6
LN_EPS = 1e-5
NEG_INF = -1e30
DIFF_DH = 64
KD = 2 * DIFF_DH
NUM_BUCKETS = 32
MAX_DISTANCE = 128
SSD_P = 64
SSD_N = 128
SSD_G = 2
SSD_CHUNK = 128
PAGE_SIZE = 128
LANES = 128
VMEM_LIMIT = 56 * 1024 * 1024
VMEM_BLOCK_BUDGET = 46 * 1024 * 1024
MIN_TN = 512
MM_ROW_CHUNK = 512

ATTN_TQ = 256
PAGES_PER_STEP = 8


def _cparams(sem):
    return pltpu.CompilerParams(dimension_semantics=sem, vmem_limit_bytes=VMEM_LIMIT)


def _pick(n, cands):
    for c in cands:
        if n % c == 0:
            return c
    return n


def _fit_tiles(m_cands, n_cands, block_bytes):
    wide = [c for c in n_cands if c >= MIN_TN] or list(n_cands)
    for cols in (wide, list(n_cands)):
        for tm in m_cands:
            for tn in cols:
                if block_bytes(tm, tn) <= VMEM_BLOCK_BUDGET:
                    return tm, tn
    return m_cands[-1], n_cands[-1]


def _row_cands(rows_per_batch, whole=None):
    if whole is not None:
        return [whole]
    return [c for c in (2048, 1024, 512, 256, 128) if rows_per_batch % c == 0] or [rows_per_batch]


def _col_cands(n, offsets=()):
    cands = [c for c in (1024, 896, 768, 640, 512, 384, 256, 128)
             if n % c == 0 and all(o % c == 0 for o in offsets)]
    return cands or [n]


def _mm_body(*refs, nl, lhs_of, nt, nr, epi, rm):
    nw = len(lhs_of)
    lrefs = refs[:nl]
    wrefs = refs[nl:nl + nw]
    trefs = refs[nl + nw:nl + nw + nt]
    rrefs = refs[nl + nw + nt:nl + nw + nt + nr]
    orefs = refs[nl + nw + nt + nr:]
    tm = orefs[0].shape[0]
    j = pl.program_id(1)
    for m0 in range(0, tm, rm):
        rs = slice(m0, m0 + rm)
        accs = [jnp.dot(lrefs[li][rs, :], w[...], preferred_element_type=F32)
                for w, li in zip(wrefs, lhs_of)]
        rows = [r[0] if r.shape[1] == 1 else r[0, rs, :] for r in rrefs]
        res = epi(accs, [t[rs, :] for t in trefs], rows, j)
        for o, r in zip(orefs, res):
            o[rs, :] = r.astype(o.dtype)


def _mm(lhs, rhs, epi, out_dtypes, n, *, tiles=(), rows=(), rows_per_batch=None):
    m = lhs[0].shape[0]
    offsets = [off for _, off, _ in rhs] + [off for _, off in tiles] + [off for _, off in rows]

    def block_bytes(tm, tn):
        b = sum(tm * a.shape[1] * a.dtype.itemsize for a in lhs)
        b += sum(a.shape[0] * tn * a.dtype.itemsize for a, _, _ in rhs)
        b += sum(tm * tn * a.dtype.itemsize for a, _ in tiles)
        b += sum(tm * tn * jnp.dtype(d).itemsize for d in out_dtypes)
        return 2 * b + len(rhs) * min(tm, MM_ROW_CHUNK) * tn * 4

    tm, tn = _fit_tiles(_row_cands(rows_per_batch, None if rows_per_batch else m),
                        _col_cands(n, offsets), block_bytes)
    tpb = rows_per_batch // tm if rows_per_batch else 1
    assert m % tm == 0 and n % tn == 0
    in_specs, args = [], []
    for a in lhs:
        in_specs.append(pl.BlockSpec((tm, a.shape[1]), lambda i, j: (i, 0)))
        args.append(a)
    for a, off, _ in rhs:
        in_specs.append(pl.BlockSpec((a.shape[0], tn), lambda i, j, o=off // tn: (0, j + o)))
        args.append(a)
    for a, off in tiles:
        in_specs.append(pl.BlockSpec((tm, tn), lambda i, j, o=off // tn: (i, j + o)))
        args.append(a)
    for a, off in rows:
        assert a.shape[1] == 1 or (a.shape[0] == 1 and a.shape[1] == tm == m)
        in_specs.append(pl.BlockSpec((1, a.shape[1], tn),
                                     lambda i, j, o=off // tn: (i // tpb, 0, j + o)))
        args.append(a)
    out_specs = [pl.BlockSpec((tm, tn), lambda i, j: (i, j)) for _ in out_dtypes]
    out_shape = [jax.ShapeDtypeStruct((m, n), d) for d in out_dtypes]
    body = functools.partial(_mm_body, nl=len(lhs), lhs_of=tuple(li for _, _, li in rhs),
                             nt=len(tiles), nr=len(rows), epi=epi, rm=min(tm, MM_ROW_CHUNK))
    return pl.pallas_call(
        body, grid=(m // tm, n // tn), in_specs=in_specs, out_specs=out_specs,
        out_shape=out_shape, compiler_params=_cparams(("parallel", "arbitrary")))(*args)


def _qkv_body(h_ref, w_ref, kin_ref, vin_ref, qkv_ref, kst_ref, vst_ref, *, nq, scale):
    del kin_ref, vin_ref
    j = pl.program_id(1)
    acc = jnp.dot(h_ref[...], w_ref[...], preferred_element_type=F32)
    qkv_ref[...] = (acc * jnp.where(j < nq, scale, 1.0)).astype(qkv_ref.dtype)

    @pl.when((j >= nq) & (j < 2 * nq))
    def _():
        kst_ref[0] = acc

    @pl.when(j >= 2 * nq)
    def _():
        vst_ref[0] = acc


def _qkv(h, w, kst, vst, *, li, scale, rows_per_batch=None):
    m, d = h.shape
    wdt = w.shape[1] // 3
    block_bytes = lambda tm, tn: 2 * (tm * d * 2 + d * tn * 2 + tm * tn * (2 + 4 + 4)) + tm * tn * 4
    tm, tn = _fit_tiles(_row_cands(rows_per_batch, None if rows_per_batch else m),
                        _col_cands(wdt), block_bytes)
    nq = wdt // tn
    assert m % tm == 0 and wdt % tn == 0
    k_idx = lambda i, j: (li, i, jnp.clip(j - nq, 0, nq - 1))
    v_idx = lambda i, j: (li, i, jnp.clip(j - 2 * nq, 0, nq - 1))
    body = functools.partial(_qkv_body, nq=nq, scale=scale)
    return pl.pallas_call(
        body, grid=(m // tm, 3 * nq),
        in_specs=[pl.BlockSpec((tm, d), lambda i, j: (i, 0)),
                  pl.BlockSpec((d, tn), lambda i, j: (0, j)),
                  pl.BlockSpec(memory_space=pl.ANY), pl.BlockSpec(memory_space=pl.ANY)],
        out_specs=[pl.BlockSpec((tm, tn), lambda i, j: (i, j)),
                   pl.BlockSpec((1, tm, tn), k_idx), pl.BlockSpec((1, tm, tn), v_idx)],
        out_shape=[jax.ShapeDtypeStruct((m, 3 * wdt), BF16),
                   jax.ShapeDtypeStruct(kst.shape, F32), jax.ShapeDtypeStruct(vst.shape, F32)],
        input_output_aliases={2: 1, 3: 2},
        compiler_params=_cparams(("arbitrary", "arbitrary")))(h, w, kst, vst)


def _ada_body(c_ref, w_ref, b_ref, o_ref):
    c = c_ref[...]
    s = (c * jax.nn.sigmoid(c)).astype(BF16)
    o_ref[0] = jnp.dot(s, w_ref[0].astype(BF16), preferred_element_type=F32) + b_ref[0]


def _ada(c_all, w_ada, b_ada):
    depth, d, n6 = w_ada.shape
    rows = c_all.shape[0]
    tn = _pick(n6, (1024, 512, 256, 128))
    return pl.pallas_call(
        _ada_body, grid=(depth, n6 // tn),
        in_specs=[pl.BlockSpec((rows, d), lambda l, j: (0, 0)),
                  pl.BlockSpec((1, d, tn), lambda l, j: (l, 0, j)),
                  pl.BlockSpec((1, 1, tn), lambda l, j: (l, 0, j))],
        out_specs=pl.BlockSpec((1, rows, tn), lambda l, j: (l, 0, j)),
        out_shape=jax.ShapeDtypeStruct((depth, rows, n6), F32),
        compiler_params=_cparams(("arbitrary", "arbitrary")))(c_all, w_ada, b_ada[:, None, :])


def _norm_mod_body(x_ref, g_ref, sc_ref, sh_ref, o_ref):
    x = x_ref[...]
    y = x * lax.rsqrt(jnp.mean(x * x, axis=-1, keepdims=True) + EPS) * g_ref[...]
    o_ref[...] = (y * (1.0 + sc_ref[0]) + sh_ref[0]).astype(o_ref.dtype)


def _norm_mod(x, g, mod, sc_blk, sh_blk, *, tm, tpb):
    m, d = x.shape
    r = mod.shape[1]
    return pl.pallas_call(
        _norm_mod_body, grid=(m // tm,),
        in_specs=[pl.BlockSpec((tm, d), lambda i: (i, 0)),
                  pl.BlockSpec((1, d), lambda i: (0, 0)),
                  pl.BlockSpec((1, r, d), lambda i: (i // tpb, 0, sc_blk)),
                  pl.BlockSpec((1, r, d), lambda i: (i // tpb, 0, sh_blk))],
        out_specs=pl.BlockSpec((tm, d), lambda i: (i, 0)),
        out_shape=jax.ShapeDtypeStruct((m, d), BF16),
        compiler_params=_cparams(("parallel",)))(x, g, mod, mod)


def _rms_body(x_ref, g_ref, o_ref):
    x = x_ref[...]
    o_ref[...] = x * lax.rsqrt(jnp.mean(x * x, axis=-1, keepdims=True) + EPS) * g_ref[...]


def _rms(x, g, *, tm):
    m, d = x.shape
    return pl.pallas_call(
        _rms_body, grid=(m // tm,),
        in_specs=[pl.BlockSpec((tm, d), lambda i: (i, 0)),
                  pl.BlockSpec((1, d), lambda i: (0, 0))],
        out_specs=pl.BlockSpec((tm, d), lambda i: (i, 0)),
        out_shape=jax.ShapeDtypeStruct((m, d), F32),
        compiler_params=_cparams(("parallel",)))(x, g)


def _dwconv_body(*refs, kc, halo, t, tc, n_seq_tiles, nx, npr, epi):
    u_ref, prev_ref, w_ref, b_ref = refs[:4]
    xrefs = refs[4:4 + nx]
    prefs = refs[4 + nx:4 + nx + npr]
    o_ref = refs[4 + nx + npr]
    win, cv = refs[4 + nx + npr + 1:]

    @pl.when(pl.program_id(2) == 0)
    def _():
        win[0:halo, :] = prev_ref[0]

    win[halo:halo + t, :] = u_ref[0]
    base = halo - (kc - 1)
    rc = min(t, 64)
    cc = min(tc, 256)
    for r0 in range(0, t, rc):
        for c0 in range(0, tc, cc):
            acc = jnp.broadcast_to(b_ref[:, c0:c0 + cc], (rc, cc))
            wlen = rc + halo
            window = win[r0:r0 + wlen, c0:c0 + cc]
            for r in range(8):
                taps = [j for j in range(kc) if (base + j) % 8 == r]
                if not taps:
                    continue
                sh = window if r == 0 else pltpu.roll(window, wlen - r, 0)
                for j in taps:
                    a = (base + j) // 8
                    acc = acc + w_ref[j:j + 1, c0:c0 + cc] * sh[8 * a:8 * a + rc]
            cv[r0:r0 + rc, c0:c0 + cc] = acc
    o_ref[0] = epi(cv[...], [x[0] for x in xrefs], [p[...] for p in prefs]).astype(o_ref.dtype)
    if n_seq_tiles > 1:
        win[0:halo, :] = win[t:t + halo, :]


def _dwconv(u, u_blk, prev, w, b, epi, out_dtype, *, c, t, tc, halo, xtiles=(), prows=()):
    bsz, l, _ = u.shape
    kc = w.shape[0]
    assert l % t == 0 and c % tc == 0 and halo >= kc - 1 and (t >= halo or l == t)
    nlt = l // t
    in_specs = [pl.BlockSpec((1, t, tc), lambda bi, ci, li: (bi, li, ci + u_blk)),
                pl.BlockSpec((1, halo, tc), lambda bi, ci, li: (bi, 0, ci)),
                pl.BlockSpec((kc, tc), lambda bi, ci, li: (0, ci)),
                pl.BlockSpec((1, tc), lambda bi, ci, li: (0, ci))]
    args = [u, prev, w, b]
    for a, off in xtiles:
        in_specs.append(pl.BlockSpec((1, t, tc), lambda bi, ci, li, off=off: (bi, li, ci + off)))
        args.append(a)
    for a in prows:
        in_specs.append(pl.BlockSpec((1, tc), lambda bi, ci, li: (0, ci)))
        args.append(a)
    body = functools.partial(_dwconv_body, kc=kc, halo=halo, t=t, tc=tc, n_seq_tiles=nlt,
                             nx=len(xtiles), npr=len(prows), epi=epi)
    return pl.pallas_call(
        body, grid=(bsz, c // tc, nlt), in_specs=in_specs,
        out_specs=pl.BlockSpec((1, t, tc), lambda bi, ci, li: (bi, li, ci)),
        out_shape=jax.ShapeDtypeStruct((bsz, l, c), out_dtype),
        scratch_shapes=[pltpu.VMEM((halo + t, tc), F32), pltpu.VMEM((t, tc), F32)],
        compiler_params=_cparams(("parallel", "parallel", "arbitrary")))(*args)


def _epi_conv_a(cv, xs, ps):
    g, b = ps
    mu = jnp.mean(cv, axis=-1, keepdims=True)
    d = cv - mu
    var = jnp.mean(d * d, axis=-1, keepdims=True)
    y = d * lax.rsqrt(var + LN_EPS) * g + b
    return y * jax.nn.sigmoid(y)


def _epi_silu(cv, xs, ps):
    return cv * jax.nn.sigmoid(cv)


def _epi_ffn(cv, xs, ps):
    return jax.nn.gelu(cv) * xs[0]


UP_HALO = 8


def _upconv_body(h_ref, wa_ref, wb_ref, prev_ref, cw_ref, cb_ref, act_ref,
                 win, carry, *, tm, tn, tpb, kc):
    i = pl.program_id(0)
    j = pl.program_id(1)
    first = i % tpb == 0

    @pl.when(first)
    def _():
        win[0:UP_HALO, :] = prev_ref[0]

    @pl.when(jnp.logical_not(first))
    def _():
        win[0:UP_HALO, :] = carry[j]

    base = UP_HALO - (kc - 1)
    rm = min(tm, 256)
    rc = min(rm, 64)
    cc = min(tn, 256)
    for m0 in range(0, tm, rm):
        h = h_ref[m0:m0 + rm, :]
        fa = jnp.dot(h, wa_ref[...], preferred_element_type=F32)
        fb = jnp.dot(h, wb_ref[...], preferred_element_type=F32)
        win[UP_HALO + m0:UP_HALO + m0 + rm, :] = fa
        for r0 in range(0, rm, rc):
            for c0 in range(0, tn, cc):
                acc = jnp.broadcast_to(cb_ref[:, c0:c0 + cc], (rc, cc))
                for t in range(kc):
                    start = base + m0 + r0 + t
                    acc = acc + cw_ref[t:t + 1, c0:c0 + cc] * win[start:start + rc, c0:c0 + cc]
                act_ref[m0 + r0:m0 + r0 + rc, c0:c0 + cc] = (
                    jax.nn.gelu(acc) * fb[r0:r0 + rc, c0:c0 + cc]).astype(act_ref.dtype)
    carry[j] = win[tm:tm + UP_HALO, :]


def _upconv(h, w_up, prev, cw, cb, *, d_ff, rows_per_batch):
    m, d = h.shape
    kc = cw.shape[0]
    block_bytes = lambda tm, tn: (2 * (tm * d * 2 + 2 * d * tn * 2 + tm * tn * 2)
                                  + (tm + UP_HALO) * tn * 4 + 2 * min(tm, 256) * tn * 4)
    tm, tn = _fit_tiles(_row_cands(rows_per_batch), _col_cands(d_ff), block_bytes)
    tpb = rows_per_batch // tm
    nj = d_ff // tn
    assert m % tm == 0 and d_ff % tn == 0 and kc - 1 <= UP_HALO <= tm
    body = functools.partial(_upconv_body, tm=tm, tn=tn, tpb=tpb, kc=kc)
    return pl.pallas_call(
        body, grid=(m // tm, nj),
        in_specs=[pl.BlockSpec((tm, d), lambda i, j: (i, 0)),
                  pl.BlockSpec((d, tn), lambda i, j: (0, j)),
                  pl.BlockSpec((d, tn), lambda i, j: (0, j + nj)),
                  pl.BlockSpec((1, UP_HALO, tn), lambda i, j: (i // tpb, 0, j)),
                  pl.BlockSpec((kc, tn), lambda i, j: (0, j)),
                  pl.BlockSpec((1, tn), lambda i, j: (0, j))],
        out_specs=pl.BlockSpec((tm, tn), lambda i, j: (i, j)),
        out_shape=jax.ShapeDtypeStruct((m, d_ff), BF16),
        scratch_shapes=[pltpu.VMEM((UP_HALO + tm, tn), F32), pltpu.VMEM((nj, UP_HALO, tn), F32)],
        compiler_params=_cparams(("arbitrary", "arbitrary")))(h, w_up, w_up, prev, cw, cb)


def _conv_state(prev, u, k1):
    l = u.shape[1]
    if l >= k1:
        return u[:, l - k1:]
    return jnp.concatenate([prev[:, l:], u], axis=1)


def _pad_prev(prev, halo):
    return jnp.pad(prev, ((0, 0), (halo - prev.shape[1], 0), (0, 0)))


def _t5_bucket(rel):
    n = jnp.maximum(rel, 0)
    max_exact = NUM_BUCKETS // 2
    nf = jnp.maximum(n, 1).astype(F32)
    large = max_exact + (jnp.log(nf / max_exact) / math.log(MAX_DISTANCE / max_exact)
                         * (NUM_BUCKETS - max_exact)).astype(jnp.int32)
    large = jnp.minimum(large, NUM_BUCKETS - 1)
    return jnp.where(n < max_exact, n, large)


def _bias_of_rel(t5_bias, rel):
    bucket = _t5_bucket(rel)
    table = t5_bias.astype(F32)
    expand = (slice(None),) + (None,) * rel.ndim
    bias = jnp.zeros((table.shape[1],) + rel.shape, F32)
    for b in range(NUM_BUCKETS):
        bias = jnp.where((bucket == b)[None], table[b][expand], bias)
    return jnp.where((rel >= 0)[None], bias, NEG_INF)


def _lambda_full(lq1, lk1, lq2, lk2, li, lam_init):
    a = jnp.sum(lq1[li:li + 1, :] * lk1[li:li + 1, :], axis=-1, keepdims=True)
    b = jnp.sum(lq2[li:li + 1, :] * lk2[li:li + 1, :], axis=-1, keepdims=True)
    return jnp.exp(a) - jnp.exp(b) + lam_init


def _softmax_step(s, v, m_sc, l_sc, acc_sc):
    m_prev = m_sc[...]
    m_new = jnp.maximum(m_prev, jnp.max(s, axis=-1, keepdims=True))
    alpha = jnp.exp(m_prev - m_new)
    p = jnp.exp(s - m_new)
    l_sc[...] = alpha * l_sc[...] + jnp.sum(p, axis=-1, keepdims=True)
    acc_sc[...] = alpha * acc_sc[...] + jnp.dot(p.astype(BF16), v, preferred_element_type=F32)
    m_sc[...] = m_new


def _sub_rms(o, g, lam_init):
    y = o * lax.rsqrt(jnp.mean(o * o, axis=-1, keepdims=True) + EPS) * g
    return y * (1.0 - lam_init)


def _pattn_body(q_ref, k_ref, v_ref, bt_ref, lq1, lk1, lq2, lk2, g_ref, o_ref,
                vp_sc, m_sc, acc_sc, *, li, tq, nh, lam_init):
    i = pl.program_id(2)

    @pl.when(i == 0)
    def _():
        for a in range(nh):
            vp_sc[a, :, :KD] = v_ref[0, :, a * KD:(a + 1) * KD]
            vp_sc[a, :, KD:] = jnp.ones((vp_sc.shape[1], KD), BF16)

    m_sc[...] = jnp.full(m_sc.shape, NEG_INF, F32)
    acc_sc[...] = jnp.zeros(acc_sc.shape, F32)
    lane = lax.broadcasted_iota(jnp.int32, (tq, KD), 1)
    qqs = []
    for a in range(nh):
        q = q_ref[0, :, a * KD:(a + 1) * KD]
        zero = jnp.zeros_like(q)
        qqs.append(jnp.concatenate([jnp.where(lane < DIFF_DH, q, zero),
                                    jnp.where(lane >= DIFF_DH, q, zero)], axis=0))

    def block(off, width, bias_of_head):
        for a in range(nh):
            kb = k_ref[0, pl.ds(off, width), a * KD:(a + 1) * KD]
            vb = vp_sc[a, pl.ds(off, width), :]
            s = lax.dot_general(qqs[a], kb, (((1,), (1,)), ((), ())),
                                preferred_element_type=F32) + bias_of_head(a)
            m_prev = m_sc[a]
            m_new = jnp.maximum(m_prev, jnp.max(s, axis=-1, keepdims=True))
            alpha = jnp.exp(m_prev - m_new)
            p = jnp.exp(s - jnp.concatenate([m_new] * (width // LANES), axis=1))
            acc_sc[a] = (jnp.concatenate([alpha, alpha], axis=1) * acc_sc[a]
                         + jnp.dot(p.astype(BF16), vb, preferred_element_type=F32))
            m_sc[a] = m_new

    n_far = jnp.maximum(i - 1, 0)
    far_of_head = lambda a: bt_ref[a, 0, tq - 1:tq, 0:1]

    def far_body(j, carry):
        block(pl.multiple_of(j * 2 * tq, 2 * tq), 2 * tq, far_of_head)
        return carry

    lax.fori_loop(0, n_far // 2, far_body, 0)

    @pl.when(n_far % 2 == 1)
    def _():
        block(pl.multiple_of((n_far - 1) * tq, tq), tq, far_of_head)

    first = (i == 0).astype(jnp.int32)

    def near_of_head(a):
        b = bt_ref[a, first]
        return jnp.concatenate([b, b], axis=0)

    block(pl.multiple_of(n_far * tq, tq), 2 * tq, near_of_head)

    lam = _lambda_full(lq1, lk1, lq2, lk2, li, lam_init)
    g = g_ref[li:li + 1, :]
    for a in range(nh):
        acc = acc_sc[a]
        o = acc[:, :KD] / acc[:, KD:]
        o = o[:tq] - lam * o[tq:]
        o_ref[0, :, a * KD:(a + 1) * KD] = _sub_rms(o, g, lam_init).astype(o_ref.dtype)


def _prompt_attn(qkv, bt, lams, subln_g, *, li, n_heads, lam_init):
    bsz, l, _ = qkv.shape
    tq = bt.shape[2]
    nh = _pick(n_heads, (4, 2, 1))
    hb = n_heads // nh
    assert l % (2 * tq) == 0
    full = lambda a: pl.BlockSpec(a.shape, lambda b, hh, i: (0,) * a.ndim)
    body = functools.partial(_pattn_body, li=li, tq=tq, nh=nh, lam_init=lam_init)
    return pl.pallas_call(
        body, grid=(bsz, hb, l // tq),
        in_specs=[pl.BlockSpec((1, tq, nh * KD), lambda b, hh, i: (b, i, hh)),
                  pl.BlockSpec((1, l, nh * KD), lambda b, hh, i: (b, 0, hb + hh)),
                  pl.BlockSpec((1, l, nh * KD), lambda b, hh, i: (b, 0, 2 * hb + hh)),
                  pl.BlockSpec((nh, 2, tq, 2 * tq), lambda b, hh, i: (hh, 0, 0, 0))]
                 + [full(a) for a in lams] + [full(subln_g)],
        out_specs=pl.BlockSpec((1, tq, nh * KD), lambda b, hh, i: (b, i, hh)),
        out_shape=jax.ShapeDtypeStruct((bsz, l, n_heads * KD), BF16),
        scratch_shapes=[pltpu.VMEM((nh, l, 2 * KD), BF16), pltpu.VMEM((nh, 2 * tq, LANES), F32),
                        pltpu.VMEM((nh, 2 * tq, 2 * KD), F32)],
        compiler_params=_cparams(("parallel", "parallel", "arbitrary")))(
            qkv, qkv, qkv, bt, *lams, subln_g)


def _sattn_body(pt_ref, qt_ref, *refs, li, pp, n_heads, n_q, lam_init):
    krefs = refs[:pp]
    vrefs = refs[pp:2 * pp]
    kn_ref, vn_ref, bias_ref, lq1, lk1, lq2, lk2, g_ref, o_ref, m_sc, l_sc, acc_sc = refs[2 * pp:]
    jp = pl.program_id(1)
    last = pl.num_programs(1) - 1

    @pl.when(jp == 0)
    def _():
        m_sc[...] = jnp.full(m_sc.shape, NEG_INF, F32)
        l_sc[...] = jnp.zeros(l_sc.shape, F32)
        acc_sc[...] = jnp.zeros(acc_sc.shape, F32)

    def block(kbs, vbs, bias):
        s = jnp.stack([lax.dot_general(qt_ref[0, h], kbs[h], (((1,), (1,)), ((), ())),
                                       preferred_element_type=F32)
                       for h in range(n_heads)]) + bias
        m_prev = m_sc[...]
        m_new = jnp.maximum(m_prev, jnp.max(s, axis=-1, keepdims=True))
        alpha = jnp.exp(m_prev - m_new)
        p = jnp.exp(s - m_new)
        l_sc[...] = alpha * l_sc[...] + jnp.sum(p, axis=-1, keepdims=True)
        pb = p.astype(BF16)
        pv = jnp.stack([jnp.dot(pb[h], vbs[h], preferred_element_type=F32)
                        for h in range(n_heads)])
        acc_sc[...] = alpha * acc_sc[...] + pv
        m_sc[...] = m_new

    def head_rows(page_refs, h):
        rows_h = pl.ds(h, PAGE_SIZE, stride=n_heads)
        return jnp.concatenate([r[0, 0, rows_h, :] for r in page_refs], axis=0).astype(BF16)

    far = bias_ref[0]
    near = jnp.where(jp == last, bias_ref[1], far)
    block([head_rows(krefs, h) for h in range(n_heads)],
          [head_rows(vrefs, h) for h in range(n_heads)],
          jnp.concatenate([far] * (pp - 1) + [near], axis=-1))

    @pl.when(jp == last)
    def _():
        cols = [slice(h * KD, (h + 1) * KD) for h in range(n_heads)]
        block([kn_ref[0, :, c] for c in cols], [vn_ref[0, :, c] for c in cols], bias_ref[2])
        lam = _lambda_full(lq1, lk1, lq2, lk2, li, lam_init)
        g = g_ref[li:li + 1, :]
        o = acc_sc[...] / l_sc[...]
        o = _sub_rms(o[:, :n_q] - lam * o[:, n_q:], g, lam_init)
        for h in range(n_heads):
            o_ref[0, :, cols[h]] = o[h]


def _sample_attn(page_table, qt, cache_k, cache_v, k_new, v_new, bias3, lams, subln_g,
                 *, li, n_heads, n_q, lam_init):
    dbsz, n_pages = page_table.shape
    pp = _pick(n_pages, (PAGES_PER_STEP, 2, 1))
    w = n_heads * KD
    pt_flat = page_table.reshape(-1)

    def page_spec(g):
        return pl.BlockSpec((1, 1, PAGE_SIZE * n_heads, KD),
                            lambda b, j, pt, g=g: (li, pt[b * n_pages + j * pp + g], 0, 0))

    full = lambda a: pl.BlockSpec(a.shape, lambda b, j, pt: (0,) * a.ndim)
    per_b = pl.BlockSpec((1, LANES, w), lambda b, j, pt: (b, 0, 0))
    grid_spec = pltpu.PrefetchScalarGridSpec(
        num_scalar_prefetch=1, grid=(dbsz, n_pages // pp),
        in_specs=[pl.BlockSpec((1, n_heads, 2 * n_q, KD), lambda b, j, pt: (b, 0, 0, 0))]
                 + [page_spec(g) for g in range(pp)] * 2 + [per_b, per_b, full(bias3)]
                 + [full(a) for a in lams] + [full(subln_g)],
        out_specs=pl.BlockSpec((1, n_q, w), lambda b, j, pt: (b, 0, 0)),
        scratch_shapes=[pltpu.VMEM((n_heads, 2 * n_q, 1), F32),
                        pltpu.VMEM((n_heads, 2 * n_q, 1), F32),
                        pltpu.VMEM((n_heads, 2 * n_q, KD), F32)])
    body = functools.partial(_sattn_body, li=li, pp=pp, n_heads=n_heads, n_q=n_q,
                             lam_init=lam_init)
    return pl.pallas_call(
        body, grid_spec=grid_spec,
        out_shape=jax.ShapeDtypeStruct((dbsz, n_q, w), F32),
        compiler_params=_cparams(("parallel", "arbitrary")))(
            pt_flat, qt, *([cache_k] * pp), *([cache_v] * pp), k_new, v_new, bias3,
            *lams, subln_g)


def _block_queries(q, n_heads):
    dbsz, n_q, _ = q.shape
    qh = jnp.transpose(q.reshape(dbsz, n_q, n_heads, KD), (0, 2, 1, 3))
    col_map = (jnp.arange(KD) // DIFF_DH)[None, :] == jnp.arange(2)[:, None]
    t = qh[:, :, None] * col_map[None, None, :, None, :].astype(q.dtype)
    return t.reshape(dbsz, n_heads, 2 * n_q, KD)


def _sample_bias(t5_bias, n_heads, n_q):
    qi = jnp.arange(n_q, dtype=jnp.int32)
    key = jnp.arange(LANES, dtype=jnp.int32)

    def tile(rel):
        b = _bias_of_rel(t5_bias, rel)
        return jnp.concatenate([b, b], axis=1)

    far = tile(jnp.full((n_q, LANES), MAX_DISTANCE, jnp.int32))
    lastp = tile(PAGE_SIZE + qi[:, None] - key[None, :])
    new = tile(jnp.where(key[None, :] < n_q, qi[:, None] - key[None, :], -1))
    return jnp.stack([far, lastp, new])


def _ssd_body(xbc_ref, z_ref, dt_ref, h0_ref, dtb_ref, alog_ref, dx_ref, gn_ref,
              y_ref, hout_ref, hs, ysc, *, n_heads, valid_len):
    c = pl.program_id(1)
    t = SSD_CHUNK
    di = n_heads * SSD_P
    e_per_g = n_heads // SSD_G

    @pl.when(c == 0)
    def _():
        hs[...] = h0_ref[0]

    xbc = xbc_ref[0]
    xs = xbc[:, :di]
    row = lax.broadcasted_iota(jnp.int32, (t, LANES), 0)
    lane = lax.broadcasted_iota(jnp.int32, (t, LANES), 1)
    raw = dt_ref[0] + dtb_ref[...]
    dt = jnp.maximum(raw, 0.0) + jnp.log1p(jnp.exp(-jnp.abs(raw)))
    dt = jnp.where((row + c * t < valid_len) & (lane < n_heads), dt, 0.0)
    da = dt * (-jnp.exp(alog_ref[...]))
    tri = (lax.broadcasted_iota(jnp.int32, (t, t), 0)
           >= lax.broadcasted_iota(jnp.int32, (t, t), 1))
    cum = jnp.dot(tri.astype(F32), da, precision=lax.Precision.HIGHEST,
                  preferred_element_type=F32)
    cum_t = cum.T
    dt_t = dt.T
    cum_last = cum[t - 1:t, :]
    w_end = dt * jnp.exp(cum_last - cum)
    e_cum = jnp.exp(cum)
    e_last = jnp.exp(cum_last)
    xs_bf = xs.astype(BF16)
    xs_t = xs.T.astype(BF16)

    for g in range(SSD_G):
        b_g = xbc[:, di + g * SSD_N:di + (g + 1) * SSD_N]
        c_g = xbc[:, di + (SSD_G + g) * SSD_N:di + (SSD_G + g + 1) * SSD_N]
        cb = lax.dot_general(c_g.astype(BF16), b_g.astype(BF16), (((1,), (1,)), ((), ())),
                             preferred_element_type=F32)
        for e in range(e_per_g):
            h = g * e_per_g + e
            seg = cum[:, h:h + 1] - cum_t[h:h + 1, :]
            decay = jnp.where(tri, jnp.exp(jnp.where(tri, seg, 0.0)), 0.0)
            mat = (cb * decay * dt_t[h:h + 1, :]).astype(BF16)
            x_e = xs_bf[:, h * SSD_P:(h + 1) * SSD_P]
            y_d = jnp.dot(mat, x_e, preferred_element_type=F32)
            h_e = hs[h]
            c_w = (c_g * e_cum[:, h:h + 1]).astype(BF16)
            y_o = lax.dot_general(c_w, h_e.astype(BF16), (((1,), (1,)), ((), ())),
                                  preferred_element_type=F32)
            b_w = (b_g * w_end[:, h:h + 1]).astype(BF16)
            st = jnp.dot(xs_t[h * SSD_P:(h + 1) * SSD_P, :], b_w,
                         preferred_element_type=F32)
            hs[h] = h_e * e_last[:, h:h + 1] + st
            ysc[:, h * SSD_P:(h + 1) * SSD_P] = y_d + y_o

    y = ysc[...] + xs * dx_ref[...]
    z = z_ref[0]
    y = y * (z * jax.nn.sigmoid(z))
    gw = di // SSD_G
    for g in range(SSD_G):
        seg = y[:, g * gw:(g + 1) * gw]
        seg = seg * lax.rsqrt(jnp.mean(seg * seg, axis=-1, keepdims=True) + EPS)
        y_ref[0, :, g * gw:(g + 1) * gw] = (seg * gn_ref[:, g * gw:(g + 1) * gw]).astype(y_ref.dtype)

    @pl.when(c == pl.num_programs(1) - 1)
    def _():
        hout_ref[0] = hs[...]


def _ssd(xbc_act, zxd, z_blk_w, dt_blk, h0, dtb, alog, dx, gn, *, n_heads, valid_len):
    bsz, l, xbc_w = xbc_act.shape
    di = n_heads * SSD_P
    t = SSD_CHUNK
    assert l % t == 0 and z_blk_w == di
    full = lambda a: pl.BlockSpec(a.shape, lambda b, c: (0,) * a.ndim)
    body = functools.partial(_ssd_body, n_heads=n_heads, valid_len=valid_len)
    return pl.pallas_call(
        body, grid=(bsz, l // t),
        in_specs=[pl.BlockSpec((1, t, xbc_w), lambda b, c: (b, c, 0)),
                  pl.BlockSpec((1, t, di), lambda b, c: (b, c, 0)),
                  pl.BlockSpec((1, t, LANES), lambda b, c: (b, c, dt_blk)),
                  pl.BlockSpec((1, n_heads, SSD_P, SSD_N), lambda b, c: (b, 0, 0, 0)),
                  full(dtb), full(alog), full(dx), full(gn)],
        out_specs=[pl.BlockSpec((1, t, di), lambda b, c: (b, c, 0)),
                   pl.BlockSpec((1, n_heads, SSD_P, SSD_N), lambda b, c: (b, 0, 0, 0))],
        out_shape=[jax.ShapeDtypeStruct((bsz, l, di), BF16),
                   jax.ShapeDtypeStruct((bsz, n_heads, SSD_P, SSD_N), F32)],
        scratch_shapes=[pltpu.VMEM((n_heads, SSD_P, SSD_N), F32), pltpu.VMEM((t, di), F32)],
        compiler_params=_cparams(("parallel", "arbitrary")))(
            xbc_act, zxd, zxd, h0, dtb, alog, dx, gn)


def _epi_glu(accs, tiles, rows, j):
    a, b = accs
    return (a * jax.nn.sigmoid(b),)


def _epi_id(accs, tiles, rows, j):
    return (accs[0],)


def _epi_sigmoid(accs, tiles, rows, j):
    return (jax.nn.sigmoid(accs[0]),)


def _epi_merge(accs, tiles, rows, j):
    ya, yb, yc = accs
    ga, gb, gc = (t.astype(F32) for t in tiles)
    return (ga * ya + gb * yb + gc * yc,)


def _epi_resid(accs, tiles, rows, j):
    return (tiles[0] + rows[0] * accs[0],)


def _trunk(x, mod, prev_a, prev_c, prev_h, prev_f, wts, p, dims, *, paged=None):
    bsz, l, d = x.shape
    m = bsz * l
    depth, n_heads, conv_c, ssd_h, d_ff = (dims[k] for k in
                                           ("depth", "n_heads", "conv_c", "ssd_h", "d_ff"))
    att_w = n_heads * KD
    ssd_di = ssd_h * SSD_P
    ssd_xbc = ssd_di + 2 * SSD_G * SSD_N
    conv_k = p["conv_a_w"].shape[1]
    ssd_k = p["ssd_conv_w"].shape[1]
    ffn_k = p["ffn_conv_w"].shape[1]
    prompt = paged is None
    tm = _pick(l, (512, 256, 128)) if prompt else m
    tpb = l // tm if prompt else 1
    rpb = l if prompt else None
    t_seq = _pick(l, (128,)) if prompt else l
    lams = (p["lambda_q1"], p["lambda_k1"], p["lambda_q2"], p["lambda_k2"])
    scale = DIFF_DH ** -0.5

    xf = x.reshape(m, d)
    out_a, out_c, out_h, out_f = [], [], [], []
    kst = jnp.zeros((depth, m, att_w), F32)
    vst = jnp.zeros((depth, m, att_w), F32)
    for li in range(depth):
        w = wts[li]
        md = mod[li]
        lam_init = 0.8 - 0.6 * math.exp(-0.3 * li)
        h = _norm_mod(xf, p["norm1_g"][li][None], md, 1, 0, tm=tm, tpb=tpb)

        (u,) = _mm([h], [(w["glu"], 0, 0), (w["glu"], conv_c, 0)], _epi_glu, [F32], conv_c,
                   rows_per_batch=rpb)
        qkv_bf, kst, vst = _qkv(h, w["qkv"], kst, vst, li=li, scale=scale, rows_per_batch=rpb)
        zxd_w = w["zxd"].shape[1]
        (zxd,) = _mm([h], [(w["zxd"], 0, 0)], _epi_id, [F32], zxd_w, rows_per_batch=rpb)
        (gates,) = _mm([h], [(w["gates"], 0, 0)], _epi_sigmoid, [BF16], 3 * d, rows_per_batch=rpb)

        u3 = u.reshape(bsz, l, conv_c)
        out_a.append(_conv_state(prev_a[li], u3, conv_k - 1))
        a_act = _dwconv(u3, 0, _pad_prev(prev_a[li], 32), p["conv_a_w"][li],
                        p["conv_a_b"][li][None], _epi_conv_a, BF16, c=conv_c, t=t_seq,
                        tc=conv_c, halo=32,
                        prows=(p["ln_a_g"][li][None], p["ln_a_b"][li][None]))

        if prompt:
            o = _prompt_attn(qkv_bf.reshape(bsz, l, 3 * att_w), dims["bias_tiles"], lams,
                             p["subln_g"], li=li, n_heads=n_heads, lam_init=lam_init)
        else:
            qb = qkv_bf.reshape(bsz, l, 3 * att_w)
            qt = _block_queries(qb[..., :att_w], n_heads)
            pad = ((0, 0), (0, LANES - l), (0, 0))
            k_new = jnp.pad(qb[..., att_w:2 * att_w], pad)
            v_new = jnp.pad(qb[..., 2 * att_w:], pad)
            o = _sample_attn(paged["page_table"], qt, paged["cache_k"], paged["cache_v"],
                             k_new, v_new, dims["bias3"], lams, p["subln_g"], li=li,
                             n_heads=n_heads, n_q=l, lam_init=lam_init).astype(BF16)
        o = o.reshape(m, att_w)

        zxd3 = zxd.reshape(bsz, l, zxd_w)
        xbc_raw = zxd3[..., ssd_di:ssd_di + ssd_xbc]
        out_c.append(_conv_state(prev_c[li], xbc_raw, ssd_k - 1))
        tc = _pick(math.gcd(ssd_di, ssd_xbc), (512, 256, 128))
        xbc_act = _dwconv(zxd3, ssd_di // tc, _pad_prev(prev_c[li], 8), p["ssd_conv_w"][li],
                          p["ssd_conv_b"][li][None], _epi_silu, F32, c=ssd_xbc,
                          t=_pick(l, (512, 256, 128)) if prompt else l, tc=tc, halo=8)
        if prompt:
            xbc_in, zxd_in = xbc_act, zxd3
        else:
            pad = ((0, 0), (0, SSD_CHUNK - l), (0, 0))
            xbc_in, zxd_in = jnp.pad(xbc_act, pad), jnp.pad(zxd3, pad)
        lane_pad = lambda a: jnp.pad(a, (0, LANES - a.shape[0]))[None]
        y_c, h_t = _ssd(xbc_in, zxd_in, ssd_di, (ssd_di + ssd_xbc) // LANES, prev_h[li],
                        lane_pad(p["dt_bias"][li]), lane_pad(p["a_log"][li]),
                        jnp.repeat(p["d_skip"][li], SSD_P)[None], p["ssd_norm_g"][li][None],
                        n_heads=ssd_h, valid_len=l)
        out_h.append(h_t)
        y_c = y_c[:, :l].reshape(m, ssd_di)

        (merged,) = _mm([a_act.reshape(m, conv_c), o, y_c],
                        [(w["proj_a"], 0, 0), (w["proj_b"], 0, 1), (w["proj_c"], 0, 2)],
                        _epi_merge, [BF16], d, tiles=[(gates, 0), (gates, d), (gates, 2 * d)],
                        rows_per_batch=rpb)
        (xf,) = _mm([merged], [(w["out"], 0, 0)], _epi_resid, [F32], d, tiles=[(xf, 0)],
                    rows=[(md, 2 * d)], rows_per_batch=rpb)

        h2 = _norm_mod(xf, p["norm2_g"][li][None], md, 4, 3, tm=tm, tpb=tpb)
        if prompt:
            act = _upconv(h2, w["up"], _pad_prev(prev_f[li], UP_HALO), p["ffn_conv_w"][li],
                          p["ffn_conv_b"][li][None], d_ff=d_ff, rows_per_batch=rpb)
            assert l >= UP_HALO
            h2_tail = h2.reshape(bsz, l, d)[:, l - UP_HALO:].reshape(bsz * UP_HALO, d)
            (fa_tail,) = _mm([h2_tail], [(w["up"], 0, 0)], _epi_id, [F32], d_ff)
            out_f.append(fa_tail.reshape(bsz, UP_HALO, d_ff)[:, UP_HALO - (ffn_k - 1):])
        else:
            (up,) = _mm([h2], [(w["up"], 0, 0)], _epi_id, [F32], 2 * d_ff)
            up3 = up.reshape(bsz, l, 2 * d_ff)
            out_f.append(_conv_state(prev_f[li], up3[..., :d_ff], ffn_k - 1))
            tc = _pick(d_ff, (512, 256, 128))
            act = _dwconv(up3, 0, _pad_prev(prev_f[li], 8), p["ffn_conv_w"][li],
                          p["ffn_conv_b"][li][None], _epi_ffn, BF16, c=d_ff, t=t_seq, tc=tc,
                          halo=8, xtiles=[(up3, d_ff // tc)])
        (xf,) = _mm([act.reshape(m, d_ff)], [(w["down"], 0, 0)], _epi_resid, [F32], d,
                    tiles=[(xf, 0)], rows=[(md, 5 * d)], rows_per_batch=rpb)

    y_out = _rms(xf, p["final_g"][None], tm=tm).reshape(bsz, l, d)
    kv_shape = (depth, bsz, l, n_heads, KD)
    return (y_out, kst.reshape(kv_shape), vst.reshape(kv_shape), jnp.stack(out_a), jnp.stack(out_c),
            jnp.stack(out_h), jnp.stack(out_f))


def _split_weights(p, li, dims):
    conv_c, att_w = dims["conv_c"], dims["n_heads"] * KD
    ssd_di = dims["ssd_h"] * SSD_P
    ssd_xbc = ssd_di + 2 * SSD_G * SSD_N
    w_in = p["w_in"][li]
    c0 = 2 * conv_c
    c1 = c0 + 3 * att_w
    c2 = c1 + ssd_di + ssd_xbc
    c3 = c2 + dims["ssd_h"]
    dt_cols = jnp.pad(w_in[:, c2:c3], ((0, 0), (0, LANES - dims["ssd_h"])))
    bf = lambda a: a.astype(BF16)
    return dict(glu=bf(w_in[:, :c0]), qkv=bf(w_in[:, c0:c1]),
                zxd=bf(jnp.concatenate([w_in[:, c1:c2], dt_cols], axis=1)),
                gates=bf(w_in[:, c3:]), proj_a=bf(p["w_proj_a"][li]), proj_b=bf(p["w_proj_b"][li]),
                proj_c=bf(p["w_proj_c"][li]), out=bf(p["w_out"][li]), up=bf(p["ffn_up"][li]),
                down=bf(p["ffn_down"][li]))


def kernel(x_prompt, x_sample, cache_k, cache_v, page_table, state_conv_a, state_ssd_conv,
           state_ssm, state_ffn_conv, c_prompt, c_sample, w_ada, b_ada, norm1_g, w_in,
           conv_a_w, conv_a_b, ln_a_g, ln_a_b, w_proj_a, t5_bias, lambda_q1, lambda_k1,
           lambda_q2, lambda_k2, subln_g, w_proj_b, ssd_conv_w, ssd_conv_b, dt_bias, a_log,
           d_skip, ssd_norm_g, w_proj_c, w_out, norm2_g, ffn_up, ffn_conv_w, ffn_conv_b,
           ffn_down, final_g):
    p = dict(w_ada=w_ada, b_ada=b_ada, norm1_g=norm1_g, w_in=w_in, conv_a_w=conv_a_w,
             conv_a_b=conv_a_b, ln_a_g=ln_a_g, ln_a_b=ln_a_b, w_proj_a=w_proj_a, t5_bias=t5_bias,
             lambda_q1=lambda_q1, lambda_k1=lambda_k1, lambda_q2=lambda_q2, lambda_k2=lambda_k2,
             subln_g=subln_g, w_proj_b=w_proj_b, ssd_conv_w=ssd_conv_w, ssd_conv_b=ssd_conv_b,
             dt_bias=dt_bias, a_log=a_log, d_skip=d_skip, ssd_norm_g=ssd_norm_g, w_proj_c=w_proj_c,
             w_out=w_out, norm2_g=norm2_g, ffn_up=ffn_up, ffn_conv_w=ffn_conv_w,
             ffn_conv_b=ffn_conv_b, ffn_down=ffn_down, final_g=final_g)
    bp, lp, d = x_prompt.shape
    bs, ls, _ = x_sample.shape
    depth = w_ada.shape[0]
    n_heads = cache_k.shape[3]
    n_pages = page_table.shape[1]
    dims = dict(depth=depth, n_heads=n_heads, conv_c=conv_a_w.shape[2], ssd_h=dt_bias.shape[1],
                d_ff=ffn_conv_w.shape[2])
    f32 = x_prompt.dtype

    tq = _pick(lp, (ATTN_TQ, 128))
    assert tq >= MAX_DISTANCE
    qi = jnp.arange(tq, dtype=jnp.int32)
    rel0 = qi[:, None] - qi[None, :]
    b_diag, b_prev = _bias_of_rel(t5_bias, rel0), _bias_of_rel(t5_bias, rel0 + tq)
    dims["bias_tiles"] = jnp.stack(
        [jnp.concatenate([b_prev, b_diag], axis=-1),
         jnp.concatenate([b_diag, jnp.full_like(b_diag, NEG_INF)], axis=-1)], axis=1)
    dims["bias3"] = _sample_bias(t5_bias, n_heads, ls)

    rows = -(-(bp + bs) // 8) * 8
    c_all = jnp.pad(jnp.concatenate([c_prompt, c_sample], axis=0), ((0, rows - bp - bs), (0, 0)))
    mod = _ada(c_all, w_ada, b_ada)
    mod_p = mod[:, :bp, None, :]
    mod_s = jnp.repeat(mod[:, bp:bp + bs], ls, axis=1)[:, None]

    wts = [_split_weights(p, li, dims) for li in range(depth)]
    zero = lambda *s: jnp.zeros((depth, bp) + s, f32)
    res_p = _trunk(x_prompt, mod_p, zero(conv_a_w.shape[1] - 1, dims["conv_c"]),
                   zero(ssd_conv_w.shape[1] - 1, ssd_conv_w.shape[2]),
                   zero(dims["ssd_h"], SSD_P, SSD_N), zero(ffn_conv_w.shape[1] - 1, dims["d_ff"]),
                   wts, p, dims)
    paged = dict(page_table=page_table,
                 cache_k=cache_k.reshape(depth, -1, PAGE_SIZE * n_heads, KD),
                 cache_v=cache_v.reshape(depth, -1, PAGE_SIZE * n_heads, KD))
    res_s = _trunk(x_sample, mod_s, state_conv_a, state_ssd_conv, state_ssm, state_ffn_conv,
                   wts, p, dims, paged=paged)
    y_p, k_p, v_p, a_p, c_p, h_p, f_p = res_p
    y_s, k_s, v_s, a_s, c_s, h_s, f_s = res_s
    return (y_p, y_s, k_p, v_p, k_s, v_s, a_p, a_s, c_p, c_s, h_p, h_s, f_p, f_s)
```

```python
import functools
import math

import jax
import jax.numpy as jnp
from jax import lax
from jax.experimental import pallas as pl
from jax.experimental.pallas import tpu as pltpu

F32 = jnp.float32
BF16 = jnp.bfloat16

EPS = 1e-6
LN_EPS = 1e-5
NEG_INF = -1e30
DIFF_DH = 64
KD = 2 * DIFF_DH
NUM_BUCKETS = 32
MAX_DISTANCE = 128
SSD_P = 64
SSD_N = 128
SSD_G = 2
SSD_CHUNK = 128
PAGE_SIZE = 128
LANES = 128
VMEM_LIMIT = 56 * 1024 * 1024
VMEM_BLOCK_BUDGET = 46 * 1024 * 1024
MIN_TN = 512
MM_ROW_CHUNK = 512

ATTN_TQ = 256
PAGES_PER_STEP = 8


def _cparams(sem):
    return pltpu.CompilerParams(dimension_semantics=sem, vmem_limit_bytes=VMEM_LIMIT)


def _pick(n, cands):
    for c in cands:
        if n % c == 0:
            return c
    return n


def _fit_tiles(m_cands, n_cands, block_bytes):
    wide = [c for c in n_cands if c >= MIN_TN] or list(n_cands)
    for cols in (wide, list(n_cands)):
        for tm in m_cands:
            for tn in cols:
                if block_bytes(tm, tn) <= VMEM_BLOCK_BUDGET:
                    return tm, tn
    return m_cands[-1], n_cands[-1]


def _row_cands(rows_per_batch, whole=None):
    if whole is not None:
        return [whole]
    return [c for c in (2048, 1024, 512, 256, 128) if rows_per_batch % c == 0] or [rows_per_batch]


def _col_cands(n, offsets=()):
    cands = [c for c in (1024, 768, 512, 256, 896, 640, 384, 128)
             if n % c == 0 and all(o % c == 0 for o in offsets)]
    return cands or [n]


def _mm_body(*refs, nl, lhs_of, nt, nr, epi, rm):
    nw = len(lhs_of)
    lrefs = refs[:nl]
    wrefs = refs[nl:nl + nw]
    trefs = refs[nl + nw:nl + nw + nt]
    rrefs = refs[nl + nw + nt:nl + nw + nt + nr]
    orefs = refs[nl + nw + nt + nr:]
    tm = orefs[0].shape[0]
    j = pl.program_id(1)
    for m0 in range(0, tm, rm):
        rs = slice(m0, m0 + rm)
        accs = [jnp.dot(lrefs[li][rs, :], w[...], preferred_element_type=F32)
                for w, li in zip(wrefs, lhs_of)]
        rows = [r[0] if r.shape[1] == 1 else r[0, rs, :] for r in rrefs]
        res = epi(accs, [t[rs, :] for t in trefs], rows, j)
        for o, r in zip(orefs, res):
            o[rs, :] = r.astype(o.dtype)


def _mm(lhs, rhs, epi, out_dtypes, n, *, tiles=(), rows=(), rows_per_batch=None):
    m = lhs[0].shape[0]
    offsets = [off for _, _, off, _ in rhs] + [off for _, off in tiles] + [off for _, off in rows]

    def block_bytes(tm, tn):
        b = sum(tm * a.shape[1] * a.dtype.itemsize for a in lhs)
        b += sum(a.shape[1] * tn * a.dtype.itemsize for a, _, _, _ in rhs)
        b += sum(tm * tn * a.dtype.itemsize for a, _ in tiles)
        b += sum(tm * tn * jnp.dtype(d).itemsize for d in out_dtypes)
        return 2 * b + len(rhs) * min(tm, MM_ROW_CHUNK) * tn * 4

    tm, tn = _fit_tiles(_row_cands(rows_per_batch, None if rows_per_batch else m),
                        _col_cands(n, offsets), block_bytes)
    tpb = rows_per_batch // tm if rows_per_batch else 1
    assert m % tm == 0 and n % tn == 0
    in_specs, args = [], []
    for a in lhs:
        in_specs.append(pl.BlockSpec((tm, a.shape[1]), lambda i, j: (i, 0)))
        args.append(a)
    for a, li, off, _ in rhs:
        in_specs.append(pl.BlockSpec((None, a.shape[1], tn),
                                     lambda i, j, li=li, o=off // tn: (li, 0, j + o)))
        args.append(a)
    for a, off in tiles:
        in_specs.append(pl.BlockSpec((tm, tn), lambda i, j, o=off // tn: (i, j + o)))
        args.append(a)
    for a, off in rows:
        assert a.shape[1] == 1 or (a.shape[0] == 1 and a.shape[1] == tm == m)
        in_specs.append(pl.BlockSpec((1, a.shape[1], tn),
                                     lambda i, j, o=off // tn: (i // tpb, 0, j + o)))
        args.append(a)
    out_specs = [pl.BlockSpec((tm, tn), lambda i, j: (i, j)) for _ in out_dtypes]
    out_shape = [jax.ShapeDtypeStruct((m, n), d) for d in out_dtypes]
    body = functools.partial(_mm_body, nl=len(lhs), lhs_of=tuple(k for _, _, _, k in rhs),
                             nt=len(tiles), nr=len(rows), epi=epi, rm=min(tm, MM_ROW_CHUNK))
    return pl.pallas_call(
        body, grid=(m // tm, n // tn), in_specs=in_specs, out_specs=out_specs,
        out_shape=out_shape, compiler_params=_cparams(("parallel", "arbitrary")))(*args)


def _qkv_body(h_ref, w_ref, kin_ref, vin_ref, qkv_ref, kst_ref, vst_ref, *, nq, scale):
    del kin_ref, vin_ref
    j = pl.program_id(1)
    acc = jnp.dot(h_ref[...], w_ref[...], preferred_element_type=F32)
    qkv_ref[...] = (acc * jnp.where(j < nq, scale, 1.0)).astype(qkv_ref.dtype)

    @pl.when((j >= nq) & (j < 2 * nq))
    def _():
        kst_ref[0] = acc

    @pl.when(j >= 2 * nq)
    def _():
        vst_ref[0] = acc


def _qkv(h, w, col0, kst, vst, *, li, scale, rows_per_batch=None):
    m, d = h.shape
    wdt = kst.shape[2]
    block_bytes = lambda tm, tn: 2 * (tm * d * 2 + d * tn * 2 + tm * tn * (2 + 4 + 4)) + tm * tn * 4
    tm, tn = _fit_tiles(_row_cands(rows_per_batch, None if rows_per_batch else m),
                        _col_cands(wdt, (col0,)), block_bytes)
    nq = wdt // tn
    j0 = col0 // tn
    assert m % tm == 0 and wdt % tn == 0 and col0 % tn == 0
    k_idx = lambda i, j: (li, i, jnp.clip(j - nq, 0, nq - 1))
    v_idx = lambda i, j: (li, i, jnp.clip(j - 2 * nq, 0, nq - 1))
    body = functools.partial(_qkv_body, nq=nq, scale=scale)
    return pl.pallas_call(
        body, grid=(m // tm, 3 * nq),
        in_specs=[pl.BlockSpec((tm, d), lambda i, j: (i, 0)),
                  pl.BlockSpec((None, d, tn), lambda i, j: (li, 0, j0 + j)),
                  pl.BlockSpec(memory_space=pl.ANY), pl.BlockSpec(memory_space=pl.ANY)],
        out_specs=[pl.BlockSpec((tm, tn), lambda i, j: (i, j)),
                   pl.BlockSpec((1, tm, tn), k_idx), pl.BlockSpec((1, tm, tn), v_idx)],
        out_shape=[jax.ShapeDtypeStruct((m, 3 * wdt), BF16),
                   jax.ShapeDtypeStruct(kst.shape, F32), jax.ShapeDtypeStruct(vst.shape, F32)],
        input_output_aliases={2: 1, 3: 2},
        compiler_params=_cparams(("arbitrary", "arbitrary")))(h, w, kst, vst)


def _ada_body(c_ref, w_ref, b_ref, o_ref):
    c = c_ref[...]
    s = (c * jax.nn.sigmoid(c)).astype(BF16)
    o_ref[0] = jnp.dot(s, w_ref[0].astype(BF16), preferred_element_type=F32) + b_ref[0]


def _ada(c_all, w_ada, b_ada):
    depth, d, n6 = w_ada.shape
    rows = c_all.shape[0]
    tn = _pick(n6, (1024, 512, 256, 128))
    return pl.pallas_call(
        _ada_body, grid=(depth, n6 // tn),
        in_specs=[pl.BlockSpec((rows, d), lambda l, j: (0, 0)),
                  pl.BlockSpec((1, d, tn), lambda l, j: (l, 0, j)),
                  pl.BlockSpec((1, 1, tn), lambda l, j: (l, 0, j))],
        out_specs=pl.BlockSpec((1, rows, tn), lambda l, j: (l, 0, j)),
        out_shape=jax.ShapeDtypeStruct((depth, rows, n6), F32),
        compiler_params=_cparams(("arbitrary", "arbitrary")))(c_all, w_ada, b_ada[:, None, :])


def _norm_mod_body(x_ref, g_ref, sc_ref, sh_ref, o_ref):
    x = x_ref[...]
    y = x * lax.rsqrt(jnp.mean(x * x, axis=-1, keepdims=True) + EPS) * g_ref[...]
    o_ref[...] = (y * (1.0 + sc_ref[0]) + sh_ref[0]).astype(o_ref.dtype)


def _norm_mod(x, g, mod, sc_blk, sh_blk, *, tm, tpb):
    m, d = x.shape
    r = mod.shape[1]
    return pl.pallas_call(
        _norm_mod_body, grid=(m // tm,),
        in_specs=[pl.BlockSpec((tm, d), lambda i: (i, 0)),
                  pl.BlockSpec((1, d), lambda i: (0, 0)),
                  pl.BlockSpec((1, r, d), lambda i: (i // tpb, 0, sc_blk)),
                  pl.BlockSpec((1, r, d), lambda i: (i // tpb, 0, sh_blk))],
        out_specs=pl.BlockSpec((tm, d), lambda i: (i, 0)),
        out_shape=jax.ShapeDtypeStruct((m, d), BF16),
        compiler_params=_cparams(("parallel",)))(x, g, mod, mod)


def _rms_body(x_ref, g_ref, o_ref):
    x = x_ref[...]
    o_ref[...] = x * lax.rsqrt(jnp.mean(x * x, axis=-1, keepdims=True) + EPS) * g_ref[...]


def _rms(x, g, *, tm):
    m, d = x.shape
    return pl.pallas_call(
        _rms_body, grid=(m // tm,),
        in_specs=[pl.BlockSpec((tm, d), lambda i: (i, 0)),
                  pl.BlockSpec((1, d), lambda i: (0, 0))],
        out_specs=pl.BlockSpec((tm, d), lambda i: (i, 0)),
        out_shape=jax.ShapeDtypeStruct((m, d), F32),
        compiler_params=_cparams(("parallel",)))(x, g)


def _dwconv_body(*refs, kc, halo, t, tc, n_seq_tiles, nx, npr, epi):
    u_ref, prev_ref, w_ref, b_ref = refs[:4]
    xrefs = refs[4:4 + nx]
    prefs = refs[4 + nx:4 + nx + npr]
    o_ref = refs[4 + nx + npr]
    win, cv = refs[4 + nx + npr + 1:]

    @pl.when(pl.program_id(2) == 0)
    def _():
        win[0:halo, :] = prev_ref[0]

    win[halo:halo + t, :] = u_ref[0]
    base = halo - (kc - 1)
    rc = min(t, 64)
    cc = min(tc, 256)
    for r0 in range(0, t, rc):
        for c0 in range(0, tc, cc):
            acc = jnp.broadcast_to(b_ref[:, c0:c0 + cc], (rc, cc))
            wlen = rc + halo
            window = win[r0:r0 + wlen, c0:c0 + cc]
            for r in range(8):
                taps = [j for j in range(kc) if (base + j) % 8 == r]
                if not taps:
                    continue
                sh = window if r == 0 else pltpu.roll(window, wlen - r, 0)
                for j in taps:
                    a = (base + j) // 8
                    acc = acc + w_ref[j:j + 1, c0:c0 + cc] * sh[8 * a:8 * a + rc]
            cv[r0:r0 + rc, c0:c0 + cc] = acc
    o_ref[0] = epi(cv[...], [x[0] for x in xrefs], [p[...] for p in prefs]).astype(o_ref.dtype)
    if n_seq_tiles > 1:
        win[0:halo, :] = win[t:t + halo, :]


def _dwconv(u, u_blk, prev, w, b, epi, out_dtype, *, c, t, tc, halo, xtiles=(), prows=()):
    bsz, l, _ = u.shape
    kc = w.shape[0]
    assert l % t == 0 and c % tc == 0 and halo >= kc - 1 and (t >= halo or l == t)
    nlt = l // t
    in_specs = [pl.BlockSpec((1, t, tc), lambda bi, ci, li: (bi, li, ci + u_blk)),
                pl.BlockSpec((1, halo, tc), lambda bi, ci, li: (bi, 0, ci)),
                pl.BlockSpec((kc, tc), lambda bi, ci, li: (0, ci)),
                pl.BlockSpec((1, tc), lambda bi, ci, li: (0, ci))]
    args = [u, prev, w, b]
    for a, off in xtiles:
        in_specs.append(pl.BlockSpec((1, t, tc), lambda bi, ci, li, off=off: (bi, li, ci + off)))
        args.append(a)
    for a in prows:
        in_specs.append(pl.BlockSpec((1, tc), lambda bi, ci, li: (0, ci)))
        args.append(a)
    body = functools.partial(_dwconv_body, kc=kc, halo=halo, t=t, tc=tc, n_seq_tiles=nlt,
                             nx=len(xtiles), npr=len(prows), epi=epi)
    return pl.pallas_call(
        body, grid=(bsz, c // tc, nlt), in_specs=in_specs,
        out_specs=pl.BlockSpec((1, t, tc), lambda bi, ci, li: (bi, li, ci)),
        out_shape=jax.ShapeDtypeStruct((bsz, l, c), out_dtype),
        scratch_shapes=[pltpu.VMEM((halo + t, tc), F32), pltpu.VMEM((t, tc), F32)],
        compiler_params=_cparams(("parallel", "parallel", "arbitrary")))(*args)


def _epi_conv_a(cv, xs, ps):
    g, b = ps
    mu = jnp.mean(cv, axis=-1, keepdims=True)
    d = cv - mu
    var = jnp.mean(d * d, axis=-1, keepdims=True)
    y = d * lax.rsqrt(var + LN_EPS) * g + b
    return y * jax.nn.sigmoid(y)


def _epi_silu(cv, xs, ps):
    return cv * jax.nn.sigmoid(cv)


def _epi_ffn(cv, xs, ps):
    return jax.nn.gelu(cv) * xs[0]


UP_HALO = 8
UP_MAX_TM = 512


def _upconv_body(h_ref, wa_ref, wb_ref, prev_ref, cw_ref, cb_ref, act_ref,
                 win, carry, *, tm, tn, tpb, kc):
    i = pl.program_id(0)
    j = pl.program_id(1)
    first = i % tpb == 0

    @pl.when(first)
    def _():
        win[0:UP_HALO, :] = prev_ref[0]

    @pl.when(jnp.logical_not(first))
    def _():
        win[0:UP_HALO, :] = carry[j]

    base = UP_HALO - (kc - 1)
    rm = min(tm, 256)
    rc = min(rm, 64)
    cc = min(tn, 256)
    for m0 in range(0, tm, rm):
        h = h_ref[m0:m0 + rm, :]
        fa = jnp.dot(h, wa_ref[...], preferred_element_type=F32)
        fb = jnp.dot(h, wb_ref[...], preferred_element_type=F32)
        win[UP_HALO + m0:UP_HALO + m0 + rm, :] = fa
        for r0 in range(0, rm, rc):
            for c0 in range(0, tn, cc):
                acc = jnp.broadcast_to(cb_ref[:, c0:c0 + cc], (rc, cc))
                for t in range(kc):
                    start = base + m0 + r0 + t
                    acc = acc + cw_ref[t:t + 1, c0:c0 + cc] * win[start:start + rc, c0:c0 + cc]
                act_ref[m0 + r0:m0 + r0 + rc, c0:c0 + cc] = (
                    jax.nn.gelu(acc) * fb[r0:r0 + rc, c0:c0 + cc]).astype(act_ref.dtype)
    carry[j] = win[tm:tm + UP_HALO, :]


def _upconv(h, w_up, li, prev, cw, cb, *, d_ff, rows_per_batch):
    m, d = h.shape
    kc = cw.shape[0]
    block_bytes = lambda tm, tn: (2 * (tm * d * 2 + 2 * d * tn * 2 + tm * tn * 2)
                                  + (tm + UP_HALO) * tn * 4 + 2 * min(tm, 256) * tn * 4)
    rows = [c for c in _row_cands(rows_per_batch) if c <= UP_MAX_TM] or [rows_per_batch]
    tm, tn = _fit_tiles(rows, _col_cands(d_ff), block_bytes)
    tpb = rows_per_batch // tm
    nj = d_ff // tn
    assert m % tm == 0 and d_ff % tn == 0 and kc - 1 <= UP_HALO <= tm
    body = functools.partial(_upconv_body, tm=tm, tn=tn, tpb=tpb, kc=kc)
    return pl.pallas_call(
        body, grid=(m // tm, nj),
        in_specs=[pl.BlockSpec((tm, d), lambda i, j: (i, 0)),
                  pl.BlockSpec((None, d, tn), lambda i, j: (li, 0, j)),
                  pl.BlockSpec((None, d, tn), lambda i, j: (li, 0, j + nj)),
                  pl.BlockSpec((1, UP_HALO, tn), lambda i, j: (i // tpb, 0, j)),
                  pl.BlockSpec((kc, tn), lambda i, j: (0, j)),
                  pl.BlockSpec((1, tn), lambda i, j: (0, j))],
        out_specs=pl.BlockSpec((tm, tn), lambda i, j: (i, j)),
        out_shape=jax.ShapeDtypeStruct((m, d_ff), BF16),
        scratch_shapes=[pltpu.VMEM((UP_HALO + tm, tn), F32), pltpu.VMEM((nj, UP_HALO, tn), F32)],
        compiler_params=_cparams(("arbitrary", "arbitrary")))(h, w_up, w_up, prev, cw, cb)


def _conv_state(prev, u, k1):
    l = u.shape[1]
    if l >= k1:
        return u[:, l - k1:]
    return jnp.concatenate([prev[:, l:], u], axis=1)


def _pad_prev(prev, halo):
    return jnp.pad(prev, ((0, 0), (halo - prev.shape[1], 0), (0, 0)))


def _t5_bucket(rel):
    n = jnp.maximum(rel, 0)
    max_exact = NUM_BUCKETS // 2
    nf = jnp.maximum(n, 1).astype(F32)
    large = max_exact + (jnp.log(nf / max_exact) / math.log(MAX_DISTANCE / max_exact)
                         * (NUM_BUCKETS - max_exact)).astype(jnp.int32)
    large = jnp.minimum(large, NUM_BUCKETS - 1)
    return jnp.where(n < max_exact, n, large)


def _bias_of_rel(t5_bias, rel):
    bucket = _t5_bucket(rel)
    table = t5_bias.astype(F32)
    expand = (slice(None),) + (None,) * rel.ndim
    bias = jnp.zeros((table.shape[1],) + rel.shape, F32)
    for b in range(NUM_BUCKETS):
        bias = jnp.where((bucket == b)[None], table[b][expand], bias)
    return jnp.where((rel >= 0)[None], bias, NEG_INF)


def _lambda_full(lq1, lk1, lq2, lk2, li, lam_init):
    a = jnp.sum(lq1[li:li + 1, :] * lk1[li:li + 1, :], axis=-1, keepdims=True)
    b = jnp.sum(lq2[li:li + 1, :] * lk2[li:li + 1, :], axis=-1, keepdims=True)
    return jnp.exp(a) - jnp.exp(b) + lam_init


def _softmax_step(s, v, m_sc, l_sc, acc_sc):
    m_prev = m_sc[...]
    m_new = jnp.maximum(m_prev, jnp.max(s, axis=-1, keepdims=True))
    alpha = jnp.exp(m_prev - m_new)
    p = jnp.exp(s - m_new)
    l_sc[...] = alpha * l_sc[...] + jnp.sum(p, axis=-1, keepdims=True)
    acc_sc[...] = alpha * acc_sc[...] + jnp.dot(p.astype(BF16), v, preferred_element_type=F32)
    m_sc[...] = m_new


def _sub_rms(o, g, lam_init):
    y = o * lax.rsqrt(jnp.mean(o * o, axis=-1, keepdims=True) + EPS) * g
    return y * (1.0 - lam_init)


def _pattn_body(q_ref, k_ref, v_ref, bt_ref, lq1, lk1, lq2, lk2, g_ref, o_ref,
                vp_sc, m_sc, acc_sc, *, li, tq, nh, lam_init):
    i = pl.program_id(2)

    @pl.when(i == 0)
    def _():
        for a in range(nh):
            vp_sc[a, :, :KD] = v_ref[0, :, a * KD:(a + 1) * KD]
            vp_sc[a, :, KD:] = jnp.ones((vp_sc.shape[1], KD), BF16)

    m_sc[...] = jnp.full(m_sc.shape, NEG_INF, F32)
    acc_sc[...] = jnp.zeros(acc_sc.shape, F32)
    lane = lax.broadcasted_iota(jnp.int32, (tq, KD), 1)
    qqs = []
    for a in range(nh):
        q = q_ref[0, :, a * KD:(a + 1) * KD]
        zero = jnp.zeros_like(q)
        qqs.append(jnp.concatenate([jnp.where(lane < DIFF_DH, q, zero),
                                    jnp.where(lane >= DIFF_DH, q, zero)], axis=0))

    def block(off, width, bias_of_head):
        for a in range(nh):
            kb = k_ref[0, pl.ds(off, width), a * KD:(a + 1) * KD]
            vb = vp_sc[a, pl.ds(off, width), :]
            s = lax.dot_general(qqs[a], kb, (((1,), (1,)), ((), ())),
                                preferred_element_type=F32) + bias_of_head(a)
            m_prev = m_sc[a]
            m_new = jnp.maximum(m_prev, jnp.max(s, axis=-1, keepdims=True))
            alpha = jnp.exp(m_prev - m_new)
            p = jnp.exp(s - jnp.concatenate([m_new] * (width // LANES), axis=1))
            acc_sc[a] = (jnp.concatenate([alpha, alpha], axis=1) * acc_sc[a]
                         + jnp.dot(p.astype(BF16), vb, preferred_element_type=F32))
            m_sc[a] = m_new

    n_far = jnp.maximum(i - 1, 0)
    far_of_head = lambda a: bt_ref[a, 0, tq - 1:tq, 0:1]

    def far_body(j, carry):
        block(pl.multiple_of(j * 2 * tq, 2 * tq), 2 * tq, far_of_head)
        return carry

    lax.fori_loop(0, n_far // 2, far_body, 0)

    @pl.when(n_far % 2 == 1)
    def _():
        block(pl.multiple_of((n_far - 1) * tq, tq), tq, far_of_head)

    first = (i == 0).astype(jnp.int32)

    def near_of_head(a):
        b = bt_ref[a, first]
        return jnp.concatenate([b, b], axis=0)

    block(pl.multiple_of(n_far * tq, tq), 2 * tq, near_of_head)

    lam = _lambda_full(lq1, lk1, lq2, lk2, li, lam_init)
    g = g_ref[li:li + 1, :]
    for a in range(nh):
        acc = acc_sc[a]
        o = acc[:, :KD] / acc[:, KD:]
        o = o[:tq] - lam * o[tq:]
        o_ref[0, :, a * KD:(a + 1) * KD] = _sub_rms(o, g, lam_init).astype(o_ref.dtype)


def _prompt_attn(qkv, bt, lams, subln_g, *, li, n_heads, lam_init):
    bsz, l, _ = qkv.shape
    tq = bt.shape[2]
    nh = _pick(n_heads, (4, 2, 1))
    hb = n_heads // nh
    assert l % (2 * tq) == 0
    full = lambda a: pl.BlockSpec(a.shape, lambda b, hh, i: (0,) * a.ndim)
    body = functools.partial(_pattn_body, li=li, tq=tq, nh=nh, lam_init=lam_init)
    return pl.pallas_call(
        body, grid=(bsz, hb, l // tq),
        in_specs=[pl.BlockSpec((1, tq, nh * KD), lambda b, hh, i: (b, i, hh)),
                  pl.BlockSpec((1, l, nh * KD), lambda b, hh, i: (b, 0, hb + hh)),
                  pl.BlockSpec((1, l, nh * KD), lambda b, hh, i: (b, 0, 2 * hb + hh)),
                  pl.BlockSpec((nh, 2, tq, 2 * tq), lambda b, hh, i: (hh, 0, 0, 0))]
                 + [full(a) for a in lams] + [full(subln_g)],
        out_specs=pl.BlockSpec((1, tq, nh * KD), lambda b, hh, i: (b, i, hh)),
        out_shape=jax.ShapeDtypeStruct((bsz, l, n_heads * KD), BF16),
        scratch_shapes=[pltpu.VMEM((nh, l, 2 * KD), BF16), pltpu.VMEM((nh, 2 * tq, LANES), F32),
                        pltpu.VMEM((nh, 2 * tq, 2 * KD), F32)],
        compiler_params=_cparams(("parallel", "parallel", "arbitrary")))(
            qkv, qkv, qkv, bt, *lams, subln_g)


def _sattn_body(pt_ref, qt_ref, *refs, li, pp, n_heads, n_q, lam_init):
    krefs = refs[:pp]
    vrefs = refs[pp:2 * pp]
    kn_ref, vn_ref, bias_ref, lq1, lk1, lq2, lk2, g_ref, o_ref, m_sc, l_sc, acc_sc = refs[2 * pp:]
    jp = pl.program_id(1)
    last = pl.num_programs(1) - 1

    @pl.when(jp == 0)
    def _():
        m_sc[...] = jnp.full(m_sc.shape, NEG_INF, F32)
        l_sc[...] = jnp.zeros(l_sc.shape, F32)
        acc_sc[...] = jnp.zeros(acc_sc.shape, F32)

    def block(kbs, vbs, bias):
        s = jnp.stack([lax.dot_general(qt_ref[0, h], kbs[h], (((1,), (1,)), ((), ())),
                                       preferred_element_type=F32)
                       for h in range(n_heads)]) + bias
        m_prev = m_sc[...]
        m_new = jnp.maximum(m_prev, jnp.max(s, axis=-1, keepdims=True))
        alpha = jnp.exp(m_prev - m_new)
        p = jnp.exp(s - m_new)
        l_sc[...] = alpha * l_sc[...] + jnp.sum(p, axis=-1, keepdims=True)
        pb = p.astype(BF16)
        pv = jnp.stack([jnp.dot(pb[h], vbs[h], preferred_element_type=F32)
                        for h in range(n_heads)])
        acc_sc[...] = alpha * acc_sc[...] + pv
        m_sc[...] = m_new

    def head_rows(page_refs, h):
        rows_h = pl.ds(h, PAGE_SIZE, stride=n_heads)
        return jnp.concatenate([r[0, 0, rows_h, :] for r in page_refs], axis=0).astype(BF16)

    far = bias_ref[0]
    near = jnp.where(jp == last, bias_ref[1], far)
    block([head_rows(krefs, h) for h in range(n_heads)],
          [head_rows(vrefs, h) for h in range(n_heads)],
          jnp.concatenate([far] * (pp - 1) + [near], axis=-1))

    @pl.when(jp == last)
    def _():
        cols = [slice(h * KD, (h + 1) * KD) for h in range(n_heads)]
        block([kn_ref[0, :, c] for c in cols], [vn_ref[0, :, c] for c in cols], bias_ref[2])
        lam = _lambda_full(lq1, lk1, lq2, lk2, li, lam_init)
        g = g_ref[li:li + 1, :]
        o = acc_sc[...] / l_sc[...]
        o = _sub_rms(o[:, :n_q] - lam * o[:, n_q:], g, lam_init)
        for h in range(n_heads):
            o_ref[0, :, cols[h]] = o[h]


def _sample_attn(page_table, qt, cache_k, cache_v, k_new, v_new, bias3, lams, subln_g,
                 *, li, n_heads, n_q, lam_init):
    dbsz, n_pages = page_table.shape
    pp = _pick(n_pages, (PAGES_PER_STEP, 2, 1))
    w = n_heads * KD
    pt_flat = page_table.reshape(-1)

    def page_spec(g):
        return pl.BlockSpec((1, 1, PAGE_SIZE * n_heads, KD),
                            lambda b, j, pt, g=g: (li, pt[b * n_pages + j * pp + g], 0, 0))

    full = lambda a: pl.BlockSpec(a.shape, lambda b, j, pt: (0,) * a.ndim)
    per_b = pl.BlockSpec((1, LANES, w), lambda b, j, pt: (b, 0, 0))
    grid_spec = pltpu.PrefetchScalarGridSpec(
        num_scalar_prefetch=1, grid=(dbsz, n_pages // pp),
        in_specs=[pl.BlockSpec((1, n_heads, 2 * n_q, KD), lambda b, j, pt: (b, 0, 0, 0))]
                 + [page_spec(g) for g in range(pp)] * 2 + [per_b, per_b, full(bias3)]
                 + [full(a) for a in lams] + [full(subln_g)],
        out_specs=pl.BlockSpec((1, n_q, w), lambda b, j, pt: (b, 0, 0)),
        scratch_shapes=[pltpu.VMEM((n_heads, 2 * n_q, 1), F32),
                        pltpu.VMEM((n_heads, 2 * n_q, 1), F32),
                        pltpu.VMEM((n_heads, 2 * n_q, KD), F32)])
    body = functools.partial(_sattn_body, li=li, pp=pp, n_heads=n_heads, n_q=n_q,
                             lam_init=lam_init)
    return pl.pallas_call(
        body, grid_spec=grid_spec,
        out_shape=jax.ShapeDtypeStruct((dbsz, n_q, w), F32),
        compiler_params=_cparams(("parallel", "arbitrary")))(
            pt_flat, qt, *([cache_k] * pp), *([cache_v] * pp), k_new, v_new, bias3,
            *lams, subln_g)


def _block_queries(q, n_heads):
    dbsz, n_q, _ = q.shape
    qh = jnp.transpose(q.reshape(dbsz, n_q, n_heads, KD), (0, 2, 1, 3))
    col_map = (jnp.arange(KD) // DIFF_DH)[None, :] == jnp.arange(2)[:, None]
    t = qh[:, :, None] * col_map[None, None, :, None, :].astype(q.dtype)
    return t.reshape(dbsz, n_heads, 2 * n_q, KD)


def _sample_bias(t5_bias, n_heads, n_q):
    qi = jnp.arange(n_q, dtype=jnp.int32)
    key = jnp.arange(LANES, dtype=jnp.int32)

    def tile(rel):
        b = _bias_of_rel(t5_bias, rel)
        return jnp.concatenate([b, b], axis=1)

    far = tile(jnp.full((n_q, LANES), MAX_DISTANCE, jnp.int32))
    lastp = tile(PAGE_SIZE + qi[:, None] - key[None, :])
    new = tile(jnp.where(key[None, :] < n_q, qi[:, None] - key[None, :], -1))
    return jnp.stack([far, lastp, new])


def _ssd_body(xbc_ref, z_ref, dt_ref, h0_ref, dtb_ref, alog_ref, dx_ref, gn_ref,
              y_ref, hout_ref, hs, ysc, *, n_heads, valid_len):
    c = pl.program_id(1)
    t = SSD_CHUNK
    di = n_heads * SSD_P
    e_per_g = n_heads // SSD_G

    @pl.when(c == 0)
    def _():
        hs[...] = h0_ref[0]

    xbc = xbc_ref[0]
    xs = xbc[:, :di]
    row = lax.broadcasted_iota(jnp.int32, (t, LANES), 0)
    lane = lax.broadcasted_iota(jnp.int32, (t, LANES), 1)
    raw = dt_ref[0] + dtb_ref[...]
    dt = jnp.maximum(raw, 0.0) + jnp.log1p(jnp.exp(-jnp.abs(raw)))
    dt = jnp.where((row + c * t < valid_len) & (lane < n_heads), dt, 0.0)
    da = dt * (-jnp.exp(alog_ref[...]))
    tri = (lax.broadcasted_iota(jnp.int32, (t, t), 0)
           >= lax.broadcasted_iota(jnp.int32, (t, t), 1))
    cum = jnp.dot(tri.astype(F32), da, precision=lax.Precision.HIGHEST,
                  preferred_element_type=F32)
    cum_t = cum.T
    dt_t = dt.T
    cum_last = cum[t - 1:t, :]
    w_end = dt * jnp.exp(cum_last - cum)
    e_cum = jnp.exp(cum)
    e_last = jnp.exp(cum_last)
    xs_bf = xs.astype(BF16)
    xs_t = xs.T.astype(BF16)

    for g in range(SSD_G):
        b_g = xbc[:, di + g * SSD_N:di + (g + 1) * SSD_N]
        c_g = xbc[:, di + (SSD_G + g) * SSD_N:di + (SSD_G + g + 1) * SSD_N]
        cb = lax.dot_general(c_g.astype(BF16), b_g.astype(BF16), (((1,), (1,)), ((), ())),
                             preferred_element_type=F32)
        for e in range(e_per_g):
            h = g * e_per_g + e
            seg = cum[:, h:h + 1] - cum_t[h:h + 1, :]
            decay = jnp.where(tri, jnp.exp(jnp.where(tri, seg, 0.0)), 0.0)
            mat = (cb * decay * dt_t[h:h + 1, :]).astype(BF16)
            x_e = xs_bf[:, h * SSD_P:(h + 1) * SSD_P]
            y_d = jnp.dot(mat, x_e, preferred_element_type=F32)
            h_e = hs[h]
            c_w = (c_g * e_cum[:, h:h + 1]).astype(BF16)
            y_o = lax.dot_general(c_w, h_e.astype(BF16), (((1,), (1,)), ((), ())),
                                  preferred_element_type=F32)
            b_w = (b_g * w_end[:, h:h + 1]).astype(BF16)
            st = jnp.dot(xs_t[h * SSD_P:(h + 1) * SSD_P, :], b_w,
                         preferred_element_type=F32)
            hs[h] = h_e * e_last[:, h:h + 1] + st
            ysc[:, h * SSD_P:(h + 1) * SSD_P] = y_d + y_o

    y = ysc[...] + xs * dx_ref[...]
    z = z_ref[0]
    y = y * (z * jax.nn.sigmoid(z))
    gw = di // SSD_G
    for g in range(SSD_G):
        seg = y[:, g * gw:(g + 1) * gw]
        seg = seg * lax.rsqrt(jnp.mean(seg * seg, axis=-1, keepdims=True) + EPS)
        y_ref[0, :, g * gw:(g + 1) * gw] = (seg * gn_ref[:, g * gw:(g + 1) * gw]).astype(y_ref.dtype)

    @pl.when(c == pl.num_programs(1) - 1)
    def _():
        hout_ref[0] = hs[...]


def _ssd(xbc_act, zx, dt_raw, h0, dtb, alog, dx, gn, *, n_heads, valid_len):
    bsz, l, xbc_w = xbc_act.shape
    di = n_heads * SSD_P
    t = SSD_CHUNK
    assert l % t == 0
    full = lambda a: pl.BlockSpec(a.shape, lambda b, c: (0,) * a.ndim)
    body = functools.partial(_ssd_body, n_heads=n_heads, valid_len=valid_len)
    return pl.pallas_call(
        body, grid=(bsz, l // t),
        in_specs=[pl.BlockSpec((1, t, xbc_w), lambda b, c: (b, c, 0)),
                  pl.BlockSpec((1, t, di), lambda b, c: (b, c, 0)),
                  pl.BlockSpec((1, t, LANES), lambda b, c: (b, c, 0)),
                  pl.BlockSpec((1, n_heads, SSD_P, SSD_N), lambda b, c: (b, 0, 0, 0)),
                  full(dtb), full(alog), full(dx), full(gn)],
        out_specs=[pl.BlockSpec((1, t, di), lambda b, c: (b, c, 0)),
                   pl.BlockSpec((1, n_heads, SSD_P, SSD_N), lambda b, c: (b, 0, 0, 0))],
        out_shape=[jax.ShapeDtypeStruct((bsz, l, di), BF16),
                   jax.ShapeDtypeStruct((bsz, n_heads, SSD_P, SSD_N), F32)],
        scratch_shapes=[pltpu.VMEM((n_heads, SSD_P, SSD_N), F32), pltpu.VMEM((t, di), F32)],
        compiler_params=_cparams(("parallel", "arbitrary")))(
            xbc_act, zx, dt_raw, h0, dtb, alog, dx, gn)


def _epi_glu(accs, tiles, rows, j):
    a, b = accs
    return (a * jax.nn.sigmoid(b),)


def _epi_id(accs, tiles, rows, j):
    return (accs[0],)


def _epi_sigmoid(accs, tiles, rows, j):
    return (jax.nn.sigmoid(accs[0]),)


def _epi_merge(accs, tiles, rows, j):
    ya, yb, yc = accs
    ga, gb, gc = (t.astype(F32) for t in tiles)
    return (ga * ya + gb * yb + gc * yc,)


def _epi_resid(accs, tiles, rows, j):
    return (tiles[0] + rows[0] * accs[0],)


def _trunk(x, mod, prev_a, prev_c, prev_h, prev_f, wts, p, dims, *, paged=None):
    bsz, l, d = x.shape
    m = bsz * l
    depth, n_heads, conv_c, ssd_h, d_ff = (dims[k] for k in
                                           ("depth", "n_heads", "conv_c", "ssd_h", "d_ff"))
    att_w = n_heads * KD
    ssd_di = ssd_h * SSD_P
    ssd_xbc = ssd_di + 2 * SSD_G * SSD_N
    conv_k = p["conv_a_w"].shape[1]
    ssd_k = p["ssd_conv_w"].shape[1]
    ffn_k = p["ffn_conv_w"].shape[1]
    prompt = paged is None
    tm = _pick(l, (512, 256, 128)) if prompt else m
    tpb = l // tm if prompt else 1
    rpb = l if prompt else None
    t_seq = _pick(l, (128,)) if prompt else l
    lams = (p["lambda_q1"], p["lambda_k1"], p["lambda_q2"], p["lambda_k2"])
    scale = DIFF_DH ** -0.5

    xf = x.reshape(m, d)
    out_a, out_c, out_h, out_f = [], [], [], []
    kst = jnp.zeros((depth, m, att_w), F32)
    vst = jnp.zeros((depth, m, att_w), F32)
    for li in range(depth):
        w = wts
        md = mod[li]
        lam_init = 0.8 - 0.6 * math.exp(-0.3 * li)
        h = _norm_mod(xf, p["norm1_g"][li][None], md, 1, 0, tm=tm, tpb=tpb)

        c_qkv = 2 * conv_c
        c_zx = c_qkv + 3 * att_w
        zx_w = ssd_di + ssd_xbc
        (u,) = _mm([h], [(w["in"], li, 0, 0), (w["in"], li, conv_c, 0)], _epi_glu, [F32], conv_c,
                   rows_per_batch=rpb)
        qkv_bf, kst, vst = _qkv(h, w["in"], c_qkv, kst, vst, li=li, scale=scale,
                                rows_per_batch=rpb)
        (zx,) = _mm([h], [(w["in"], li, c_zx, 0)], _epi_id, [F32], zx_w, rows_per_batch=rpb)
        (dt_raw,) = _mm([h], [(w["dt"], li, 0, 0)], _epi_id, [F32], LANES, rows_per_batch=rpb)
        (gates,) = _mm([h], [(w["gates"], li, 0, 0)], _epi_sigmoid, [BF16], 3 * d,
                       rows_per_batch=rpb)

        u3 = u.reshape(bsz, l, conv_c)
        out_a.append(_conv_state(prev_a[li], u3, conv_k - 1))
        a_act = _dwconv(u3, 0, _pad_prev(prev_a[li], 32), p["conv_a_w"][li],
                        p["conv_a_b"][li][None], _epi_conv_a, BF16, c=conv_c, t=t_seq,
                        tc=conv_c, halo=32,
                        prows=(p["ln_a_g"][li][None], p["ln_a_b"][li][None]))

        if prompt:
            o = _prompt_attn(qkv_bf.reshape(bsz, l, 3 * att_w), dims["bias_tiles"], lams,
                             p["subln_g"], li=li, n_heads=n_heads, lam_init=lam_init)
        else:
            qb = qkv_bf.reshape(bsz, l, 3 * att_w)
            qt = _block_queries(qb[..., :att_w], n_heads)
            pad = ((0, 0), (0, LANES - l), (0, 0))
            k_new = jnp.pad(qb[..., att_w:2 * att_w], pad)
            v_new = jnp.pad(qb[..., 2 * att_w:], pad)
            o = _sample_attn(paged["page_table"], qt, paged["cache_k"], paged["cache_v"],
                             k_new, v_new, dims["bias3"], lams, p["subln_g"], li=li,
                             n_heads=n_heads, n_q=l, lam_init=lam_init).astype(BF16)
        o = o.reshape(m, att_w)

        zx3 = zx.reshape(bsz, l, zx_w)
        dt3 = dt_raw.reshape(bsz, l, LANES)
        xbc_raw = zx3[..., ssd_di:]
        out_c.append(_conv_state(prev_c[li], xbc_raw, ssd_k - 1))
        tc = _pick(math.gcd(ssd_di, ssd_xbc), (512, 256, 128))
        xbc_act = _dwconv(zx3, ssd_di // tc, _pad_prev(prev_c[li], 8), p["ssd_conv_w"][li],
                          p["ssd_conv_b"][li][None], _epi_silu, F32, c=ssd_xbc,
                          t=_pick(l, (512, 256, 128)) if prompt else l, tc=tc, halo=8)
        if prompt:
            xbc_in, zx_in, dt_in = xbc_act, zx3, dt3
        else:
            pad = ((0, 0), (0, SSD_CHUNK - l), (0, 0))
            xbc_in, zx_in, dt_in = jnp.pad(xbc_act, pad), jnp.pad(zx3, pad), jnp.pad(dt3, pad)
        lane_pad = lambda a: jnp.pad(a, (0, LANES - a.shape[0]))[None]
        y_c, h_t = _ssd(xbc_in, zx_in, dt_in, prev_h[li],
                        lane_pad(p["dt_bias"][li]), lane_pad(p["a_log"][li]),
                        jnp.repeat(p["d_skip"][li], SSD_P)[None], p["ssd_norm_g"][li][None],
                        n_heads=ssd_h, valid_len=l)
        out_h.append(h_t)
        y_c = y_c[:, :l].reshape(m, ssd_di)

        (merged,) = _mm([a_act.reshape(m, conv_c), o, y_c],
                        [(w["proj_a"], li, 0, 0), (w["proj_b"], li, 0, 1), (w["proj_c"], li, 0, 2)],
                        _epi_merge, [BF16], d, tiles=[(gates, 0), (gates, d), (gates, 2 * d)],
                        rows_per_batch=rpb)
        (xf,) = _mm([merged], [(w["out"], li, 0, 0)], _epi_resid, [F32], d, tiles=[(xf, 0)],
                    rows=[(md, 2 * d)], rows_per_batch=rpb)

        h2 = _norm_mod(xf, p["norm2_g"][li][None], md, 4, 3, tm=tm, tpb=tpb)
        if prompt:
            act = _upconv(h2, w["up"], li, _pad_prev(prev_f[li], UP_HALO), p["ffn_conv_w"][li],
                          p["ffn_conv_b"][li][None], d_ff=d_ff, rows_per_batch=rpb)
            assert l >= UP_HALO
            h2_tail = h2.reshape(bsz, l, d)[:, l - UP_HALO:].reshape(bsz * UP_HALO, d)
            (fa_tail,) = _mm([h2_tail], [(w["up"], li, 0, 0)], _epi_id, [F32], d_ff)
            out_f.append(fa_tail.reshape(bsz, UP_HALO, d_ff)[:, UP_HALO - (ffn_k - 1):])
        else:
            (up,) = _mm([h2], [(w["up"], li, 0, 0)], _epi_id, [F32], 2 * d_ff)
            up3 = up.reshape(bsz, l, 2 * d_ff)
            out_f.append(_conv_state(prev_f[li], up3[..., :d_ff], ffn_k - 1))
            tc = _pick(d_ff, (512, 256, 128))
            act = _dwconv(up3, 0, _pad_prev(prev_f[li], 8), p["ffn_conv_w"][li],
                          p["ffn_conv_b"][li][None], _epi_ffn, BF16, c=d_ff, t=t_seq, tc=tc,
                          halo=8, xtiles=[(up3, d_ff // tc)])
        (xf,) = _mm([act.reshape(m, d_ff)], [(w["down"], li, 0, 0)], _epi_resid, [F32], d,
                    tiles=[(xf, 0)], rows=[(md, 5 * d)], rows_per_batch=rpb)

    y_out = _rms(xf, p["final_g"][None], tm=tm).reshape(bsz, l, d)
    kv_shape = (depth, bsz, l, n_heads, KD)
    return (y_out, kst.reshape(kv_shape), vst.reshape(kv_shape), jnp.stack(out_a), jnp.stack(out_c),
            jnp.stack(out_h), jnp.stack(out_f))


def _bf16_weights(p, dims):
    conv_c, att_w = dims["conv_c"], dims["n_heads"] * KD
    ssd_di = dims["ssd_h"] * SSD_P
    ssd_xbc = ssd_di + 2 * SSD_G * SSD_N
    c_dt = 2 * conv_c + 3 * att_w + ssd_di + ssd_xbc
    c_gates = c_dt + dims["ssd_h"]
    w_in = p["w_in"].astype(BF16)
    dt_cols = jnp.pad(w_in[:, :, c_dt:c_gates], ((0, 0), (0, 0), (0, LANES - dims["ssd_h"])))
    bf = lambda a: a.astype(BF16)
    return {"in": w_in, "dt": dt_cols, "gates": w_in[:, :, c_gates:],
            "proj_a": bf(p["w_proj_a"]), "proj_b": bf(p["w_proj_b"]), "proj_c": bf(p["w_proj_c"]),
            "out": bf(p["w_out"]), "up": bf(p["ffn_up"]), "down": bf(p["ffn_down"])}


def kernel(x_prompt, x_sample, cache_k, cache_v, page_table, state_conv_a, state_ssd_conv,
           state_ssm, state_ffn_conv, c_prompt, c_sample, w_ada, b_ada, norm1_g, w_in,
           conv_a_w, conv_a_b, ln_a_g, ln_a_b, w_proj_a, t5_bias, lambda_q1, lambda_k1,
           lambda_q2, lambda_k2, subln_g, w_proj_b, ssd_conv_w, ssd_conv_b, dt_bias, a_log,
           d_skip, ssd_norm_g, w_proj_c, w_out, norm2_g, ffn_up, ffn_conv_w, ffn_conv_b,
           ffn_down, final_g):
    p = dict(w_ada=w_ada, b_ada=b_ada, norm1_g=norm1_g, w_in=w_in, conv_a_w=conv_a_w,
             conv_a_b=conv_a_b, ln_a_g=ln_a_g, ln_a_b=ln_a_b, w_proj_a=w_proj_a, t5_bias=t5_bias,
             lambda_q1=lambda_q1, lambda_k1=lambda_k1, lambda_q2=lambda_q2, lambda_k2=lambda_k2,
             subln_g=subln_g, w_proj_b=w_proj_b, ssd_conv_w=ssd_conv_w, ssd_conv_b=ssd_conv_b,
             dt_bias=dt_bias, a_log=a_log, d_skip=d_skip, ssd_norm_g=ssd_norm_g, w_proj_c=w_proj_c,
             w_out=w_out, norm2_g=norm2_g, ffn_up=ffn_up, ffn_conv_w=ffn_conv_w,
             ffn_conv_b=ffn_conv_b, ffn_down=ffn_down, final_g=final_g)
    bp, lp, d = x_prompt.shape
    bs, ls, _ = x_sample.shape
    depth = w_ada.shape[0]
    n_heads = cache_k.shape[3]
    n_pages = page_table.shape[1]
    dims = dict(depth=depth, n_heads=n_heads, conv_c=conv_a_w.shape[2], ssd_h=dt_bias.shape[1],
                d_ff=ffn_conv_w.shape[2])
    f32 = x_prompt.dtype

    tq = _pick(lp, (ATTN_TQ, 128))
    assert tq >= MAX_DISTANCE
    qi = jnp.arange(tq, dtype=jnp.int32)
    rel0 = qi[:, None] - qi[None, :]
    b_diag, b_prev = _bias_of_rel(t5_bias, rel0), _bias_of_rel(t5_bias, rel0 + tq)
    dims["bias_tiles"] = jnp.stack(
        [jnp.concatenate([b_prev, b_diag], axis=-1),
         jnp.concatenate([b_diag, jnp.full_like(b_diag, NEG_INF)], axis=-1)], axis=1)
    dims["bias3"] = _sample_bias(t5_bias, n_heads, ls)

    rows = -(-(bp + bs) // 8) * 8
    c_all = jnp.pad(jnp.concatenate([c_prompt, c_sample], axis=0), ((0, rows - bp - bs), (0, 0)))
    mod = _ada(c_all, w_ada, b_ada)
    mod_p = mod[:, :bp, None, :]
    mod_s = jnp.repeat(mod[:, bp:bp + bs], ls, axis=1)[:, None]

    wts = _bf16_weights(p, dims)
    zero = lambda *s: jnp.zeros((depth, bp) + s, f32)
    res_p = _trunk(x_prompt, mod_p, zero(conv_a_w.shape[1] - 1, dims["conv_c"]),
                   zero(ssd_conv_w.shape[1] - 1, ssd_conv_w.shape[2]),
                   zero(dims["ssd_h"], SSD_P, SSD_N), zero(ffn_conv_w.shape[1] - 1, dims["d_ff"]),
                   wts, p, dims)
    paged = dict(page_table=page_table,
                 cache_k=cache_k.reshape(depth, -1, PAGE_SIZE * n_heads, KD),
                 cache_v=cache_v.reshape(depth, -1, PAGE_SIZE * n_heads, KD))
    res_s = _trunk(x_sample, mod_s, state_conv_a, state_ssd_conv, state_ssm, state_ffn_conv,
                   wts, p, dims, paged=paged)
    y_p, k_p, v_p, a_p, c_p, h_p, f_p = res_p
    y_s, k_s, v_s, a_s, c_s, h_s, f_s = res_s
    return (y_p, y_s, k_p, v_p, k_s, v_s, a_p, a_s, c_p, c_s, h_p, h_s, f_p, f_s)
```

```python
import functools
import math

import jax
import jax.numpy as jnp
from jax import lax
from jax.experimental import pallas as pl
from jax.experimental.pallas import tpu as pltpu

F32 = jnp.float32
BF16 = jnp.bfloat16

EPS = 1e-6
LN_EPS = 1e-5
NEG_INF = -1e30
DIFF_DH = 64
KD = 2 * DIFF_DH
NUM_BUCKETS = 32
MAX_DISTANCE = 128
SSD_P = 64
SSD_N = 128
SSD_G = 2
SSD_CHUNK = 128
PAGE_SIZE = 128
LANES = 128
VMEM_LIMIT = 56 * 1024 * 1024
VMEM_BLOCK_BUDGET = 46 * 1024 * 1024
MIN_TN = 512
MM_ROW_CHUNK = 512

ATTN_TQ = 256
PAGES_PER_STEP = 8


def _cparams(sem):
    return pltpu.CompilerParams(dimension_semantics=sem, vmem_limit_bytes=VMEM_LIMIT)


def _pick(n, cands):
    for c in cands:
        if n % c == 0:
            return c
    return n


def _fit_tiles(m_cands, n_cands, block_bytes):
    wide = [c for c in n_cands if c >= MIN_TN] or list(n_cands)
    for cols in (wide, list(n_cands)):
        for tm in m_cands:
            for tn in cols:
                if block_bytes(tm, tn) <= VMEM_BLOCK_BUDGET:
                    return tm, tn
    return m_cands[-1], n_cands[-1]


def _row_cands(rows_per_batch, whole=None):
    if whole is not None:
        return [whole]
    return [c for c in (2048, 1024, 512, 256, 128) if rows_per_batch % c == 0] or [rows_per_batch]


def _col_cands(n, offsets=()):
    cands = [c for c in (1024, 768, 512, 256, 896, 640, 384, 128)
             if n % c == 0 and all(o % c == 0 for o in offsets)]
    return cands or [n]


def _mm_body(*refs, nl, lhs_of, nt, nr, epi, rm):
    nw = len(lhs_of)
    lrefs = refs[:nl]
    wrefs = refs[nl:nl + nw]
    trefs = refs[nl + nw:nl + nw + nt]
    rrefs = refs[nl + nw + nt:nl + nw + nt + nr]
    orefs = refs[nl + nw + nt + nr:]
    tm = orefs[0].shape[0]
    j = pl.program_id(1)
    for m0 in range(0, tm, rm):
        rs = slice(m0, m0 + rm)
        accs = [jnp.dot(lrefs[li][rs, :], w[...], preferred_element_type=F32)
                for w, li in zip(wrefs, lhs_of)]
        rows = [r[0] if r.shape[1] == 1 else r[0, rs, :] for r in rrefs]
        res = epi(accs, [t[rs, :] for t in trefs], rows, j)
        for o, r in zip(orefs, res):
            o[rs, :] = r.astype(o.dtype)


def _mm(lhs, rhs, epi, out_dtypes, n, *, tiles=(), rows=(), rows_per_batch=None):
    m = lhs[0].shape[0]
    offsets = [off for _, _, off, _ in rhs] + [off for _, off in tiles] + [off for _, off in rows]

    def block_bytes(tm, tn):
        b = sum(tm * a.shape[1] * a.dtype.itemsize for a in lhs)
        b += sum(a.shape[1] * tn * a.dtype.itemsize for a, _, _, _ in rhs)
        b += sum(tm * tn * a.dtype.itemsize for a, _ in tiles)
        b += sum(tm * tn * jnp.dtype(d).itemsize for d in out_dtypes)
        return 2 * b + len(rhs) * min(tm, MM_ROW_CHUNK) * tn * 4

    tm, tn = _fit_tiles(_row_cands(rows_per_batch, None if rows_per_batch else m),
                        _col_cands(n, offsets), block_bytes)
    tpb = rows_per_batch // tm if rows_per_batch else 1
    assert m % tm == 0 and n % tn == 0
    in_specs, args = [], []
    for a in lhs:
        in_specs.append(pl.BlockSpec((tm, a.shape[1]), lambda i, j: (i, 0)))
        args.append(a)
    for a, li, off, _ in rhs:
        in_specs.append(pl.BlockSpec((None, a.shape[1], tn),
                                     lambda i, j, li=li, o=off // tn: (li, 0, j + o)))
        args.append(a)
    for a, off in tiles:
        in_specs.append(pl.BlockSpec((tm, tn), lambda i, j, o=off // tn: (i, j + o)))
        args.append(a)
    for a, off in rows:
        assert a.shape[1] == 1 or (a.shape[0] == 1 and a.shape[1] == tm == m)
        in_specs.append(pl.BlockSpec((1, a.shape[1], tn),
                                     lambda i, j, o=off // tn: (i // tpb, 0, j + o)))
        args.append(a)
    out_specs = [pl.BlockSpec((tm, tn), lambda i, j: (i, j)) for _ in out_dtypes]
    out_shape = [jax.ShapeDtypeStruct((m, n), d) for d in out_dtypes]
    body = functools.partial(_mm_body, nl=len(lhs), lhs_of=tuple(k for _, _, _, k in rhs),
                             nt=len(tiles), nr=len(rows), epi=epi, rm=min(tm, MM_ROW_CHUNK))
    return pl.pallas_call(
        body, grid=(m // tm, n // tn), in_specs=in_specs, out_specs=out_specs,
        out_shape=out_shape, compiler_params=_cparams(("parallel", "arbitrary")))(*args)


def _qkv_body(h_ref, w_ref, kin_ref, vin_ref, qkv_ref, kst_ref, vst_ref, *, nq, scale):
    del kin_ref, vin_ref
    j = pl.program_id(1)
    acc = jnp.dot(h_ref[...], w_ref[...], preferred_element_type=F32)
    qkv_ref[...] = (acc * jnp.where(j < nq, scale, 1.0)).astype(qkv_ref.dtype)

    @pl.when((j >= nq) & (j < 2 * nq))
    def _():
        kst_ref[0] = acc

    @pl.when(j >= 2 * nq)
    def _():
        vst_ref[0] = acc


def _qkv(h, w, col0, kst, vst, *, li, scale, rows_per_batch=None):
    m, d = h.shape
    wdt = kst.shape[2]
    block_bytes = lambda tm, tn: 2 * (tm * d * 2 + d * tn * 2 + tm * tn * (2 + 4 + 4)) + tm * tn * 4
    tm, tn = _fit_tiles(_row_cands(rows_per_batch, None if rows_per_batch else m),
                        _col_cands(wdt, (col0,)), block_bytes)
    nq = wdt // tn
    j0 = col0 // tn
    assert m % tm == 0 and wdt % tn == 0 and col0 % tn == 0
    k_idx = lambda i, j: (li, i, jnp.clip(j - nq, 0, nq - 1))
    v_idx = lambda i, j: (li, i, jnp.clip(j - 2 * nq, 0, nq - 1))
    body = functools.partial(_qkv_body, nq=nq, scale=scale)
    return pl.pallas_call(
        body, grid=(m // tm, 3 * nq),
        in_specs=[pl.BlockSpec((tm, d), lambda i, j: (i, 0)),
                  pl.BlockSpec((None, d, tn), lambda i, j: (li, 0, j0 + j)),
                  pl.BlockSpec(memory_space=pl.ANY), pl.BlockSpec(memory_space=pl.ANY)],
        out_specs=[pl.BlockSpec((tm, tn), lambda i, j: (i, j)),
                   pl.BlockSpec((1, tm, tn), k_idx), pl.BlockSpec((1, tm, tn), v_idx)],
        out_shape=[jax.ShapeDtypeStruct((m, 3 * wdt), BF16),
                   jax.ShapeDtypeStruct(kst.shape, F32), jax.ShapeDtypeStruct(vst.shape, F32)],
        input_output_aliases={2: 1, 3: 2},
        compiler_params=_cparams(("arbitrary", "arbitrary")))(h, w, kst, vst)


def _ada_body(c_ref, w_ref, b_ref, o_ref):
    c = c_ref[...]
    s = (c * jax.nn.sigmoid(c)).astype(BF16)
    o_ref[0] = jnp.dot(s, w_ref[0].astype(BF16), preferred_element_type=F32) + b_ref[0]


def _ada(c_all, w_ada, b_ada):
    depth, d, n6 = w_ada.shape
    rows = c_all.shape[0]
    tn = _pick(n6, (1024, 512, 256, 128))
    return pl.pallas_call(
        _ada_body, grid=(depth, n6 // tn),
        in_specs=[pl.BlockSpec((rows, d), lambda l, j: (0, 0)),
                  pl.BlockSpec((1, d, tn), lambda l, j: (l, 0, j)),
                  pl.BlockSpec((1, 1, tn), lambda l, j: (l, 0, j))],
        out_specs=pl.BlockSpec((1, rows, tn), lambda l, j: (l, 0, j)),
        out_shape=jax.ShapeDtypeStruct((depth, rows, n6), F32),
        compiler_params=_cparams(("arbitrary", "arbitrary")))(c_all, w_ada, b_ada[:, None, :])


NORM_ROW_CHUNK = 32


def _norm_mod_body(x_ref, g_ref, sc_ref, sh_ref, o_ref):
    tm = x_ref.shape[0]
    rc = min(tm, NORM_ROW_CHUNK)
    g = g_ref[...]
    for r0 in range(0, tm, rc):
        rs = slice(r0, r0 + rc)
        x = x_ref[rs, :]
        y = x * lax.rsqrt(jnp.mean(x * x, axis=-1, keepdims=True) + EPS) * g
        sc, sh = (r[0] if r.shape[1] == 1 else r[0, rs, :] for r in (sc_ref, sh_ref))
        o_ref[rs, :] = (y * (1.0 + sc) + sh).astype(o_ref.dtype)


def _norm_mod(x, g, mod, sc_blk, sh_blk, *, tm, tpb):
    m, d = x.shape
    r = mod.shape[1]
    return pl.pallas_call(
        _norm_mod_body, grid=(m // tm,),
        in_specs=[pl.BlockSpec((tm, d), lambda i: (i, 0)),
                  pl.BlockSpec((1, d), lambda i: (0, 0)),
                  pl.BlockSpec((1, r, d), lambda i: (i // tpb, 0, sc_blk)),
                  pl.BlockSpec((1, r, d), lambda i: (i // tpb, 0, sh_blk))],
        out_specs=pl.BlockSpec((tm, d), lambda i: (i, 0)),
        out_shape=jax.ShapeDtypeStruct((m, d), BF16),
        compiler_params=_cparams(("parallel",)))(x, g, mod, mod)


def _rms_body(x_ref, g_ref, o_ref):
    tm = x_ref.shape[0]
    rc = min(tm, NORM_ROW_CHUNK)
    g = g_ref[...]
    for r0 in range(0, tm, rc):
        x = x_ref[r0:r0 + rc, :]
        o_ref[r0:r0 + rc, :] = x * lax.rsqrt(jnp.mean(x * x, axis=-1, keepdims=True) + EPS) * g


def _rms(x, g, *, tm):
    m, d = x.shape
    return pl.pallas_call(
        _rms_body, grid=(m // tm,),
        in_specs=[pl.BlockSpec((tm, d), lambda i: (i, 0)),
                  pl.BlockSpec((1, d), lambda i: (0, 0))],
        out_specs=pl.BlockSpec((tm, d), lambda i: (i, 0)),
        out_shape=jax.ShapeDtypeStruct((m, d), F32),
        compiler_params=_cparams(("parallel",)))(x, g)


def _dwconv_body(*refs, kc, halo, t, tc, n_seq_tiles, nx, npr, epi):
    u_ref, prev_ref, w_ref, b_ref = refs[:4]
    xrefs = refs[4:4 + nx]
    prefs = refs[4 + nx:4 + nx + npr]
    o_ref = refs[4 + nx + npr]
    win, cv = refs[4 + nx + npr + 1:]

    @pl.when(pl.program_id(2) == 0)
    def _():
        win[0:halo, :] = prev_ref[0]

    win[halo:halo + t, :] = u_ref[0]
    base = halo - (kc - 1)
    rc = min(t, 64)
    cc = min(tc, 256)
    for r0 in range(0, t, rc):
        for c0 in range(0, tc, cc):
            acc = jnp.broadcast_to(b_ref[:, c0:c0 + cc], (rc, cc))
            wlen = rc + halo
            window = win[r0:r0 + wlen, c0:c0 + cc]
            for r in range(8):
                taps = [j for j in range(kc) if (base + j) % 8 == r]
                if not taps:
                    continue
                sh = window if r == 0 else pltpu.roll(window, wlen - r, 0)
                for j in taps:
                    a = (base + j) // 8
                    acc = acc + w_ref[j:j + 1, c0:c0 + cc] * sh[8 * a:8 * a + rc]
            cv[r0:r0 + rc, c0:c0 + cc] = acc
    o_ref[0] = epi(cv[...], [x[0] for x in xrefs], [p[...] for p in prefs]).astype(o_ref.dtype)
    if n_seq_tiles > 1:
        win[0:halo, :] = win[t:t + halo, :]


def _dwconv(u, u_blk, prev, w, b, epi, out_dtype, *, c, t, tc, halo, xtiles=(), prows=()):
    bsz, l, _ = u.shape
    kc = w.shape[0]
    assert l % t == 0 and c % tc == 0 and halo >= kc - 1 and (t >= halo or l == t)
    nlt = l // t
    in_specs = [pl.BlockSpec((1, t, tc), lambda bi, ci, li: (bi, li, ci + u_blk)),
                pl.BlockSpec((1, halo, tc), lambda bi, ci, li: (bi, 0, ci)),
                pl.BlockSpec((kc, tc), lambda bi, ci, li: (0, ci)),
                pl.BlockSpec((1, tc), lambda bi, ci, li: (0, ci))]
    args = [u, prev, w, b]
    for a, off in xtiles:
        in_specs.append(pl.BlockSpec((1, t, tc), lambda bi, ci, li, off=off: (bi, li, ci + off)))
        args.append(a)
    for a in prows:
        in_specs.append(pl.BlockSpec((1, tc), lambda bi, ci, li: (0, ci)))
        args.append(a)
    body = functools.partial(_dwconv_body, kc=kc, halo=halo, t=t, tc=tc, n_seq_tiles=nlt,
                             nx=len(xtiles), npr=len(prows), epi=epi)
    return pl.pallas_call(
        body, grid=(bsz, c // tc, nlt), in_specs=in_specs,
        out_specs=pl.BlockSpec((1, t, tc), lambda bi, ci, li: (bi, li, ci)),
        out_shape=jax.ShapeDtypeStruct((bsz, l, c), out_dtype),
        scratch_shapes=[pltpu.VMEM((halo + t, tc), F32), pltpu.VMEM((t, tc), F32)],
        compiler_params=_cparams(("parallel", "parallel", "arbitrary")))(*args)


def _epi_conv_a(cv, xs, ps):
    g, b = ps
    mu = jnp.mean(cv, axis=-1, keepdims=True)
    d = cv - mu
    var = jnp.mean(d * d, axis=-1, keepdims=True)
    y = d * lax.rsqrt(var + LN_EPS) * g + b
    return y * jax.nn.sigmoid(y)


def _epi_silu(cv, xs, ps):
    return cv * jax.nn.sigmoid(cv)


def _epi_ffn(cv, xs, ps):
    return jax.nn.gelu(cv) * xs[0]


UP_HALO = 8
UP_MAX_TM = 1024


def _upconv_body(h_ref, wa_ref, wb_ref, prev_ref, cw_ref, cb_ref, act_ref,
                 win, carry, *, tm, tn, tpb, kc):
    i = pl.program_id(0)
    j = pl.program_id(1)
    first = i % tpb == 0

    @pl.when(first)
    def _():
        win[0:UP_HALO, :] = prev_ref[0]

    @pl.when(jnp.logical_not(first))
    def _():
        win[0:UP_HALO, :] = carry[j]

    base = UP_HALO - (kc - 1)
    rm = min(tm, 512)
    rc = min(rm, 64)
    cc = min(tn, 256)
    for m0 in range(0, tm, rm):
        h = h_ref[m0:m0 + rm, :]
        fa = jnp.dot(h, wa_ref[...], preferred_element_type=F32)
        fb = jnp.dot(h, wb_ref[...], preferred_element_type=F32)
        win[UP_HALO + m0:UP_HALO + m0 + rm, :] = fa
        for r0 in range(0, rm, rc):
            for c0 in range(0, tn, cc):
                acc = jnp.broadcast_to(cb_ref[:, c0:c0 + cc], (rc, cc))
                for t in range(kc):
                    start = base + m0 + r0 + t
                    acc = acc + cw_ref[t:t + 1, c0:c0 + cc] * win[start:start + rc, c0:c0 + cc]
                act_ref[m0 + r0:m0 + r0 + rc, c0:c0 + cc] = (
                    jax.nn.gelu(acc) * fb[r0:r0 + rc, c0:c0 + cc]).astype(act_ref.dtype)
    carry[j] = win[tm:tm + UP_HALO, :]


def _upconv(h, w_up, li, prev, cw, cb, *, d_ff, rows_per_batch):
    m, d = h.shape
    kc = cw.shape[0]
    block_bytes = lambda tm, tn: (2 * (tm * d * 2 + 2 * d * tn * 2 + tm * tn * 2)
                                  + (tm + UP_HALO) * tn * 4 + 2 * min(tm, 512) * tn * 4)
    rows = [c for c in _row_cands(rows_per_batch) if c <= UP_MAX_TM] or [rows_per_batch]
    tm, tn = _fit_tiles(rows, _col_cands(d_ff), block_bytes)
    tpb = rows_per_batch // tm
    nj = d_ff // tn
    assert m % tm == 0 and d_ff % tn == 0 and kc - 1 <= UP_HALO <= tm
    body = functools.partial(_upconv_body, tm=tm, tn=tn, tpb=tpb, kc=kc)
    return pl.pallas_call(
        body, grid=(m // tm, nj),
        in_specs=[pl.BlockSpec((tm, d), lambda i, j: (i, 0)),
                  pl.BlockSpec((None, d, tn), lambda i, j: (li, 0, j)),
                  pl.BlockSpec((None, d, tn), lambda i, j: (li, 0, j + nj)),
                  pl.BlockSpec((1, UP_HALO, tn), lambda i, j: (i // tpb, 0, j)),
                  pl.BlockSpec((kc, tn), lambda i, j: (0, j)),
                  pl.BlockSpec((1, tn), lambda i, j: (0, j))],
        out_specs=pl.BlockSpec((tm, tn), lambda i, j: (i, j)),
        out_shape=jax.ShapeDtypeStruct((m, d_ff), BF16),
        scratch_shapes=[pltpu.VMEM((UP_HALO + tm, tn), F32), pltpu.VMEM((nj, UP_HALO, tn), F32)],
        compiler_params=_cparams(("arbitrary", "arbitrary")))(h, w_up, w_up, prev, cw, cb)


GLU_HALO = 32
GLU_TM = 512


def _gluconv_body(h_ref, wa_ref, wb_ref, prev_ref, cw_ref, cb_ref, g_ref, b_ref, o_ref,
                  win, cv, *, tm, c, tpb, kc):
    halo = GLU_HALO

    @pl.when(pl.program_id(0) % tpb == 0)
    def _():
        win[0:halo, :] = prev_ref[0]

    base = halo - (kc - 1)
    rm = min(tm, 256)
    rc = min(rm, 64)
    cc = min(c, LANES)
    for m0 in range(0, tm, rm):
        h = h_ref[m0:m0 + rm, :]
        a = jnp.dot(h, wa_ref[...], preferred_element_type=F32)
        b = jnp.dot(h, wb_ref[...], preferred_element_type=F32)
        win[halo + m0:halo + m0 + rm, :] = a * jax.nn.sigmoid(b)
        for r0 in range(m0, m0 + rm, rc):
            for c0 in range(0, c, cc):
                acc = jnp.broadcast_to(cb_ref[:, c0:c0 + cc], (rc, cc))
                wlen = rc + halo
                window = win[r0:r0 + wlen, c0:c0 + cc]
                for r in range(8):
                    taps = [j for j in range(kc) if (base + j) % 8 == r]
                    if not taps:
                        continue
                    sh = window if r == 0 else pltpu.roll(window, wlen - r, 0)
                    for j in taps:
                        t8 = (base + j) // 8 * 8
                        acc = acc + cw_ref[j:j + 1, c0:c0 + cc] * sh[t8:t8 + rc]
                cv[r0:r0 + rc, c0:c0 + cc] = acc
        y = _epi_conv_a(cv[m0:m0 + rm, :], (), (g_ref[...], b_ref[...]))
        o_ref[m0:m0 + rm, :] = y.astype(o_ref.dtype)
    win[0:halo, :] = win[tm:tm + halo, :]


def _gluconv(h, w_in, li, prev, cw, cb, ln_g, ln_b, *, c, rows_per_batch):
    m, d = h.shape
    kc = cw.shape[0]
    tm = _pick(rows_per_batch, (GLU_TM, 256, 128))
    tpb = rows_per_batch // tm
    assert m % tm == 0 and kc - 1 <= GLU_HALO <= tm and c % LANES == 0
    full = lambda a: pl.BlockSpec(a.shape, lambda i: (0,) * a.ndim)
    body = functools.partial(_gluconv_body, tm=tm, c=c, tpb=tpb, kc=kc)
    return pl.pallas_call(
        body, grid=(m // tm,),
        in_specs=[pl.BlockSpec((tm, d), lambda i: (i, 0)),
                  pl.BlockSpec((None, d, c), lambda i: (li, 0, 0)),
                  pl.BlockSpec((None, d, c), lambda i: (li, 0, 1)),
                  pl.BlockSpec((1, GLU_HALO, c), lambda i: (i // tpb, 0, 0)),
                  full(cw), full(cb), full(ln_g), full(ln_b)],
        out_specs=pl.BlockSpec((tm, c), lambda i: (i, 0)),
        out_shape=jax.ShapeDtypeStruct((m, c), BF16),
        scratch_shapes=[pltpu.VMEM((GLU_HALO + tm, c), F32), pltpu.VMEM((tm, c), F32)],
        compiler_params=_cparams(("arbitrary",)))(h, w_in, w_in, prev, cw, cb, ln_g, ln_b)


def _conv_state(prev, u, k1):
    l = u.shape[1]
    if l >= k1:
        return u[:, l - k1:]
    return jnp.concatenate([prev[:, l:], u], axis=1)


def _pad_prev(prev, halo):
    return jnp.pad(prev, ((0, 0), (halo - prev.shape[1], 0), (0, 0)))


def _t5_bucket(rel):
    n = jnp.maximum(rel, 0)
    max_exact = NUM_BUCKETS // 2
    nf = jnp.maximum(n, 1).astype(F32)
    large = max_exact + (jnp.log(nf / max_exact) / math.log(MAX_DISTANCE / max_exact)
                         * (NUM_BUCKETS - max_exact)).astype(jnp.int32)
    large = jnp.minimum(large, NUM_BUCKETS - 1)
    return jnp.where(n < max_exact, n, large)


def _bias_of_rel(t5_bias, rel):
    bucket = _t5_bucket(rel)
    table = t5_bias.astype(F32)
    expand = (slice(None),) + (None,) * rel.ndim
    bias = jnp.zeros((table.shape[1],) + rel.shape, F32)
    for b in range(NUM_BUCKETS):
        bias = jnp.where((bucket == b)[None], table[b][expand], bias)
    return jnp.where((rel >= 0)[None], bias, NEG_INF)


def _lambda_full(lq1, lk1, lq2, lk2, li, lam_init):
    a = jnp.sum(lq1[li:li + 1, :] * lk1[li:li + 1, :], axis=-1, keepdims=True)
    b = jnp.sum(lq2[li:li + 1, :] * lk2[li:li + 1, :], axis=-1, keepdims=True)
    return jnp.exp(a) - jnp.exp(b) + lam_init


def _softmax_step(s, v, m_sc, l_sc, acc_sc):
    m_prev = m_sc[...]
    m_new = jnp.maximum(m_prev, jnp.max(s, axis=-1, keepdims=True))
    alpha = jnp.exp(m_prev - m_new)
    p = jnp.exp(s - m_new)
    l_sc[...] = alpha * l_sc[...] + jnp.sum(p, axis=-1, keepdims=True)
    acc_sc[...] = alpha * acc_sc[...] + jnp.dot(p.astype(BF16), v, preferred_element_type=F32)
    m_sc[...] = m_new


def _sub_rms(o, g, lam_init):
    y = o * lax.rsqrt(jnp.mean(o * o, axis=-1, keepdims=True) + EPS) * g
    return y * (1.0 - lam_init)


def _pattn_body(q_ref, k_ref, v_ref, bt_ref, lq1, lk1, lq2, lk2, g_ref, o_ref,
                vp_sc, m_sc, acc_sc, *, li, tq, nh, lam_init):
    i = pl.program_id(2)

    @pl.when(i == 0)
    def _():
        for a in range(nh):
            vp_sc[a, :, :KD] = v_ref[0, :, a * KD:(a + 1) * KD]
            vp_sc[a, :, KD:] = jnp.ones((vp_sc.shape[1], KD), BF16)

    m_sc[...] = jnp.full(m_sc.shape, NEG_INF, F32)
    acc_sc[...] = jnp.zeros(acc_sc.shape, F32)
    lane = lax.broadcasted_iota(jnp.int32, (tq, KD), 1)
    qqs = []
    for a in range(nh):
        q = q_ref[0, :, a * KD:(a + 1) * KD]
        zero = jnp.zeros_like(q)
        qqs.append(jnp.concatenate([jnp.where(lane < DIFF_DH, q, zero),
                                    jnp.where(lane >= DIFF_DH, q, zero)], axis=0))

    def block(off, width, bias_of_head):
        for a in range(nh):
            kb = k_ref[0, pl.ds(off, width), a * KD:(a + 1) * KD]
            vb = vp_sc[a, pl.ds(off, width), :]
            s = lax.dot_general(qqs[a], kb, (((1,), (1,)), ((), ())),
                                preferred_element_type=F32) + bias_of_head(a)
            m_prev = m_sc[a]
            m_new = jnp.maximum(m_prev, jnp.max(s, axis=-1, keepdims=True))
            alpha = jnp.exp(m_prev - m_new)
            p = jnp.exp(s - jnp.concatenate([m_new] * (width // LANES), axis=1))
            acc_sc[a] = (jnp.concatenate([alpha, alpha], axis=1) * acc_sc[a]
                         + jnp.dot(p.astype(BF16), vb, preferred_element_type=F32))
            m_sc[a] = m_new

    n_far = jnp.maximum(i - 1, 0)
    far_of_head = lambda a: bt_ref[a, 0, tq - 1:tq, 0:1]

    def far_body(j, carry):
        block(pl.multiple_of(j * 2 * tq, 2 * tq), 2 * tq, far_of_head)
        return carry

    lax.fori_loop(0, n_far // 2, far_body, 0)

    @pl.when(n_far % 2 == 1)
    def _():
        block(pl.multiple_of((n_far - 1) * tq, tq), tq, far_of_head)

    first = (i == 0).astype(jnp.int32)

    def near_of_head(a):
        b = bt_ref[a, first]
        return jnp.concatenate([b, b], axis=0)

    block(pl.multiple_of(n_far * tq, tq), 2 * tq, near_of_head)

    lam = _lambda_full(lq1, lk1, lq2, lk2, li, lam_init)
    g = g_ref[li:li + 1, :]
    for a in range(nh):
        acc = acc_sc[a]
        o = acc[:, :KD] / acc[:, KD:]
        o = o[:tq] - lam * o[tq:]
        o_ref[0, :, a * KD:(a + 1) * KD] = _sub_rms(o, g, lam_init).astype(o_ref.dtype)


def _prompt_attn(qkv, bt, lams, subln_g, *, li, n_heads, lam_init):
    bsz, l, _ = qkv.shape
    tq = bt.shape[2]
    nh = _pick(n_heads, (4, 2, 1))
    hb = n_heads // nh
    assert l % (2 * tq) == 0
    full = lambda a: pl.BlockSpec(a.shape, lambda b, hh, i: (0,) * a.ndim)
    body = functools.partial(_pattn_body, li=li, tq=tq, nh=nh, lam_init=lam_init)
    return pl.pallas_call(
        body, grid=(bsz, hb, l // tq),
        in_specs=[pl.BlockSpec((1, tq, nh * KD), lambda b, hh, i: (b, i, hh)),
                  pl.BlockSpec((1, l, nh * KD), lambda b, hh, i: (b, 0, hb + hh)),
                  pl.BlockSpec((1, l, nh * KD), lambda b, hh, i: (b, 0, 2 * hb + hh)),
                  pl.BlockSpec((nh, 2, tq, 2 * tq), lambda b, hh, i: (hh, 0, 0, 0))]
                 + [full(a) for a in lams] + [full(subln_g)],
        out_specs=pl.BlockSpec((1, tq, nh * KD), lambda b, hh, i: (b, i, hh)),
        out_shape=jax.ShapeDtypeStruct((bsz, l, n_heads * KD), BF16),
        scratch_shapes=[pltpu.VMEM((nh, l, 2 * KD), BF16), pltpu.VMEM((nh, 2 * tq, LANES), F32),
                        pltpu.VMEM((nh, 2 * tq, 2 * KD), F32)],
        compiler_params=_cparams(("parallel", "parallel", "arbitrary")))(
            qkv, qkv, qkv, bt, *lams, subln_g)


def _sattn_body(pt_ref, qt_ref, *refs, li, pp, n_heads, n_q, lam_init):
    krefs = refs[:pp]
    vrefs = refs[pp:2 * pp]
    kn_ref, vn_ref, bias_ref, lq1, lk1, lq2, lk2, g_ref, o_ref, m_sc, l_sc, acc_sc = refs[2 * pp:]
    jp = pl.program_id(1)
    last = pl.num_programs(1) - 1

    @pl.when(jp == 0)
    def _():
        m_sc[...] = jnp.full(m_sc.shape, NEG_INF, F32)
        l_sc[...] = jnp.zeros(l_sc.shape, F32)
        acc_sc[...] = jnp.zeros(acc_sc.shape, F32)

    def block(kbs, vbs, bias):
        s = jnp.stack([lax.dot_general(qt_ref[0, h], kbs[h], (((1,), (1,)), ((), ())),
                                       preferred_element_type=F32)
                       for h in range(n_heads)]) + bias
        m_prev = m_sc[...]
        m_new = jnp.maximum(m_prev, jnp.max(s, axis=-1, keepdims=True))
        alpha = jnp.exp(m_prev - m_new)
        p = jnp.exp(s - m_new)
        l_sc[...] = alpha * l_sc[...] + jnp.sum(p, axis=-1, keepdims=True)
        pb = p.astype(BF16)
        pv = jnp.stack([jnp.dot(pb[h], vbs[h], preferred_element_type=F32)
                        for h in range(n_heads)])
        acc_sc[...] = alpha * acc_sc[...] + pv
        m_sc[...] = m_new

    def head_rows(page_refs, h):
        rows_h = pl.ds(h, PAGE_SIZE, stride=n_heads)
        return jnp.concatenate([r[0, 0, rows_h, :] for r in page_refs], axis=0).astype(BF16)

    far = bias_ref[0]
    near = jnp.where(jp == last, bias_ref[1], far)
    block([head_rows(krefs, h) for h in range(n_heads)],
          [head_rows(vrefs, h) for h in range(n_heads)],
          jnp.concatenate([far] * (pp - 1) + [near], axis=-1))

    @pl.when(jp == last)
    def _():
        cols = [slice(h * KD, (h + 1) * KD) for h in range(n_heads)]
        block([kn_ref[0, :, c] for c in cols], [vn_ref[0, :, c] for c in cols], bias_ref[2])
        lam = _lambda_full(lq1, lk1, lq2, lk2, li, lam_init)
        g = g_ref[li:li + 1, :]
        o = acc_sc[...] / l_sc[...]
        o = _sub_rms(o[:, :n_q] - lam * o[:, n_q:], g, lam_init)
        for h in range(n_heads):
            o_ref[0, :, cols[h]] = o[h]


def _sample_attn(page_table, qt, cache_k, cache_v, k_new, v_new, bias3, lams, subln_g,
                 *, li, n_heads, n_q, lam_init):
    dbsz, n_pages = page_table.shape
    pp = _pick(n_pages, (PAGES_PER_STEP, 2, 1))
    w = n_heads * KD
    pt_flat = page_table.reshape(-1)

    def page_spec(g):
        return pl.BlockSpec((1, 1, PAGE_SIZE * n_heads, KD),
                            lambda b, j, pt, g=g: (li, pt[b * n_pages + j * pp + g], 0, 0))

    full = lambda a: pl.BlockSpec(a.shape, lambda b, j, pt: (0,) * a.ndim)
    per_b = pl.BlockSpec((1, LANES, w), lambda b, j, pt: (b, 0, 0))
    grid_spec = pltpu.PrefetchScalarGridSpec(
        num_scalar_prefetch=1, grid=(dbsz, n_pages // pp),
        in_specs=[pl.BlockSpec((1, n_heads, 2 * n_q, KD), lambda b, j, pt: (b, 0, 0, 0))]
                 + [page_spec(g) for g in range(pp)] * 2 + [per_b, per_b, full(bias3)]
                 + [full(a) for a in lams] + [full(subln_g)],
        out_specs=pl.BlockSpec((1, n_q, w), lambda b, j, pt: (b, 0, 0)),
        scratch_shapes=[pltpu.VMEM((n_heads, 2 * n_q, 1), F32),
                        pltpu.VMEM((n_heads, 2 * n_q, 1), F32),
                        pltpu.VMEM((n_heads, 2 * n_q, KD), F32)])
    body = functools.partial(_sattn_body, li=li, pp=pp, n_heads=n_heads, n_q=n_q,
                             lam_init=lam_init)
    return pl.pallas_call(
        body, grid_spec=grid_spec,
        out_shape=jax.ShapeDtypeStruct((dbsz, n_q, w), F32),
        compiler_params=_cparams(("parallel", "arbitrary")))(
            pt_flat, qt, *([cache_k] * pp), *([cache_v] * pp), k_new, v_new, bias3,
            *lams, subln_g)


def _block_queries(q, n_heads):
    dbsz, n_q, _ = q.shape
    qh = jnp.transpose(q.reshape(dbsz, n_q, n_heads, KD), (0, 2, 1, 3))
    col_map = (jnp.arange(KD) // DIFF_DH)[None, :] == jnp.arange(2)[:, None]
    t = qh[:, :, None] * col_map[None, None, :, None, :].astype(q.dtype)
    return t.reshape(dbsz, n_heads, 2 * n_q, KD)


def _sample_bias(t5_bias, n_heads, n_q):
    qi = jnp.arange(n_q, dtype=jnp.int32)
    key = jnp.arange(LANES, dtype=jnp.int32)

    def tile(rel):
        b = _bias_of_rel(t5_bias, rel)
        return jnp.concatenate([b, b], axis=1)

    far = tile(jnp.full((n_q, LANES), MAX_DISTANCE, jnp.int32))
    lastp = tile(PAGE_SIZE + qi[:, None] - key[None, :])
    new = tile(jnp.where(key[None, :] < n_q, qi[:, None] - key[None, :], -1))
    return jnp.stack([far, lastp, new])


def _ssd_body(xbc_ref, z_ref, dt_ref, h0_ref, dtb_ref, alog_ref, dx_ref, gn_ref,
              y_ref, hout_ref, hs, ysc, *, n_heads, valid_len):
    c = pl.program_id(1)
    t = SSD_CHUNK
    di = n_heads * SSD_P
    e_per_g = n_heads // SSD_G

    @pl.when(c == 0)
    def _():
        hs[...] = h0_ref[0]

    xbc = xbc_ref[0]
    xs = xbc[:, :di]
    row = lax.broadcasted_iota(jnp.int32, (t, LANES), 0)
    lane = lax.broadcasted_iota(jnp.int32, (t, LANES), 1)
    raw = dt_ref[0] + dtb_ref[...]
    dt = jnp.maximum(raw, 0.0) + jnp.log1p(jnp.exp(-jnp.abs(raw)))
    dt = jnp.where((row + c * t < valid_len) & (lane < n_heads), dt, 0.0)
    da = dt * (-jnp.exp(alog_ref[...]))
    tri = (lax.broadcasted_iota(jnp.int32, (t, t), 0)
           >= lax.broadcasted_iota(jnp.int32, (t, t), 1))
    cum = jnp.dot(tri.astype(F32), da, precision=lax.Precision.HIGHEST,
                  preferred_element_type=F32)
    cum_t = cum.T
    dt_t = dt.T
    cum_last = cum[t - 1:t, :]
    w_end = dt * jnp.exp(cum_last - cum)
    e_cum = jnp.exp(cum)
    e_last = jnp.exp(cum_last)
    xs_bf = xs.astype(BF16)
    xs_t = xs.T.astype(BF16)

    for g in range(SSD_G):
        b_g = xbc[:, di + g * SSD_N:di + (g + 1) * SSD_N]
        c_g = xbc[:, di + (SSD_G + g) * SSD_N:di + (SSD_G + g + 1) * SSD_N]
        cb = lax.dot_general(c_g.astype(BF16), b_g.astype(BF16), (((1,), (1,)), ((), ())),
                             preferred_element_type=F32)
        for e in range(e_per_g):
            h = g * e_per_g + e
            seg = cum[:, h:h + 1] - cum_t[h:h + 1, :]
            decay = jnp.where(tri, jnp.exp(jnp.where(tri, seg, 0.0)), 0.0)
            mat = (cb * decay * dt_t[h:h + 1, :]).astype(BF16)
            x_e = xs_bf[:, h * SSD_P:(h + 1) * SSD_P]
            y_d = jnp.dot(mat, x_e, preferred_element_type=F32)
            h_e = hs[h]
            c_w = (c_g * e_cum[:, h:h + 1]).astype(BF16)
            y_o = lax.dot_general(c_w, h_e.astype(BF16), (((1,), (1,)), ((), ())),
                                  preferred_element_type=F32)
            b_w = (b_g * w_end[:, h:h + 1]).astype(BF16)
            st = jnp.dot(xs_t[h * SSD_P:(h + 1) * SSD_P, :], b_w,
                         preferred_element_type=F32)
            hs[h] = h_e * e_last[:, h:h + 1] + st
            ysc[:, h * SSD_P:(h + 1) * SSD_P] = y_d + y_o

    y = ysc[...] + xs * dx_ref[...]
    z = z_ref[0]
    y = y * (z * jax.nn.sigmoid(z))
    gw = di // SSD_G
    for g in range(SSD_G):
        seg = y[:, g * gw:(g + 1) * gw]
        seg = seg * lax.rsqrt(jnp.mean(seg * seg, axis=-1, keepdims=True) + EPS)
        y_ref[0, :, g * gw:(g + 1) * gw] = (seg * gn_ref[:, g * gw:(g + 1) * gw]).astype(y_ref.dtype)

    @pl.when(c == pl.num_programs(1) - 1)
    def _():
        hout_ref[0] = hs[...]


def _ssd(xbc_act, zx, dt_raw, h0, dtb, alog, dx, gn, *, n_heads, valid_len):
    bsz, l, xbc_w = xbc_act.shape
    di = n_heads * SSD_P
    t = SSD_CHUNK
    assert l % t == 0
    full = lambda a: pl.BlockSpec(a.shape, lambda b, c: (0,) * a.ndim)
    body = functools.partial(_ssd_body, n_heads=n_heads, valid_len=valid_len)
    return pl.pallas_call(
        body, grid=(bsz, l // t),
        in_specs=[pl.BlockSpec((1, t, xbc_w), lambda b, c: (b, c, 0)),
                  pl.BlockSpec((1, t, di), lambda b, c: (b, c, 0)),
                  pl.BlockSpec((1, t, LANES), lambda b, c: (b, c, 0)),
                  pl.BlockSpec((1, n_heads, SSD_P, SSD_N), lambda b, c: (b, 0, 0, 0)),
                  full(dtb), full(alog), full(dx), full(gn)],
        out_specs=[pl.BlockSpec((1, t, di), lambda b, c: (b, c, 0)),
                   pl.BlockSpec((1, n_heads, SSD_P, SSD_N), lambda b, c: (b, 0, 0, 0))],
        out_shape=[jax.ShapeDtypeStruct((bsz, l, di), BF16),
                   jax.ShapeDtypeStruct((bsz, n_heads, SSD_P, SSD_N), F32)],
        scratch_shapes=[pltpu.VMEM((n_heads, SSD_P, SSD_N), F32), pltpu.VMEM((t, di), F32)],
        compiler_params=_cparams(("parallel", "arbitrary")))(
            xbc_act, zx, dt_raw, h0, dtb, alog, dx, gn)


def _epi_glu(accs, tiles, rows, j):
    a, b = accs
    return (a * jax.nn.sigmoid(b),)


def _epi_id(accs, tiles, rows, j):
    return (accs[0],)


def _epi_sigmoid(accs, tiles, rows, j):
    return (jax.nn.sigmoid(accs[0]),)


def _epi_merge(accs, tiles, rows, j):
    ya, yb, yc = accs
    ga, gb, gc = (t.astype(F32) for t in tiles)
    return (ga * ya + gb * yb + gc * yc,)


def _epi_resid(accs, tiles, rows, j):
    return (tiles[0] + rows[0] * accs[0],)


def _trunk(x, mod, prev_a, prev_c, prev_h, prev_f, wts, p, dims, *, paged=None):
    bsz, l, d = x.shape
    m = bsz * l
    depth, n_heads, conv_c, ssd_h, d_ff = (dims[k] for k in
                                           ("depth", "n_heads", "conv_c", "ssd_h", "d_ff"))
    att_w = n_heads * KD
    ssd_di = ssd_h * SSD_P
    ssd_xbc = ssd_di + 2 * SSD_G * SSD_N
    conv_k = p["conv_a_w"].shape[1]
    ssd_k = p["ssd_conv_w"].shape[1]
    ffn_k = p["ffn_conv_w"].shape[1]
    prompt = paged is None
    tm = _pick(l, (512, 256, 128)) if prompt else m
    tpb = l // tm if prompt else 1
    rpb = l if prompt else None
    t_seq = _pick(l, (128,)) if prompt else l
    lams = (p["lambda_q1"], p["lambda_k1"], p["lambda_q2"], p["lambda_k2"])
    scale = DIFF_DH ** -0.5

    xf = x.reshape(m, d)
    out_a, out_c, out_h, out_f = [], [], [], []
    kst = jnp.zeros((depth, m, att_w), F32)
    vst = jnp.zeros((depth, m, att_w), F32)
    for li in range(depth):
        w = wts
        md = mod[li]
        lam_init = 0.8 - 0.6 * math.exp(-0.3 * li)
        h = _norm_mod(xf, p["norm1_g"][li][None], md, 1, 0, tm=tm, tpb=tpb)

        c_qkv = 2 * conv_c
        c_zx = c_qkv + 3 * att_w
        zx_w = ssd_di + ssd_xbc
        glu_w = [(w["in"], li, 0, 0), (w["in"], li, conv_c, 0)]
        qkv_bf, kst, vst = _qkv(h, w["in"], c_qkv, kst, vst, li=li, scale=scale,
                                rows_per_batch=rpb)
        (zx,) = _mm([h], [(w["in"], li, c_zx, 0)], _epi_id, [F32], zx_w, rows_per_batch=rpb)
        (dt_raw,) = _mm([h], [(w["dt"], li, 0, 0)], _epi_id, [F32], LANES, rows_per_batch=rpb)
        (gates,) = _mm([h], [(w["gates"], li, 0, 0)], _epi_sigmoid, [BF16], 3 * d,
                       rows_per_batch=rpb)

        ln_rows = (p["ln_a_g"][li][None], p["ln_a_b"][li][None])
        if prompt and l >= GLU_HALO:
            a_act = _gluconv(h, w["in"], li, _pad_prev(prev_a[li], GLU_HALO), p["conv_a_w"][li],
                             p["conv_a_b"][li][None], *ln_rows, c=conv_c, rows_per_batch=rpb)
            h_tail = h.reshape(bsz, l, d)[:, l - GLU_HALO:].reshape(bsz * GLU_HALO, d)
            (u_tail,) = _mm([h_tail], glu_w, _epi_glu, [F32], conv_c)
            out_a.append(u_tail.reshape(bsz, GLU_HALO, conv_c)[:, GLU_HALO - (conv_k - 1):])
        else:
            (u,) = _mm([h], glu_w, _epi_glu, [F32], conv_c, rows_per_batch=rpb)
            u3 = u.reshape(bsz, l, conv_c)
            out_a.append(_conv_state(prev_a[li], u3, conv_k - 1))
            a_act = _dwconv(u3, 0, _pad_prev(prev_a[li], GLU_HALO), p["conv_a_w"][li],
                            p["conv_a_b"][li][None], _epi_conv_a, BF16, c=conv_c, t=t_seq,
                            tc=conv_c, halo=GLU_HALO, prows=ln_rows)

        if prompt:
            o = _prompt_attn(qkv_bf.reshape(bsz, l, 3 * att_w), dims["bias_tiles"], lams,
                             p["subln_g"], li=li, n_heads=n_heads, lam_init=lam_init)
        else:
            qb = qkv_bf.reshape(bsz, l, 3 * att_w)
            qt = _block_queries(qb[..., :att_w], n_heads)
            pad = ((0, 0), (0, LANES - l), (0, 0))
            k_new = jnp.pad(qb[..., att_w:2 * att_w], pad)
            v_new = jnp.pad(qb[..., 2 * att_w:], pad)
            o = _sample_attn(paged["page_table"], qt, paged["cache_k"], paged["cache_v"],
                             k_new, v_new, dims["bias3"], lams, p["subln_g"], li=li,
                             n_heads=n_heads, n_q=l, lam_init=lam_init).astype(BF16)
        o = o.reshape(m, att_w)

        zx3 = zx.reshape(bsz, l, zx_w)
        dt3 = dt_raw.reshape(bsz, l, LANES)
        xbc_raw = zx3[..., ssd_di:]
        out_c.append(_conv_state(prev_c[li], xbc_raw, ssd_k - 1))
        tc = _pick(math.gcd(ssd_di, ssd_xbc), (512, 256, 128))
        xbc_act = _dwconv(zx3, ssd_di // tc, _pad_prev(prev_c[li], 8), p["ssd_conv_w"][li],
                          p["ssd_conv_b"][li][None], _epi_silu, F32, c=ssd_xbc,
                          t=_pick(l, (512, 256, 128)) if prompt else l, tc=tc, halo=8)
        if prompt:
            xbc_in, zx_in, dt_in = xbc_act, zx3, dt3
        else:
            pad = ((0, 0), (0, SSD_CHUNK - l), (0, 0))
            xbc_in, zx_in, dt_in = jnp.pad(xbc_act, pad), jnp.pad(zx3, pad), jnp.pad(dt3, pad)
        lane_pad = lambda a: jnp.pad(a, (0, LANES - a.shape[0]))[None]
        y_c, h_t = _ssd(xbc_in, zx_in, dt_in, prev_h[li],
                        lane_pad(p["dt_bias"][li]), lane_pad(p["a_log"][li]),
                        jnp.repeat(p["d_skip"][li], SSD_P)[None], p["ssd_norm_g"][li][None],
                        n_heads=ssd_h, valid_len=l)
        out_h.append(h_t)
        y_c = y_c[:, :l].reshape(m, ssd_di)

        (merged,) = _mm([a_act.reshape(m, conv_c), o, y_c],
                        [(w["proj_a"], li, 0, 0), (w["proj_b"], li, 0, 1), (w["proj_c"], li, 0, 2)],
                        _epi_merge, [BF16], d, tiles=[(gates, 0), (gates, d), (gates, 2 * d)],
                        rows_per_batch=rpb)
        (xf,) = _mm([merged], [(w["out"], li, 0, 0)], _epi_resid, [F32], d, tiles=[(xf, 0)],
                    rows=[(md, 2 * d)], rows_per_batch=rpb)

        h2 = _norm_mod(xf, p["norm2_g"][li][None], md, 4, 3, tm=tm, tpb=tpb)
        if prompt:
            act = _upconv(h2, w["up"], li, _pad_prev(prev_f[li], UP_HALO), p["ffn_conv_w"][li],
                          p["ffn_conv_b"][li][None], d_ff=d_ff, rows_per_batch=rpb)
            assert l >= UP_HALO
            h2_tail = h2.reshape(bsz, l, d)[:, l - UP_HALO:].reshape(bsz * UP_HALO, d)
            (fa_tail,) = _mm([h2_tail], [(w["up"], li, 0, 0)], _epi_id, [F32], d_ff)
            out_f.append(fa_tail.reshape(bsz, UP_HALO, d_ff)[:, UP_HALO - (ffn_k - 1):])
        else:
            (up,) = _mm([h2], [(w["up"], li, 0, 0)], _epi_id, [F32], 2 * d_ff)
            up3 = up.reshape(bsz, l, 2 * d_ff)
            out_f.append(_conv_state(prev_f[li], up3[..., :d_ff], ffn_k - 1))
            tc = d_ff
            act = _dwconv(up3, 0, _pad_prev(prev_f[li], 8), p["ffn_conv_w"][li],
                          p["ffn_conv_b"][li][None], _epi_ffn, BF16, c=d_ff, t=t_seq, tc=tc,
                          halo=8, xtiles=[(up3, d_ff // tc)])
        (xf,) = _mm([act.reshape(m, d_ff)], [(w["down"], li, 0, 0)], _epi_resid, [F32], d,
                    tiles=[(xf, 0)], rows=[(md, 5 * d)], rows_per_batch=rpb)

    y_out = _rms(xf, p["final_g"][None], tm=tm).reshape(bsz, l, d)
    kv_shape = (depth, bsz, l, n_heads, KD)
    return (y_out, kst.reshape(kv_shape), vst.reshape(kv_shape), jnp.stack(out_a), jnp.stack(out_c),
            jnp.stack(out_h), jnp.stack(out_f))


def _bf16_weights(p, dims):
    conv_c, att_w = dims["conv_c"], dims["n_heads"] * KD
    ssd_di = dims["ssd_h"] * SSD_P
    ssd_xbc = ssd_di + 2 * SSD_G * SSD_N
    c_dt = 2 * conv_c + 3 * att_w + ssd_di + ssd_xbc
    c_gates = c_dt + dims["ssd_h"]
    w_in = p["w_in"].astype(BF16)
    dt_cols = jnp.pad(w_in[:, :, c_dt:c_gates], ((0, 0), (0, 0), (0, LANES - dims["ssd_h"])))
    bf = lambda a: a.astype(BF16)
    return {"in": w_in, "dt": dt_cols, "gates": w_in[:, :, c_gates:],
            "proj_a": bf(p["w_proj_a"]), "proj_b": bf(p["w_proj_b"]), "proj_c": bf(p["w_proj_c"]),
            "out": bf(p["w_out"]), "up": bf(p["ffn_up"]), "down": bf(p["ffn_down"])}


def kernel(x_prompt, x_sample, cache_k, cache_v, page_table, state_conv_a, state_ssd_conv,
           state_ssm, state_ffn_conv, c_prompt, c_sample, w_ada, b_ada, norm1_g, w_in,
           conv_a_w, conv_a_b, ln_a_g, ln_a_b, w_proj_a, t5_bias, lambda_q1, lambda_k1,
           lambda_q2, lambda_k2, subln_g, w_proj_b, ssd_conv_w, ssd_conv_b, dt_bias, a_log,
           d_skip, ssd_norm_g, w_proj_c, w_out, norm2_g, ffn_up, ffn_conv_w, ffn_conv_b,
           ffn_down, final_g):
    p = dict(w_ada=w_ada, b_ada=b_ada, norm1_g=norm1_g, w_in=w_in, conv_a_w=conv_a_w,
             conv_a_b=conv_a_b, ln_a_g=ln_a_g, ln_a_b=ln_a_b, w_proj_a=w_proj_a, t5_bias=t5_bias,
             lambda_q1=lambda_q1, lambda_k1=lambda_k1, lambda_q2=lambda_q2, lambda_k2=lambda_k2,
             subln_g=subln_g, w_proj_b=w_proj_b, ssd_conv_w=ssd_conv_w, ssd_conv_b=ssd_conv_b,
             dt_bias=dt_bias, a_log=a_log, d_skip=d_skip, ssd_norm_g=ssd_norm_g, w_proj_c=w_proj_c,
             w_out=w_out, norm2_g=norm2_g, ffn_up=ffn_up, ffn_conv_w=ffn_conv_w,
             ffn_conv_b=ffn_conv_b, ffn_down=ffn_down, final_g=final_g)
    bp, lp, d = x_prompt.shape
    bs, ls, _ = x_sample.shape
    depth = w_ada.shape[0]
    n_heads = cache_k.shape[3]
    n_pages = page_table.shape[1]
    dims = dict(depth=depth, n_heads=n_heads, conv_c=conv_a_w.shape[2], ssd_h=dt_bias.shape[1],
                d_ff=ffn_conv_w.shape[2])
    f32 = x_prompt.dtype

    tq = _pick(lp, (ATTN_TQ, 128))
    assert tq >= MAX_DISTANCE
    qi = jnp.arange(tq, dtype=jnp.int32)
    rel0 = qi[:, None] - qi[None, :]
    b_diag, b_prev = _bias_of_rel(t5_bias, rel0), _bias_of_rel(t5_bias, rel0 + tq)
    dims["bias_tiles"] = jnp.stack(
        [jnp.concatenate([b_prev, b_diag], axis=-1),
         jnp.concatenate([b_diag, jnp.full_like(b_diag, NEG_INF)], axis=-1)], axis=1)
    dims["bias3"] = _sample_bias(t5_bias, n_heads, ls)

    rows = -(-(bp + bs) // 8) * 8
    c_all = jnp.pad(jnp.concatenate([c_prompt, c_sample], axis=0), ((0, rows - bp - bs), (0, 0)))
    mod = _ada(c_all, w_ada, b_ada)
    mod_p = mod[:, :bp, None, :]
    mod_s = jnp.repeat(mod[:, bp:bp + bs], ls, axis=1)[:, None]

    wts = _bf16_weights(p, dims)
    zero = lambda *s: jnp.zeros((depth, bp) + s, f32)
    res_p = _trunk(x_prompt, mod_p, zero(conv_a_w.shape[1] - 1, dims["conv_c"]),
                   zero(ssd_conv_w.shape[1] - 1, ssd_conv_w.shape[2]),
                   zero(dims["ssd_h"], SSD_P, SSD_N), zero(ffn_conv_w.shape[1] - 1, dims["d_ff"]),
                   wts, p, dims)
    paged = dict(page_table=page_table,
                 cache_k=cache_k.reshape(depth, -1, PAGE_SIZE * n_heads, KD),
                 cache_v=cache_v.reshape(depth, -1, PAGE_SIZE * n_heads, KD))
    res_s = _trunk(x_sample, mod_s, state_conv_a, state_ssd_conv, state_ssm, state_ffn_conv,
                   wts, p, dims, paged=paged)
    y_p, k_p, v_p, a_p, c_p, h_p, f_p = res_p
    y_s, k_s, v_s, a_s, c_s, h_s, f_s = res_s
    return (y_p, y_s, k_p, v_p, k_s, v_s, a_p, a_s, c_p, c_s, h_p, h_s, f_p, f_s)
```

```python
import functools
import math

import jax
import jax.numpy as jnp
from jax import lax
from jax.experimental import pallas as pl
from jax.experimental.pallas import tpu as pltpu

F32 = jnp.float32
BF16 = jnp.bfloat16

EPS = 1e-6
LN_EPS = 1e-5
NEG_INF = -1e30
DIFF_DH = 64
KD = 2 * DIFF_DH
NUM_BUCKETS = 32
MAX_DISTANCE = 128
SSD_P = 64
SSD_N = 128
SSD_G = 2
SSD_CHUNK = 128
PAGE_SIZE = 128
LANES = 128
VMEM_LIMIT = 56 * 1024 * 1024
VMEM_BLOCK_BUDGET = 46 * 1024 * 1024
MIN_TN = 512
MM_ROW_CHUNK = 512

ATTN_TQ = 256
PAGES_PER_STEP = 16


def _cparams(sem):
    return pltpu.CompilerParams(dimension_semantics=sem, vmem_limit_bytes=VMEM_LIMIT)


def _pick(n, cands):
    for c in cands:
        if n % c == 0:
            return c
    return n


def _fit_tiles(m_cands, n_cands, block_bytes):
    wide = [c for c in n_cands if c >= MIN_TN] or list(n_cands)
    for cols in (wide, list(n_cands)):
        for tm in m_cands:
            for tn in cols:
                if block_bytes(tm, tn) <= VMEM_BLOCK_BUDGET:
                    return tm, tn
    return m_cands[-1], n_cands[-1]


def _row_cands(rows_per_batch, whole=None):
    if whole is not None:
        return [whole]
    return [c for c in (2048, 1024, 512, 256, 128) if rows_per_batch % c == 0] or [rows_per_batch]


def _col_cands(n, offsets=()):
    cands = [c for c in (1024, 768, 512, 256, 896, 640, 384, 128)
             if n % c == 0 and all(o % c == 0 for o in offsets)]
    return cands or [n]


def _mm_body(*refs, nl, lhs_of, nt, nr, epi, rm):
    nw = len(lhs_of)
    lrefs = refs[:nl]
    wrefs = refs[nl:nl + nw]
    trefs = refs[nl + nw:nl + nw + nt]
    rrefs = refs[nl + nw + nt:nl + nw + nt + nr]
    orefs = refs[nl + nw + nt + nr:]
    tm = orefs[0].shape[0]
    j = pl.program_id(1)
    for m0 in range(0, tm, rm):
        rs = slice(m0, m0 + rm)
        accs = [jnp.dot(lrefs[li][rs, :], w[...], preferred_element_type=F32)
                for w, li in zip(wrefs, lhs_of)]
        rows = [r[0] if r.shape[1] == 1 else r[0, rs, :] for r in rrefs]
        res = epi(accs, [t[rs, :] for t in trefs], rows, j)
        for o, r in zip(orefs, res):
            o[rs, :] = r.astype(o.dtype)


def _mm(lhs, rhs, epi, out_dtypes, n, *, tiles=(), rows=(), rows_per_batch=None):
    m = lhs[0].shape[0]
    offsets = [off for _, _, off, _ in rhs] + [off for _, off in tiles] + [off for _, off in rows]

    def block_bytes(tm, tn):
        b = sum(tm * a.shape[1] * a.dtype.itemsize for a in lhs)
        b += sum(a.shape[1] * tn * a.dtype.itemsize for a, _, _, _ in rhs)
        b += sum(tm * tn * a.dtype.itemsize for a, _ in tiles)
        b += sum(tm * tn * jnp.dtype(d).itemsize for d in out_dtypes)
        return 2 * b + len(rhs) * min(tm, MM_ROW_CHUNK) * tn * 4

    tm, tn = _fit_tiles(_row_cands(rows_per_batch, None if rows_per_batch else m),
                        _col_cands(n, offsets), block_bytes)
    tpb = rows_per_batch // tm if rows_per_batch else 1
    assert m % tm == 0 and n % tn == 0
    in_specs, args = [], []
    for a in lhs:
        in_specs.append(pl.BlockSpec((tm, a.shape[1]), lambda i, j: (i, 0)))
        args.append(a)
    for a, li, off, _ in rhs:
        in_specs.append(pl.BlockSpec((None, a.shape[1], tn),
                                     lambda i, j, li=li, o=off // tn: (li, 0, j + o)))
        args.append(a)
    for a, off in tiles:
        in_specs.append(pl.BlockSpec((tm, tn), lambda i, j, o=off // tn: (i, j + o)))
        args.append(a)
    for a, off in rows:
        assert a.shape[1] == 1 or (a.shape[0] == 1 and a.shape[1] == tm == m)
        in_specs.append(pl.BlockSpec((1, a.shape[1], tn),
                                     lambda i, j, o=off // tn: (i // tpb, 0, j + o)))
        args.append(a)
    out_specs = [pl.BlockSpec((tm, tn), lambda i, j: (i, j)) for _ in out_dtypes]
    out_shape = [jax.ShapeDtypeStruct((m, n), d) for d in out_dtypes]
    body = functools.partial(_mm_body, nl=len(lhs), lhs_of=tuple(k for _, _, _, k in rhs),
                             nt=len(tiles), nr=len(rows), epi=epi, rm=min(tm, MM_ROW_CHUNK))
    return pl.pallas_call(
        body, grid=(m // tm, n // tn), in_specs=in_specs, out_specs=out_specs,
        out_shape=out_shape, compiler_params=_cparams(("parallel", "arbitrary")))(*args)


def _qkv_body(h_ref, w_ref, kin_ref, vin_ref, qkv_ref, kst_ref, vst_ref, *, nq, scale):
    del kin_ref, vin_ref
    j = pl.program_id(1)
    acc = jnp.dot(h_ref[...], w_ref[...], preferred_element_type=F32)
    qkv_ref[...] = (acc * jnp.where(j < nq, scale, 1.0)).astype(qkv_ref.dtype)

    @pl.when((j >= nq) & (j < 2 * nq))
    def _():
        kst_ref[0] = acc

    @pl.when(j >= 2 * nq)
    def _():
        vst_ref[0] = acc


def _qkv(h, w, col0, kst, vst, *, li, scale, rows_per_batch=None):
    m, d = h.shape
    wdt = kst.shape[2]
    block_bytes = lambda tm, tn: 2 * (tm * d * 2 + d * tn * 2 + tm * tn * (2 + 4 + 4)) + tm * tn * 4
    tm, tn = _fit_tiles(_row_cands(rows_per_batch, None if rows_per_batch else m),
                        _col_cands(wdt, (col0,)), block_bytes)
    nq = wdt // tn
    j0 = col0 // tn
    assert m % tm == 0 and wdt % tn == 0 and col0 % tn == 0
    k_idx = lambda i, j: (li, i, jnp.clip(j - nq, 0, nq - 1))
    v_idx = lambda i, j: (li, i, jnp.clip(j - 2 * nq, 0, nq - 1))
    body = functools.partial(_qkv_body, nq=nq, scale=scale)
    return pl.pallas_call(
        body, grid=(m // tm, 3 * nq),
        in_specs=[pl.BlockSpec((tm, d), lambda i, j: (i, 0)),
                  pl.BlockSpec((None, d, tn), lambda i, j: (li, 0, j0 + j)),
                  pl.BlockSpec(memory_space=pl.ANY), pl.BlockSpec(memory_space=pl.ANY)],
        out_specs=[pl.BlockSpec((tm, tn), lambda i, j: (i, j)),
                   pl.BlockSpec((1, tm, tn), k_idx), pl.BlockSpec((1, tm, tn), v_idx)],
        out_shape=[jax.ShapeDtypeStruct((m, 3 * wdt), BF16),
                   jax.ShapeDtypeStruct(kst.shape, F32), jax.ShapeDtypeStruct(vst.shape, F32)],
        input_output_aliases={2: 1, 3: 2},
        compiler_params=_cparams(("arbitrary", "arbitrary")))(h, w, kst, vst)


def _ada_body(c_ref, w_ref, b_ref, o_ref):
    c = c_ref[...]
    s = (c * jax.nn.sigmoid(c)).astype(BF16)
    o_ref[0] = jnp.dot(s, w_ref[0].astype(BF16), preferred_element_type=F32) + b_ref[0]


def _ada(c_all, w_ada, b_ada):
    depth, d, n6 = w_ada.shape
    rows = c_all.shape[0]
    tn = _pick(n6, (1024, 512, 256, 128))
    return pl.pallas_call(
        _ada_body, grid=(depth, n6 // tn),
        in_specs=[pl.BlockSpec((rows, d), lambda l, j: (0, 0)),
                  pl.BlockSpec((1, d, tn), lambda l, j: (l, 0, j)),
                  pl.BlockSpec((1, 1, tn), lambda l, j: (l, 0, j))],
        out_specs=pl.BlockSpec((1, rows, tn), lambda l, j: (l, 0, j)),
        out_shape=jax.ShapeDtypeStruct((depth, rows, n6), F32),
        compiler_params=_cparams(("arbitrary", "arbitrary")))(c_all, w_ada, b_ada[:, None, :])


NORM_ROW_CHUNK = 32


def _norm_mod_body(x_ref, g_ref, sc_ref, sh_ref, o_ref):
    tm = x_ref.shape[0]
    rc = min(tm, NORM_ROW_CHUNK)
    g = g_ref[...]
    for r0 in range(0, tm, rc):
        rs = slice(r0, r0 + rc)
        x = x_ref[rs, :]
        y = x * lax.rsqrt(jnp.mean(x * x, axis=-1, keepdims=True) + EPS) * g
        sc, sh = (r[0] if r.shape[1] == 1 else r[0, rs, :] for r in (sc_ref, sh_ref))
        o_ref[rs, :] = (y * (1.0 + sc) + sh).astype(o_ref.dtype)


def _norm_mod(x, g, mod, sc_blk, sh_blk, *, tm, tpb):
    m, d = x.shape
    r = mod.shape[1]
    return pl.pallas_call(
        _norm_mod_body, grid=(m // tm,),
        in_specs=[pl.BlockSpec((tm, d), lambda i: (i, 0)),
                  pl.BlockSpec((1, d), lambda i: (0, 0)),
                  pl.BlockSpec((1, r, d), lambda i: (i // tpb, 0, sc_blk)),
                  pl.BlockSpec((1, r, d), lambda i: (i // tpb, 0, sh_blk))],
        out_specs=pl.BlockSpec((tm, d), lambda i: (i, 0)),
        out_shape=jax.ShapeDtypeStruct((m, d), BF16),
        compiler_params=_cparams(("parallel",)))(x, g, mod, mod)


def _rms_body(x_ref, g_ref, o_ref):
    tm = x_ref.shape[0]
    rc = min(tm, NORM_ROW_CHUNK)
    g = g_ref[...]
    for r0 in range(0, tm, rc):
        x = x_ref[r0:r0 + rc, :]
        o_ref[r0:r0 + rc, :] = x * lax.rsqrt(jnp.mean(x * x, axis=-1, keepdims=True) + EPS) * g


def _rms(x, g, *, tm):
    m, d = x.shape
    return pl.pallas_call(
        _rms_body, grid=(m // tm,),
        in_specs=[pl.BlockSpec((tm, d), lambda i: (i, 0)),
                  pl.BlockSpec((1, d), lambda i: (0, 0))],
        out_specs=pl.BlockSpec((tm, d), lambda i: (i, 0)),
        out_shape=jax.ShapeDtypeStruct((m, d), F32),
        compiler_params=_cparams(("parallel",)))(x, g)


def _dwconv_body(*refs, kc, halo, t, tc, n_seq_tiles, nx, npr, epi):
    u_ref, prev_ref, w_ref, b_ref = refs[:4]
    xrefs = refs[4:4 + nx]
    prefs = refs[4 + nx:4 + nx + npr]
    o_ref = refs[4 + nx + npr]
    win, cv = refs[4 + nx + npr + 1:]

    @pl.when(pl.program_id(2) == 0)
    def _():
        win[0:halo, :] = prev_ref[0]

    win[halo:halo + t, :] = u_ref[0]
    base = halo - (kc - 1)
    rc = min(t, 64)
    cc = min(tc, 256)
    for r0 in range(0, t, rc):
        for c0 in range(0, tc, cc):
            acc = jnp.broadcast_to(b_ref[:, c0:c0 + cc], (rc, cc))
            wlen = rc + halo
            window = win[r0:r0 + wlen, c0:c0 + cc]
            for r in range(8):
                taps = [j for j in range(kc) if (base + j) % 8 == r]
                if not taps:
                    continue
                sh = window if r == 0 else pltpu.roll(window, wlen - r, 0)
                for j in taps:
                    a = (base + j) // 8
                    acc = acc + w_ref[j:j + 1, c0:c0 + cc] * sh[8 * a:8 * a + rc]
            cv[r0:r0 + rc, c0:c0 + cc] = acc
    o_ref[0] = epi(cv[...], [x[0] for x in xrefs], [p[...] for p in prefs]).astype(o_ref.dtype)
    if n_seq_tiles > 1:
        win[0:halo, :] = win[t:t + halo, :]


def _dwconv(u, u_blk, prev, w, b, epi, out_dtype, *, c, t, tc, halo, xtiles=(), prows=()):
    bsz, l, _ = u.shape
    kc = w.shape[0]
    assert l % t == 0 and c % tc == 0 and halo >= kc - 1 and (t >= halo or l == t)
    nlt = l // t
    in_specs = [pl.BlockSpec((1, t, tc), lambda bi, ci, li: (bi, li, ci + u_blk)),
                pl.BlockSpec((1, halo, tc), lambda bi, ci, li: (bi, 0, ci)),
                pl.BlockSpec((kc, tc), lambda bi, ci, li: (0, ci)),
                pl.BlockSpec((1, tc), lambda bi, ci, li: (0, ci))]
    args = [u, prev, w, b]
    for a, off in xtiles:
        in_specs.append(pl.BlockSpec((1, t, tc), lambda bi, ci, li, off=off: (bi, li, ci + off)))
        args.append(a)
    for a in prows:
        in_specs.append(pl.BlockSpec((1, tc), lambda bi, ci, li: (0, ci)))
        args.append(a)
    body = functools.partial(_dwconv_body, kc=kc, halo=halo, t=t, tc=tc, n_seq_tiles=nlt,
                             nx=len(xtiles), npr=len(prows), epi=epi)
    return pl.pallas_call(
        body, grid=(bsz, c // tc, nlt), in_specs=in_specs,
        out_specs=pl.BlockSpec((1, t, tc), lambda bi, ci, li: (bi, li, ci)),
        out_shape=jax.ShapeDtypeStruct((bsz, l, c), out_dtype),
        scratch_shapes=[pltpu.VMEM((halo + t, tc), F32), pltpu.VMEM((t, tc), F32)],
        compiler_params=_cparams(("parallel", "parallel", "arbitrary")))(*args)


def _epi_conv_a(cv, xs, ps):
    g, b = ps
    mu = jnp.mean(cv, axis=-1, keepdims=True)
    d = cv - mu
    var = jnp.mean(d * d, axis=-1, keepdims=True)
    y = d * lax.rsqrt(var + LN_EPS) * g + b
    return y * jax.nn.sigmoid(y)


def _epi_silu(cv, xs, ps):
    return cv * jax.nn.sigmoid(cv)


def _epi_ffn(cv, xs, ps):
    return jax.nn.gelu(cv) * xs[0]


UP_HALO = 8
UP_MAX_TM = 1024


def _upconv_body(h_ref, wa_ref, wb_ref, prev_ref, cw_ref, cb_ref, act_ref,
                 win, carry, *, tm, tn, tpb, kc):
    i = pl.program_id(0)
    j = pl.program_id(1)
    first = i % tpb == 0

    @pl.when(first)
    def _():
        win[0:UP_HALO, :] = prev_ref[0]

    @pl.when(jnp.logical_not(first))
    def _():
        win[0:UP_HALO, :] = carry[j]

    base = UP_HALO - (kc - 1)
    rm = min(tm, 512)
    rc = min(rm, 64)
    cc = min(tn, 256)
    for m0 in range(0, tm, rm):
        h = h_ref[m0:m0 + rm, :]
        fa = jnp.dot(h, wa_ref[...], preferred_element_type=F32)
        fb = jnp.dot(h, wb_ref[...], preferred_element_type=F32)
        win[UP_HALO + m0:UP_HALO + m0 + rm, :] = fa
        for r0 in range(0, rm, rc):
            for c0 in range(0, tn, cc):
                acc = jnp.broadcast_to(cb_ref[:, c0:c0 + cc], (rc, cc))
                for t in range(kc):
                    start = base + m0 + r0 + t
                    acc = acc + cw_ref[t:t + 1, c0:c0 + cc] * win[start:start + rc, c0:c0 + cc]
                act_ref[m0 + r0:m0 + r0 + rc, c0:c0 + cc] = (
                    jax.nn.gelu(acc) * fb[r0:r0 + rc, c0:c0 + cc]).astype(act_ref.dtype)
    carry[j] = win[tm:tm + UP_HALO, :]


def _upconv(h, w_up, li, prev, cw, cb, *, d_ff, rows_per_batch):
    m, d = h.shape
    kc = cw.shape[0]
    block_bytes = lambda tm, tn: (2 * (tm * d * 2 + 2 * d * tn * 2 + tm * tn * 2)
                                  + (tm + UP_HALO) * tn * 4 + 2 * min(tm, 512) * tn * 4)
    rows = [c for c in _row_cands(rows_per_batch) if c <= UP_MAX_TM] or [rows_per_batch]
    tm, tn = _fit_tiles(rows, _col_cands(d_ff), block_bytes)
    tpb = rows_per_batch // tm
    nj = d_ff // tn
    assert m % tm == 0 and d_ff % tn == 0 and kc - 1 <= UP_HALO <= tm
    body = functools.partial(_upconv_body, tm=tm, tn=tn, tpb=tpb, kc=kc)
    return pl.pallas_call(
        body, grid=(m // tm, nj),
        in_specs=[pl.BlockSpec((tm, d), lambda i, j: (i, 0)),
                  pl.BlockSpec((None, d, tn), lambda i, j: (li, 0, j)),
                  pl.BlockSpec((None, d, tn), lambda i, j: (li, 0, j + nj)),
                  pl.BlockSpec((1, UP_HALO, tn), lambda i, j: (i // tpb, 0, j)),
                  pl.BlockSpec((kc, tn), lambda i, j: (0, j)),
                  pl.BlockSpec((1, tn), lambda i, j: (0, j))],
        out_specs=pl.BlockSpec((tm, tn), lambda i, j: (i, j)),
        out_shape=jax.ShapeDtypeStruct((m, d_ff), BF16),
        scratch_shapes=[pltpu.VMEM((UP_HALO + tm, tn), F32), pltpu.VMEM((nj, UP_HALO, tn), F32)],
        compiler_params=_cparams(("arbitrary", "arbitrary")))(h, w_up, w_up, prev, cw, cb)


GLU_HALO = 32
GLU_TM = 512


def _gluconv_body(h_ref, wa_ref, wb_ref, prev_ref, cw_ref, cb_ref, g_ref, b_ref, o_ref,
                  win, cv, *, tm, c, tpb, kc):
    halo = GLU_HALO

    @pl.when(pl.program_id(0) % tpb == 0)
    def _():
        win[0:halo, :] = prev_ref[0]

    base = halo - (kc - 1)
    rm = min(tm, 256)
    rc = min(rm, 64)
    cc = min(c, LANES)
    for m0 in range(0, tm, rm):
        h = h_ref[m0:m0 + rm, :]
        a = jnp.dot(h, wa_ref[...], preferred_element_type=F32)
        b = jnp.dot(h, wb_ref[...], preferred_element_type=F32)
        win[halo + m0:halo + m0 + rm, :] = a * jax.nn.sigmoid(b)
        for r0 in range(m0, m0 + rm, rc):
            for c0 in range(0, c, cc):
                acc = jnp.broadcast_to(cb_ref[:, c0:c0 + cc], (rc, cc))
                wlen = rc + halo
                window = win[r0:r0 + wlen, c0:c0 + cc]
                for r in range(8):
                    taps = [j for j in range(kc) if (base + j) % 8 == r]
                    if not taps:
                        continue
                    sh = window if r == 0 else pltpu.roll(window, wlen - r, 0)
                    for j in taps:
                        t8 = (base + j) // 8 * 8
                        acc = acc + cw_ref[j:j + 1, c0:c0 + cc] * sh[t8:t8 + rc]
                cv[r0:r0 + rc, c0:c0 + cc] = acc
        y = _epi_conv_a(cv[m0:m0 + rm, :], (), (g_ref[...], b_ref[...]))
        o_ref[m0:m0 + rm, :] = y.astype(o_ref.dtype)
    win[0:halo, :] = win[tm:tm + halo, :]


def _gluconv(h, w_in, li, prev, cw, cb, ln_g, ln_b, *, c, rows_per_batch):
    m, d = h.shape
    kc = cw.shape[0]
    tm = _pick(rows_per_batch, (GLU_TM, 256, 128))
    tpb = rows_per_batch // tm
    assert m % tm == 0 and kc - 1 <= GLU_HALO <= tm and c % LANES == 0
    full = lambda a: pl.BlockSpec(a.shape, lambda i: (0,) * a.ndim)
    body = functools.partial(_gluconv_body, tm=tm, c=c, tpb=tpb, kc=kc)
    return pl.pallas_call(
        body, grid=(m // tm,),
        in_specs=[pl.BlockSpec((tm, d), lambda i: (i, 0)),
                  pl.BlockSpec((None, d, c), lambda i: (li, 0, 0)),
                  pl.BlockSpec((None, d, c), lambda i: (li, 0, 1)),
                  pl.BlockSpec((1, GLU_HALO, c), lambda i: (i // tpb, 0, 0)),
                  full(cw), full(cb), full(ln_g), full(ln_b)],
        out_specs=pl.BlockSpec((tm, c), lambda i: (i, 0)),
        out_shape=jax.ShapeDtypeStruct((m, c), BF16),
        scratch_shapes=[pltpu.VMEM((GLU_HALO + tm, c), F32), pltpu.VMEM((tm, c), F32)],
        compiler_params=_cparams(("arbitrary",)))(h, w_in, w_in, prev, cw, cb, ln_g, ln_b)


def _conv_state(prev, u, k1):
    l = u.shape[1]
    if l >= k1:
        return u[:, l - k1:]
    return jnp.concatenate([prev[:, l:], u], axis=1)


def _pad_prev(prev, halo):
    return jnp.pad(prev, ((0, 0), (halo - prev.shape[1], 0), (0, 0)))


def _t5_bucket(rel):
    n = jnp.maximum(rel, 0)
    max_exact = NUM_BUCKETS // 2
    nf = jnp.maximum(n, 1).astype(F32)
    large = max_exact + (jnp.log(nf / max_exact) / math.log(MAX_DISTANCE / max_exact)
                         * (NUM_BUCKETS - max_exact)).astype(jnp.int32)
    large = jnp.minimum(large, NUM_BUCKETS - 1)
    return jnp.where(n < max_exact, n, large)


def _bias_of_rel(t5_bias, rel):
    bucket = _t5_bucket(rel)
    table = t5_bias.astype(F32)
    expand = (slice(None),) + (None,) * rel.ndim
    bias = jnp.zeros((table.shape[1],) + rel.shape, F32)
    for b in range(NUM_BUCKETS):
        bias = jnp.where((bucket == b)[None], table[b][expand], bias)
    return jnp.where((rel >= 0)[None], bias, NEG_INF)


def _lambda_full(lq1, lk1, lq2, lk2, li, lam_init):
    a = jnp.sum(lq1[li:li + 1, :] * lk1[li:li + 1, :], axis=-1, keepdims=True)
    b = jnp.sum(lq2[li:li + 1, :] * lk2[li:li + 1, :], axis=-1, keepdims=True)
    return jnp.exp(a) - jnp.exp(b) + lam_init


def _softmax_step(s, v, m_sc, l_sc, acc_sc):
    m_prev = m_sc[...]
    m_new = jnp.maximum(m_prev, jnp.max(s, axis=-1, keepdims=True))
    alpha = jnp.exp(m_prev - m_new)
    p = jnp.exp(s - m_new)
    l_sc[...] = alpha * l_sc[...] + jnp.sum(p, axis=-1, keepdims=True)
    acc_sc[...] = alpha * acc_sc[...] + jnp.dot(p.astype(BF16), v, preferred_element_type=F32)
    m_sc[...] = m_new


def _sub_rms(o, g, lam_init):
    y = o * lax.rsqrt(jnp.mean(o * o, axis=-1, keepdims=True) + EPS) * g
    return y * (1.0 - lam_init)


def _pattn_body(q_ref, k_ref, v_ref, bt_ref, lq1, lk1, lq2, lk2, g_ref, o_ref,
                vp_sc, m_sc, acc_sc, *, li, tq, nh, lam_init):
    i = pl.program_id(2)

    @pl.when(i == 0)
    def _():
        for a in range(nh):
            vp_sc[a, :, :KD] = v_ref[0, :, a * KD:(a + 1) * KD]
            vp_sc[a, :, KD:] = jnp.ones((vp_sc.shape[1], KD), BF16)

    m_sc[...] = jnp.full(m_sc.shape, NEG_INF, F32)
    acc_sc[...] = jnp.zeros(acc_sc.shape, F32)
    lane = lax.broadcasted_iota(jnp.int32, (tq, KD), 1)
    qqs = []
    for a in range(nh):
        q = q_ref[0, :, a * KD:(a + 1) * KD]
        zero = jnp.zeros_like(q)
        qqs.append(jnp.concatenate([jnp.where(lane < DIFF_DH, q, zero),
                                    jnp.where(lane >= DIFF_DH, q, zero)], axis=0))

    def block(off, width, bias_of_head):
        for a in range(nh):
            kb = k_ref[0, pl.ds(off, width), a * KD:(a + 1) * KD]
            vb = vp_sc[a, pl.ds(off, width), :]
            s = lax.dot_general(qqs[a], kb, (((1,), (1,)), ((), ())),
                                preferred_element_type=F32) + bias_of_head(a)
            m_prev = m_sc[a]
            m_new = jnp.maximum(m_prev, jnp.max(s, axis=-1, keepdims=True))
            alpha = jnp.exp(m_prev - m_new)
            p = jnp.exp(s - jnp.concatenate([m_new] * (width // LANES), axis=1))
            acc_sc[a] = (jnp.concatenate([alpha, alpha], axis=1) * acc_sc[a]
                         + jnp.dot(p.astype(BF16), vb, preferred_element_type=F32))
            m_sc[a] = m_new

    n_far = jnp.maximum(i - 1, 0)
    far_of_head = lambda a: bt_ref[a, 0, tq - 1:tq, 0:1]

    def far_body(j, carry):
        block(pl.multiple_of(j * 2 * tq, 2 * tq), 2 * tq, far_of_head)
        return carry

    lax.fori_loop(0, n_far // 2, far_body, 0)

    @pl.when(n_far % 2 == 1)
    def _():
        block(pl.multiple_of((n_far - 1) * tq, tq), tq, far_of_head)

    first = (i == 0).astype(jnp.int32)

    def near_of_head(a):
        b = bt_ref[a, first]
        return jnp.concatenate([b, b], axis=0)

    block(pl.multiple_of(n_far * tq, tq), 2 * tq, near_of_head)

    lam = _lambda_full(lq1, lk1, lq2, lk2, li, lam_init)
    g = g_ref[li:li + 1, :]
    for a in range(nh):
        acc = acc_sc[a]
        o = acc[:, :KD] / acc[:, KD:]
        o = o[:tq] - lam * o[tq:]
        o_ref[0, :, a * KD:(a + 1) * KD] = _sub_rms(o, g, lam_init).astype(o_ref.dtype)


def _prompt_attn(qkv, bt, lams, subln_g, *, li, n_heads, lam_init):
    bsz, l, _ = qkv.shape
    tq = bt.shape[2]
    nh = _pick(n_heads, (4, 2, 1))
    hb = n_heads // nh
    assert l % (2 * tq) == 0
    full = lambda a: pl.BlockSpec(a.shape, lambda b, hh, i: (0,) * a.ndim)
    body = functools.partial(_pattn_body, li=li, tq=tq, nh=nh, lam_init=lam_init)
    return pl.pallas_call(
        body, grid=(bsz, hb, l // tq),
        in_specs=[pl.BlockSpec((1, tq, nh * KD), lambda b, hh, i: (b, i, hh)),
                  pl.BlockSpec((1, l, nh * KD), lambda b, hh, i: (b, 0, hb + hh)),
                  pl.BlockSpec((1, l, nh * KD), lambda b, hh, i: (b, 0, 2 * hb + hh)),
                  pl.BlockSpec((nh, 2, tq, 2 * tq), lambda b, hh, i: (hh, 0, 0, 0))]
                 + [full(a) for a in lams] + [full(subln_g)],
        out_specs=pl.BlockSpec((1, tq, nh * KD), lambda b, hh, i: (b, i, hh)),
        out_shape=jax.ShapeDtypeStruct((bsz, l, n_heads * KD), BF16),
        scratch_shapes=[pltpu.VMEM((nh, l, 2 * KD), BF16), pltpu.VMEM((nh, 2 * tq, LANES), F32),
                        pltpu.VMEM((nh, 2 * tq, 2 * KD), F32)],
        compiler_params=_cparams(("parallel", "parallel", "arbitrary")))(
            qkv, qkv, qkv, bt, *lams, subln_g)


def _sattn_body(pt_ref, qt_ref, *refs, li, pp, n_heads, n_q, lam_init):
    krefs = refs[:pp]
    vrefs = refs[pp:2 * pp]
    kn_ref, vn_ref, bias_ref, lq1, lk1, lq2, lk2, g_ref, o_ref, m_sc, l_sc, acc_sc = refs[2 * pp:]
    jp = pl.program_id(1)
    last = pl.num_programs(1) - 1

    @pl.when(jp == 0)
    def _():
        m_sc[...] = jnp.full(m_sc.shape, NEG_INF, F32)
        l_sc[...] = jnp.zeros(l_sc.shape, F32)
        acc_sc[...] = jnp.zeros(acc_sc.shape, F32)

    def block(kbs, vbs, bias):
        s = jnp.stack([lax.dot_general(qt_ref[0, h], kbs[h], (((1,), (1,)), ((), ())),
                                       preferred_element_type=F32)
                       for h in range(n_heads)]) + bias
        m_prev = m_sc[...]
        m_new = jnp.maximum(m_prev, jnp.max(s, axis=-1, keepdims=True))
        alpha = jnp.exp(m_prev - m_new)
        p = jnp.exp(s - m_new)
        l_sc[...] = alpha * l_sc[...] + jnp.sum(p, axis=-1, keepdims=True)
        pb = p.astype(BF16)
        pv = jnp.stack([jnp.dot(pb[h], vbs[h], preferred_element_type=F32)
                        for h in range(n_heads)])
        acc_sc[...] = alpha * acc_sc[...] + pv
        m_sc[...] = m_new

    def head_rows(page_refs, h):
        rows_h = pl.ds(h, PAGE_SIZE, stride=n_heads)
        return jnp.concatenate([r[0, 0, rows_h, :] for r in page_refs], axis=0).astype(BF16)

    far = bias_ref[0]
    near = jnp.where(jp == last, bias_ref[1], far)
    block([head_rows(krefs, h) for h in range(n_heads)],
          [head_rows(vrefs, h) for h in range(n_heads)],
          jnp.concatenate([far] * (pp - 1) + [near], axis=-1))

    @pl.when(jp == last)
    def _():
        cols = [slice(h * KD, (h + 1) * KD) for h in range(n_heads)]
        block([kn_ref[0, :, c] for c in cols], [vn_ref[0, :, c] for c in cols], bias_ref[2])
        lam = _lambda_full(lq1, lk1, lq2, lk2, li, lam_init)
        g = g_ref[li:li + 1, :]
        o = acc_sc[...] / l_sc[...]
        o = _sub_rms(o[:, :n_q] - lam * o[:, n_q:], g, lam_init)
        for h in range(n_heads):
            o_ref[0, :, cols[h]] = o[h]


def _sample_attn(page_table, qt, cache_k, cache_v, k_new, v_new, bias3, lams, subln_g,
                 *, li, n_heads, n_q, lam_init):
    dbsz, n_pages = page_table.shape
    pp = _pick(n_pages, (PAGES_PER_STEP, 8, 4, 2, 1))
    w = n_heads * KD
    pt_flat = page_table.reshape(-1)

    def page_spec(g):
        return pl.BlockSpec((1, 1, PAGE_SIZE * n_heads, KD),
                            lambda b, j, pt, g=g: (li, pt[b * n_pages + j * pp + g], 0, 0))

    full = lambda a: pl.BlockSpec(a.shape, lambda b, j, pt: (0,) * a.ndim)
    per_b = pl.BlockSpec((1, LANES, w), lambda b, j, pt: (b, 0, 0))
    grid_spec = pltpu.PrefetchScalarGridSpec(
        num_scalar_prefetch=1, grid=(dbsz, n_pages // pp),
        in_specs=[pl.BlockSpec((1, n_heads, 2 * n_q, KD), lambda b, j, pt: (b, 0, 0, 0))]
                 + [page_spec(g) for g in range(pp)] * 2 + [per_b, per_b, full(bias3)]
                 + [full(a) for a in lams] + [full(subln_g)],
        out_specs=pl.BlockSpec((1, n_q, w), lambda b, j, pt: (b, 0, 0)),
        scratch_shapes=[pltpu.VMEM((n_heads, 2 * n_q, 1), F32),
                        pltpu.VMEM((n_heads, 2 * n_q, 1), F32),
                        pltpu.VMEM((n_heads, 2 * n_q, KD), F32)])
    body = functools.partial(_sattn_body, li=li, pp=pp, n_heads=n_heads, n_q=n_q,
                             lam_init=lam_init)
    return pl.pallas_call(
        body, grid_spec=grid_spec,
        out_shape=jax.ShapeDtypeStruct((dbsz, n_q, w), F32),
        compiler_params=_cparams(("parallel", "arbitrary")))(
            pt_flat, qt, *([cache_k] * pp), *([cache_v] * pp), k_new, v_new, bias3,
            *lams, subln_g)


def _block_queries(q, n_heads):
    dbsz, n_q, _ = q.shape
    qh = jnp.transpose(q.reshape(dbsz, n_q, n_heads, KD), (0, 2, 1, 3))
    col_map = (jnp.arange(KD) // DIFF_DH)[None, :] == jnp.arange(2)[:, None]
    t = qh[:, :, None] * col_map[None, None, :, None, :].astype(q.dtype)
    return t.reshape(dbsz, n_heads, 2 * n_q, KD)


def _sample_bias(t5_bias, n_heads, n_q):
    qi = jnp.arange(n_q, dtype=jnp.int32)
    key = jnp.arange(LANES, dtype=jnp.int32)

    def tile(rel):
        b = _bias_of_rel(t5_bias, rel)
        return jnp.concatenate([b, b], axis=1)

    far = tile(jnp.full((n_q, LANES), MAX_DISTANCE, jnp.int32))
    lastp = tile(PAGE_SIZE + qi[:, None] - key[None, :])
    new = tile(jnp.where(key[None, :] < n_q, qi[:, None] - key[None, :], -1))
    return jnp.stack([far, lastp, new])


def _ssd_body(xbc_ref, z_ref, dt_ref, h0_ref, dtb_ref, alog_ref, dx_ref, gn_ref,
              y_ref, hout_ref, hs, ysc, *, n_heads, valid_len):
    c = pl.program_id(1)
    t = SSD_CHUNK
    di = n_heads * SSD_P
    e_per_g = n_heads // SSD_G

    @pl.when(c == 0)
    def _():
        hs[...] = h0_ref[0]

    xbc = xbc_ref[0]
    xs = xbc[:, :di]
    row = lax.broadcasted_iota(jnp.int32, (t, LANES), 0)
    lane = lax.broadcasted_iota(jnp.int32, (t, LANES), 1)
    raw = dt_ref[0] + dtb_ref[...]
    dt = jnp.maximum(raw, 0.0) + jnp.log1p(jnp.exp(-jnp.abs(raw)))
    dt = jnp.where((row + c * t < valid_len) & (lane < n_heads), dt, 0.0)
    da = dt * (-jnp.exp(alog_ref[...]))
    tri = (lax.broadcasted_iota(jnp.int32, (t, t), 0)
           >= lax.broadcasted_iota(jnp.int32, (t, t), 1))
    cum = jnp.dot(tri.astype(F32), da, precision=lax.Precision.HIGHEST,
                  preferred_element_type=F32)
    cum_t = cum.T
    dt_t = dt.T
    cum_last = cum[t - 1:t, :]
    w_end = dt * jnp.exp(cum_last - cum)
    e_cum = jnp.exp(cum)
    e_last = jnp.exp(cum_last)
    xs_bf = xs.astype(BF16)
    xs_t = xs.T.astype(BF16)

    for g in range(SSD_G):
        b_g = xbc[:, di + g * SSD_N:di + (g + 1) * SSD_N]
        c_g = xbc[:, di + (SSD_G + g) * SSD_N:di + (SSD_G + g + 1) * SSD_N]
        cb = lax.dot_general(c_g.astype(BF16), b_g.astype(BF16), (((1,), (1,)), ((), ())),
                             preferred_element_type=F32)
        for e in range(e_per_g):
            h = g * e_per_g + e
            seg = cum[:, h:h + 1] - cum_t[h:h + 1, :]
            decay = jnp.where(tri, jnp.exp(jnp.where(tri, seg, 0.0)), 0.0)
            mat = (cb * decay * dt_t[h:h + 1, :]).astype(BF16)
            x_e = xs_bf[:, h * SSD_P:(h + 1) * SSD_P]
            y_d = jnp.dot(mat, x_e, preferred_element_type=F32)
            h_e = hs[h]
            c_w = (c_g * e_cum[:, h:h + 1]).astype(BF16)
            y_o = lax.dot_general(c_w, h_e.astype(BF16), (((1,), (1,)), ((), ())),
                                  preferred_element_type=F32)
            b_w = (b_g * w_end[:, h:h + 1]).astype(BF16)
            st = jnp.dot(xs_t[h * SSD_P:(h + 1) * SSD_P, :], b_w,
                         preferred_element_type=F32)
            hs[h] = h_e * e_last[:, h:h + 1] + st
            ysc[:, h * SSD_P:(h + 1) * SSD_P] = y_d + y_o

    y = ysc[...] + xs * dx_ref[...]
    z = z_ref[0]
    y = y * (z * jax.nn.sigmoid(z))
    gw = di // SSD_G
    for g in range(SSD_G):
        seg = y[:, g * gw:(g + 1) * gw]
        seg = seg * lax.rsqrt(jnp.mean(seg * seg, axis=-1, keepdims=True) + EPS)
        y_ref[0, :, g * gw:(g + 1) * gw] = (seg * gn_ref[:, g * gw:(g + 1) * gw]).astype(y_ref.dtype)

    @pl.when(c == pl.num_programs(1) - 1)
    def _():
        hout_ref[0] = hs[...]


def _ssd(xbc_act, zx, dt_raw, h0, dtb, alog, dx, gn, *, n_heads, valid_len):
    bsz, l, xbc_w = xbc_act.shape
    di = n_heads * SSD_P
    t = SSD_CHUNK
    assert l % t == 0
    full = lambda a: pl.BlockSpec(a.shape, lambda b, c: (0,) * a.ndim)
    body = functools.partial(_ssd_body, n_heads=n_heads, valid_len=valid_len)
    return pl.pallas_call(
        body, grid=(bsz, l // t),
        in_specs=[pl.BlockSpec((1, t, xbc_w), lambda b, c: (b, c, 0)),
                  pl.BlockSpec((1, t, di), lambda b, c: (b, c, 0)),
                  pl.BlockSpec((1, t, LANES), lambda b, c: (b, c, 0)),
                  pl.BlockSpec((1, n_heads, SSD_P, SSD_N), lambda b, c: (b, 0, 0, 0)),
                  full(dtb), full(alog), full(dx), full(gn)],
        out_specs=[pl.BlockSpec((1, t, di), lambda b, c: (b, c, 0)),
                   pl.BlockSpec((1, n_heads, SSD_P, SSD_N), lambda b, c: (b, 0, 0, 0))],
        out_shape=[jax.ShapeDtypeStruct((bsz, l, di), BF16),
                   jax.ShapeDtypeStruct((bsz, n_heads, SSD_P, SSD_N), F32)],
        scratch_shapes=[pltpu.VMEM((n_heads, SSD_P, SSD_N), F32), pltpu.VMEM((t, di), F32)],
        compiler_params=_cparams(("parallel", "arbitrary")))(
            xbc_act, zx, dt_raw, h0, dtb, alog, dx, gn)


def _epi_glu(accs, tiles, rows, j):
    a, b = accs
    return (a * jax.nn.sigmoid(b),)


def _epi_id(accs, tiles, rows, j):
    return (accs[0],)


def _epi_sigmoid(accs, tiles, rows, j):
    return (jax.nn.sigmoid(accs[0]),)


def _epi_merge(accs, tiles, rows, j):
    ya, yb, yc = accs
    ga, gb, gc = (t.astype(F32) for t in tiles)
    return (ga * ya + gb * yb + gc * yc,)


def _epi_resid(accs, tiles, rows, j):
    return (tiles[0] + rows[0] * accs[0],)


def _trunk(x, mod, prev_a, prev_c, prev_h, prev_f, wts, p, dims, *, paged=None):
    bsz, l, d = x.shape
    m = bsz * l
    depth, n_heads, conv_c, ssd_h, d_ff = (dims[k] for k in
                                           ("depth", "n_heads", "conv_c", "ssd_h", "d_ff"))
    att_w = n_heads * KD
    ssd_di = ssd_h * SSD_P
    ssd_xbc = ssd_di + 2 * SSD_G * SSD_N
    conv_k = p["conv_a_w"].shape[1]
    ssd_k = p["ssd_conv_w"].shape[1]
    ffn_k = p["ffn_conv_w"].shape[1]
    prompt = paged is None
    tm = _pick(l, (512, 256, 128)) if prompt else m
    tpb = l // tm if prompt else 1
    rpb = l if prompt else None
    t_seq = _pick(l, (128,)) if prompt else l
    lams = (p["lambda_q1"], p["lambda_k1"], p["lambda_q2"], p["lambda_k2"])
    scale = DIFF_DH ** -0.5

    xf = x.reshape(m, d)
    out_a, out_c, out_h, out_f = [], [], [], []
    kst = jnp.zeros((depth, m, att_w), F32)
    vst = jnp.zeros((depth, m, att_w), F32)
    for li in range(depth):
        w = wts
        md = mod[li]
        lam_init = 0.8 - 0.6 * math.exp(-0.3 * li)
        h = _norm_mod(xf, p["norm1_g"][li][None], md, 1, 0, tm=tm, tpb=tpb)

        c_qkv = 2 * conv_c
        c_zx = c_qkv + 3 * att_w
        zx_w = ssd_di + ssd_xbc
        glu_w = [(w["in"], li, 0, 0), (w["in"], li, conv_c, 0)]
        qkv_bf, kst, vst = _qkv(h, w["in"], c_qkv, kst, vst, li=li, scale=scale,
                                rows_per_batch=rpb)
        (zx,) = _mm([h], [(w["in"], li, c_zx, 0)], _epi_id, [F32], zx_w, rows_per_batch=rpb)
        (dt_raw,) = _mm([h], [(w["dt"], li, 0, 0)], _epi_id, [F32], LANES, rows_per_batch=rpb)
        (gates,) = _mm([h], [(w["gates"], li, 0, 0)], _epi_sigmoid, [BF16], 3 * d,
                       rows_per_batch=rpb)

        ln_rows = (p["ln_a_g"][li][None], p["ln_a_b"][li][None])
        if prompt and l >= GLU_HALO:
            a_act = _gluconv(h, w["in"], li, _pad_prev(prev_a[li], GLU_HALO), p["conv_a_w"][li],
                             p["conv_a_b"][li][None], *ln_rows, c=conv_c, rows_per_batch=rpb)
            h_tail = h.reshape(bsz, l, d)[:, l - GLU_HALO:].reshape(bsz * GLU_HALO, d)
            (u_tail,) = _mm([h_tail], glu_w, _epi_glu, [F32], conv_c)
            out_a.append(u_tail.reshape(bsz, GLU_HALO, conv_c)[:, GLU_HALO - (conv_k - 1):])
        else:
            (u,) = _mm([h], glu_w, _epi_glu, [F32], conv_c, rows_per_batch=rpb)
            u3 = u.reshape(bsz, l, conv_c)
            out_a.append(_conv_state(prev_a[li], u3, conv_k - 1))
            a_act = _dwconv(u3, 0, _pad_prev(prev_a[li], GLU_HALO), p["conv_a_w"][li],
                            p["conv_a_b"][li][None], _epi_conv_a, BF16, c=conv_c, t=t_seq,
                            tc=conv_c, halo=GLU_HALO, prows=ln_rows)

        if prompt:
            o = _prompt_attn(qkv_bf.reshape(bsz, l, 3 * att_w), dims["bias_tiles"], lams,
                             p["subln_g"], li=li, n_heads=n_heads, lam_init=lam_init)
        else:
            qb = qkv_bf.reshape(bsz, l, 3 * att_w)
            qt = _block_queries(qb[..., :att_w], n_heads)
            pad = ((0, 0), (0, LANES - l), (0, 0))
            k_new = jnp.pad(qb[..., att_w:2 * att_w], pad)
            v_new = jnp.pad(qb[..., 2 * att_w:], pad)
            o = _sample_attn(paged["page_table"], qt, paged["cache_k"], paged["cache_v"],
                             k_new, v_new, dims["bias3"], lams, p["subln_g"], li=li,
                             n_heads=n_heads, n_q=l, lam_init=lam_init).astype(BF16)
        o = o.reshape(m, att_w)

        zx3 = zx.reshape(bsz, l, zx_w)
        dt3 = dt_raw.reshape(bsz, l, LANES)
        xbc_raw = zx3[..., ssd_di:]
        out_c.append(_conv_state(prev_c[li], xbc_raw, ssd_k - 1))
        tc = _pick(math.gcd(ssd_di, ssd_xbc), (512, 256, 128))
        xbc_act = _dwconv(zx3, ssd_di // tc, _pad_prev(prev_c[li], 8), p["ssd_conv_w"][li],
                          p["ssd_conv_b"][li][None], _epi_silu, F32, c=ssd_xbc,
                          t=_pick(l, (512, 256, 128)) if prompt else l, tc=tc, halo=8)
        if prompt:
            xbc_in, zx_in, dt_in = xbc_act, zx3, dt3
        else:
            pad = ((0, 0), (0, SSD_CHUNK - l), (0, 0))
            xbc_in, zx_in, dt_in = jnp.pad(xbc_act, pad), jnp.pad(zx3, pad), jnp.pad(dt3, pad)
        lane_pad = lambda a: jnp.pad(a, (0, LANES - a.shape[0]))[None]
        y_c, h_t = _ssd(xbc_in, zx_in, dt_in, prev_h[li],
                        lane_pad(p["dt_bias"][li]), lane_pad(p["a_log"][li]),
                        jnp.repeat(p["d_skip"][li], SSD_P)[None], p["ssd_norm_g"][li][None],
                        n_heads=ssd_h, valid_len=l)
        out_h.append(h_t)
        y_c = y_c[:, :l].reshape(m, ssd_di)

        (merged,) = _mm([a_act.reshape(m, conv_c), o, y_c],
                        [(w["proj_a"], li, 0, 0), (w["proj_b"], li, 0, 1), (w["proj_c"], li, 0, 2)],
                        _epi_merge, [BF16], d, tiles=[(gates, 0), (gates, d), (gates, 2 * d)],
                        rows_per_batch=rpb)
        (xf,) = _mm([merged], [(w["out"], li, 0, 0)], _epi_resid, [F32], d, tiles=[(xf, 0)],
                    rows=[(md, 2 * d)], rows_per_batch=rpb)

        h2 = _norm_mod(xf, p["norm2_g"][li][None], md, 4, 3, tm=tm, tpb=tpb)
        if prompt:
            act = _upconv(h2, w["up"], li, _pad_prev(prev_f[li], UP_HALO), p["ffn_conv_w"][li],
                          p["ffn_conv_b"][li][None], d_ff=d_ff, rows_per_batch=rpb)
            assert l >= UP_HALO
            h2_tail = h2.reshape(bsz, l, d)[:, l - UP_HALO:].reshape(bsz * UP_HALO, d)
            (fa_tail,) = _mm([h2_tail], [(w["up"], li, 0, 0)], _epi_id, [F32], d_ff)
            out_f.append(fa_tail.reshape(bsz, UP_HALO, d_ff)[:, UP_HALO - (ffn_k - 1):])
        else:
            (up,) = _mm([h2], [(w["up"], li, 0, 0)], _epi_id, [F32], 2 * d_ff)
            up3 = up.reshape(bsz, l, 2 * d_ff)
            out_f.append(_conv_state(prev_f[li], up3[..., :d_ff], ffn_k - 1))
            tc = d_ff
            act = _dwconv(up3, 0, _pad_prev(prev_f[li], 8), p["ffn_conv_w"][li],
                          p["ffn_conv_b"][li][None], _epi_ffn, BF16, c=d_ff, t=t_seq, tc=tc,
                          halo=8, xtiles=[(up3, d_ff // tc)])
        (xf,) = _mm([act.reshape(m, d_ff)], [(w["down"], li, 0, 0)], _epi_resid, [F32], d,
                    tiles=[(xf, 0)], rows=[(md, 5 * d)], rows_per_batch=rpb)

    y_out = _rms(xf, p["final_g"][None], tm=tm).reshape(bsz, l, d)
    kv_shape = (depth, bsz, l, n_heads, KD)
    return (y_out, kst.reshape(kv_shape), vst.reshape(kv_shape), jnp.stack(out_a), jnp.stack(out_c),
            jnp.stack(out_h), jnp.stack(out_f))


def _cast_body(x_ref, o_ref):
    o_ref[...] = x_ref[...].astype(o_ref.dtype)


def _to_bf16(w):
    depth, k, n = w.shape
    tk = _pick(k, (256, 128))
    spec = pl.BlockSpec((1, tk, n), lambda l, i: (l, i, 0))
    return pl.pallas_call(
        _cast_body, grid=(depth, k // tk), in_specs=[spec], out_specs=spec,
        out_shape=jax.ShapeDtypeStruct(w.shape, BF16),
        compiler_params=_cparams(("parallel", "parallel")))(w)


def _bf16_weights(p, dims):
    conv_c, att_w = dims["conv_c"], dims["n_heads"] * KD
    ssd_di = dims["ssd_h"] * SSD_P
    ssd_xbc = ssd_di + 2 * SSD_G * SSD_N
    c_dt = 2 * conv_c + 3 * att_w + ssd_di + ssd_xbc
    c_gates = c_dt + dims["ssd_h"]
    w_in = _to_bf16(p["w_in"])
    dt_cols = jnp.pad(w_in[:, :, c_dt:c_gates], ((0, 0), (0, 0), (0, LANES - dims["ssd_h"])))
    bf = lambda a: a.astype(BF16)
    return {"in": w_in, "dt": dt_cols, "gates": w_in[:, :, c_gates:],
            "proj_a": bf(p["w_proj_a"]), "proj_b": bf(p["w_proj_b"]), "proj_c": bf(p["w_proj_c"]),
            "out": bf(p["w_out"]), "up": bf(p["ffn_up"]), "down": bf(p["ffn_down"])}


def kernel(x_prompt, x_sample, cache_k, cache_v, page_table, state_conv_a, state_ssd_conv,
           state_ssm, state_ffn_conv, c_prompt, c_sample, w_ada, b_ada, norm1_g, w_in,
           conv_a_w, conv_a_b, ln_a_g, ln_a_b, w_proj_a, t5_bias, lambda_q1, lambda_k1,
           lambda_q2, lambda_k2, subln_g, w_proj_b, ssd_conv_w, ssd_conv_b, dt_bias, a_log,
           d_skip, ssd_norm_g, w_proj_c, w_out, norm2_g, ffn_up, ffn_conv_w, ffn_conv_b,
           ffn_down, final_g):
    p = dict(w_ada=w_ada, b_ada=b_ada, norm1_g=norm1_g, w_in=w_in, conv_a_w=conv_a_w,
             conv_a_b=conv_a_b, ln_a_g=ln_a_g, ln_a_b=ln_a_b, w_proj_a=w_proj_a, t5_bias=t5_bias,
             lambda_q1=lambda_q1, lambda_k1=lambda_k1, lambda_q2=lambda_q2, lambda_k2=lambda_k2,
             subln_g=subln_g, w_proj_b=w_proj_b, ssd_conv_w=ssd_conv_w, ssd_conv_b=ssd_conv_b,
             dt_bias=dt_bias, a_log=a_log, d_skip=d_skip, ssd_norm_g=ssd_norm_g, w_proj_c=w_proj_c,
             w_out=w_out, norm2_g=norm2_g, ffn_up=ffn_up, ffn_conv_w=ffn_conv_w,
             ffn_conv_b=ffn_conv_b, ffn_down=ffn_down, final_g=final_g)
    bp, lp, d = x_prompt.shape
    bs, ls, _ = x_sample.shape
    depth = w_ada.shape[0]
    n_heads = cache_k.shape[3]
    n_pages = page_table.shape[1]
    dims = dict(depth=depth, n_heads=n_heads, conv_c=conv_a_w.shape[2], ssd_h=dt_bias.shape[1],
                d_ff=ffn_conv_w.shape[2])
    f32 = x_prompt.dtype

    tq = _pick(lp, (ATTN_TQ, 128))
    assert tq >= MAX_DISTANCE
    qi = jnp.arange(tq, dtype=jnp.int32)
    rel0 = qi[:, None] - qi[None, :]
    b_diag, b_prev = _bias_of_rel(t5_bias, rel0), _bias_of_rel(t5_bias, rel0 + tq)
    dims["bias_tiles"] = jnp.stack(
        [jnp.concatenate([b_prev, b_diag], axis=-1),
         jnp.concatenate([b_diag, jnp.full_like(b_diag, NEG_INF)], axis=-1)], axis=1)
    dims["bias3"] = _sample_bias(t5_bias, n_heads, ls)

    rows = -(-(bp + bs) // 8) * 8
    c_all = jnp.pad(jnp.concatenate([c_prompt, c_sample], axis=0), ((0, rows - bp - bs), (0, 0)))
    mod = _ada(c_all, w_ada, b_ada)
    mod_p = mod[:, :bp, None, :]
    mod_s = jnp.repeat(mod[:, bp:bp + bs], ls, axis=1)[:, None]

    wts = _bf16_weights(p, dims)
    zero = lambda *s: jnp.zeros((depth, bp) + s, f32)
    res_p = _trunk(x_prompt, mod_p, zero(conv_a_w.shape[1] - 1, dims["conv_c"]),
                   zero(ssd_conv_w.shape[1] - 1, ssd_conv_w.shape[2]),
                   zero(dims["ssd_h"], SSD_P, SSD_N), zero(ffn_conv_w.shape[1] - 1, dims["d_ff"]),
                   wts, p, dims)
    paged = dict(page_table=page_table,
                 cache_k=cache_k.reshape(depth, -1, PAGE_SIZE * n_heads, KD),
                 cache_v=cache_v.reshape(depth, -1, PAGE_SIZE * n_heads, KD))
    res_s = _trunk(x_sample, mod_s, state_conv_a, state_ssd_conv, state_ssm, state_ffn_conv,
                   wts, p, dims, paged=paged)
    y_p, k_p, v_p, a_p, c_p, h_p, f_p = res_p
    y_s, k_s, v_s, a_s, c_s, h_s, f_s = res_s
    return (y_p, y_s, k_p, v_p, k_s, v_s, a_p, a_s, c_p, c_s, h_p, h_s, f_p, f_s)
```

```python
import functools
import math

import jax
import jax.numpy as jnp
from jax import lax
from jax.experimental import pallas as pl
from jax.experimental.pallas import tpu as pltpu

F32 = jnp.float32
BF16 = jnp.bfloat16

EPS = 1e-6
LN_EPS = 1e-5
NEG_INF = -1e30
DIFF_DH = 64
KD = 2 * DIFF_DH
NUM_BUCKETS = 32
MAX_DISTANCE = 128
SSD_P = 64
SSD_N = 128
SSD_G = 2
SSD_CHUNK = 128
PAGE_SIZE = 128
LANES = 128
VMEM_LIMIT = 56 * 1024 * 1024
VMEM_BLOCK_BUDGET = 46 * 1024 * 1024
MIN_TN = 512
MM_ROW_CHUNK = 512

ATTN_TQ = 256
PAGES_PER_STEP = 16


def _cparams(sem):
    return pltpu.CompilerParams(dimension_semantics=sem, vmem_limit_bytes=VMEM_LIMIT)


def _pick(n, cands):
    for c in cands:
        if n % c == 0:
            return c
    return n


def _fit_tiles(m_cands, n_cands, block_bytes):
    wide = [c for c in n_cands if c >= MIN_TN] or list(n_cands)
    for cols in (wide, list(n_cands)):
        for tm in m_cands:
            for tn in cols:
                if block_bytes(tm, tn) <= VMEM_BLOCK_BUDGET:
                    return tm, tn
    return m_cands[-1], n_cands[-1]


def _row_cands(rows_per_batch, whole=None):
    if whole is not None:
        return [whole]
    return [c for c in (2048, 1024, 512, 256, 128) if rows_per_batch % c == 0] or [rows_per_batch]


def _col_cands(n, offsets=()):
    cands = [c for c in (1024, 768, 512, 256, 896, 640, 384, 128)
             if n % c == 0 and all(o % c == 0 for o in offsets)]
    return cands or [n]


def _mm_body(*refs, nl, lhs_of, nt, nr, epi, rm):
    nw = len(lhs_of)
    lrefs = refs[:nl]
    wrefs = refs[nl:nl + nw]
    trefs = refs[nl + nw:nl + nw + nt]
    rrefs = refs[nl + nw + nt:nl + nw + nt + nr]
    orefs = refs[nl + nw + nt + nr:]
    tm = orefs[0].shape[0]
    j = pl.program_id(1)
    for m0 in range(0, tm, rm):
        rs = slice(m0, m0 + rm)
        accs = [jnp.dot(lrefs[li][rs, :], w[...], preferred_element_type=F32)
                for w, li in zip(wrefs, lhs_of)]
        rows = [r[0] if r.shape[1] == 1 else r[0, rs, :] for r in rrefs]
        res = epi(accs, [t[rs, :] for t in trefs], rows, j)
        for o, r in zip(orefs, res):
            o[rs, :] = r.astype(o.dtype)


def _mm(lhs, rhs, epi, out_dtypes, n, *, tiles=(), rows=(), rows_per_batch=None):
    m = lhs[0].shape[0]
    offsets = [off for _, _, off, _ in rhs] + [off for _, off in tiles] + [off for _, off in rows]

    def block_bytes(tm, tn):
        b = sum(tm * a.shape[1] * a.dtype.itemsize for a in lhs)
        b += sum(a.shape[1] * tn * a.dtype.itemsize for a, _, _, _ in rhs)
        b += sum(tm * tn * a.dtype.itemsize for a, _ in tiles)
        b += sum(tm * tn * jnp.dtype(d).itemsize for d in out_dtypes)
        return 2 * b + len(rhs) * min(tm, MM_ROW_CHUNK) * tn * 4

    tm, tn = _fit_tiles(_row_cands(rows_per_batch, None if rows_per_batch else m),
                        _col_cands(n, offsets), block_bytes)
    tpb = rows_per_batch // tm if rows_per_batch else 1
    assert m % tm == 0 and n % tn == 0
    in_specs, args = [], []
    for a in lhs:
        in_specs.append(pl.BlockSpec((tm, a.shape[1]), lambda i, j: (i, 0)))
        args.append(a)
    for a, li, off, _ in rhs:
        in_specs.append(pl.BlockSpec((None, a.shape[1], tn),
                                     lambda i, j, li=li, o=off // tn: (li, 0, j + o)))
        args.append(a)
    for a, off in tiles:
        in_specs.append(pl.BlockSpec((tm, tn), lambda i, j, o=off // tn: (i, j + o)))
        args.append(a)
    for a, off in rows:
        assert a.shape[1] == 1 or (a.shape[0] == 1 and a.shape[1] == tm == m)
        in_specs.append(pl.BlockSpec((1, a.shape[1], tn),
                                     lambda i, j, o=off // tn: (i // tpb, 0, j + o)))
        args.append(a)
    out_specs = [pl.BlockSpec((tm, tn), lambda i, j: (i, j)) for _ in out_dtypes]
    out_shape = [jax.ShapeDtypeStruct((m, n), d) for d in out_dtypes]
    body = functools.partial(_mm_body, nl=len(lhs), lhs_of=tuple(k for _, _, _, k in rhs),
                             nt=len(tiles), nr=len(rows), epi=epi, rm=min(tm, MM_ROW_CHUNK))
    return pl.pallas_call(
        body, grid=(m // tm, n // tn), in_specs=in_specs, out_specs=out_specs,
        out_shape=out_shape, compiler_params=_cparams(("parallel", "arbitrary")))(*args)


def _qkv_body(h_ref, w_ref, kin_ref, vin_ref, qkv_ref, kst_ref, vst_ref, *, nq, scale):
    del kin_ref, vin_ref
    j = pl.program_id(1)
    acc = jnp.dot(h_ref[...], w_ref[...], preferred_element_type=F32)
    qkv_ref[...] = (acc * jnp.where(j < nq, scale, 1.0)).astype(qkv_ref.dtype)

    @pl.when((j >= nq) & (j < 2 * nq))
    def _():
        kst_ref[0] = acc

    @pl.when(j >= 2 * nq)
    def _():
        vst_ref[0] = acc


def _qkv(h, w, col0, kst, vst, *, li, scale, rows_per_batch=None):
    m, d = h.shape
    wdt = kst.shape[2]
    block_bytes = lambda tm, tn: 2 * (tm * d * 2 + d * tn * 2 + tm * tn * (2 + 4 + 4)) + tm * tn * 4
    tm, tn = _fit_tiles(_row_cands(rows_per_batch, None if rows_per_batch else m),
                        _col_cands(wdt, (col0,)), block_bytes)
    nq = wdt // tn
    j0 = col0 // tn
    assert m % tm == 0 and wdt % tn == 0 and col0 % tn == 0
    k_idx = lambda i, j: (li, i, jnp.clip(j - nq, 0, nq - 1))
    v_idx = lambda i, j: (li, i, jnp.clip(j - 2 * nq, 0, nq - 1))
    body = functools.partial(_qkv_body, nq=nq, scale=scale)
    return pl.pallas_call(
        body, grid=(m // tm, 3 * nq),
        in_specs=[pl.BlockSpec((tm, d), lambda i, j: (i, 0)),
                  pl.BlockSpec((None, d, tn), lambda i, j: (li, 0, j0 + j)),
                  pl.BlockSpec(memory_space=pl.ANY), pl.BlockSpec(memory_space=pl.ANY)],
        out_specs=[pl.BlockSpec((tm, tn), lambda i, j: (i, j)),
                   pl.BlockSpec((1, tm, tn), k_idx), pl.BlockSpec((1, tm, tn), v_idx)],
        out_shape=[jax.ShapeDtypeStruct((m, 3 * wdt), BF16),
                   jax.ShapeDtypeStruct(kst.shape, F32), jax.ShapeDtypeStruct(vst.shape, F32)],
        input_output_aliases={2: 1, 3: 2},
        compiler_params=_cparams(("arbitrary", "arbitrary")))(h, w, kst, vst)


def _ada_body(c_ref, w_ref, b_ref, o_ref):
    c = c_ref[...]
    s = (c * jax.nn.sigmoid(c)).astype(BF16)
    o_ref[0] = jnp.dot(s, w_ref[0].astype(BF16), preferred_element_type=F32) + b_ref[0]


def _ada(c_all, w_ada, b_ada):
    depth, d, n6 = w_ada.shape
    rows = c_all.shape[0]
    tn = _pick(n6, (1024, 512, 256, 128))
    return pl.pallas_call(
        _ada_body, grid=(depth, n6 // tn),
        in_specs=[pl.BlockSpec((rows, d), lambda l, j: (0, 0)),
                  pl.BlockSpec((1, d, tn), lambda l, j: (l, 0, j)),
                  pl.BlockSpec((1, 1, tn), lambda l, j: (l, 0, j))],
        out_specs=pl.BlockSpec((1, rows, tn), lambda l, j: (l, 0, j)),
        out_shape=jax.ShapeDtypeStruct((depth, rows, n6), F32),
        compiler_params=_cparams(("arbitrary", "arbitrary")))(c_all, w_ada, b_ada[:, None, :])


NORM_ROW_CHUNK = 32


def _norm_mod_body(x_ref, g_ref, sc_ref, sh_ref, o_ref):
    tm = x_ref.shape[0]
    rc = min(tm, NORM_ROW_CHUNK)
    g = g_ref[...]
    for r0 in range(0, tm, rc):
        rs = slice(r0, r0 + rc)
        x = x_ref[rs, :]
        y = x * lax.rsqrt(jnp.mean(x * x, axis=-1, keepdims=True) + EPS) * g
        sc, sh = (r[0] if r.shape[1] == 1 else r[0, rs, :] for r in (sc_ref, sh_ref))
        o_ref[rs, :] = (y * (1.0 + sc) + sh).astype(o_ref.dtype)


def _norm_mod(x, g, mod, sc_blk, sh_blk, *, tm, tpb):
    m, d = x.shape
    r = mod.shape[1]
    return pl.pallas_call(
        _norm_mod_body, grid=(m // tm,),
        in_specs=[pl.BlockSpec((tm, d), lambda i: (i, 0)),
                  pl.BlockSpec((1, d), lambda i: (0, 0)),
                  pl.BlockSpec((1, r, d), lambda i: (i // tpb, 0, sc_blk)),
                  pl.BlockSpec((1, r, d), lambda i: (i // tpb, 0, sh_blk))],
        out_specs=pl.BlockSpec((tm, d), lambda i: (i, 0)),
        out_shape=jax.ShapeDtypeStruct((m, d), BF16),
        compiler_params=_cparams(("parallel",)))(x, g, mod, mod)


def _rms_body(x_ref, g_ref, o_ref):
    tm = x_ref.shape[0]
    rc = min(tm, NORM_ROW_CHUNK)
    g = g_ref[...]
    for r0 in range(0, tm, rc):
        x = x_ref[r0:r0 + rc, :]
        o_ref[r0:r0 + rc, :] = x * lax.rsqrt(jnp.mean(x * x, axis=-1, keepdims=True) + EPS) * g


def _rms(x, g, *, tm):
    m, d = x.shape
    return pl.pallas_call(
        _rms_body, grid=(m // tm,),
        in_specs=[pl.BlockSpec((tm, d), lambda i: (i, 0)),
                  pl.BlockSpec((1, d), lambda i: (0, 0))],
        out_specs=pl.BlockSpec((tm, d), lambda i: (i, 0)),
        out_shape=jax.ShapeDtypeStruct((m, d), F32),
        compiler_params=_cparams(("parallel",)))(x, g)


def _dwconv_body(*refs, kc, halo, t, tc, n_seq_tiles, nx, npr, epi):
    u_ref, prev_ref, w_ref, b_ref = refs[:4]
    xrefs = refs[4:4 + nx]
    prefs = refs[4 + nx:4 + nx + npr]
    o_ref = refs[4 + nx + npr]
    win, cv = refs[4 + nx + npr + 1:]

    @pl.when(pl.program_id(2) == 0)
    def _():
        win[0:halo, :] = prev_ref[0]

    win[halo:halo + t, :] = u_ref[0]
    base = halo - (kc - 1)
    rc = min(t, 64)
    cc = min(tc, 256)
    for r0 in range(0, t, rc):
        for c0 in range(0, tc, cc):
            acc = jnp.broadcast_to(b_ref[:, c0:c0 + cc], (rc, cc))
            wlen = rc + halo
            window = win[r0:r0 + wlen, c0:c0 + cc]
            for r in range(8):
                taps = [j for j in range(kc) if (base + j) % 8 == r]
                if not taps:
                    continue
                sh = window if r == 0 else pltpu.roll(window, wlen - r, 0)
                for j in taps:
                    a = (base + j) // 8
                    acc = acc + w_ref[j:j + 1, c0:c0 + cc] * sh[8 * a:8 * a + rc]
            cv[r0:r0 + rc, c0:c0 + cc] = acc
    o_ref[0] = epi(cv[...], [x[0] for x in xrefs], [p[...] for p in prefs]).astype(o_ref.dtype)
    if n_seq_tiles > 1:
        win[0:halo, :] = win[t:t + halo, :]


def _dwconv(u, u_blk, prev, w, b, epi, out_dtype, *, c, t, tc, halo, xtiles=(), prows=()):
    bsz, l, _ = u.shape
    kc = w.shape[0]
    assert l % t == 0 and c % tc == 0 and halo >= kc - 1 and (t >= halo or l == t)
    nlt = l // t
    in_specs = [pl.BlockSpec((1, t, tc), lambda bi, ci, li: (bi, li, ci + u_blk)),
                pl.BlockSpec((1, halo, tc), lambda bi, ci, li: (bi, 0, ci)),
                pl.BlockSpec((kc, tc), lambda bi, ci, li: (0, ci)),
                pl.BlockSpec((1, tc), lambda bi, ci, li: (0, ci))]
    args = [u, prev, w, b]
    for a, off in xtiles:
        in_specs.append(pl.BlockSpec((1, t, tc), lambda bi, ci, li, off=off: (bi, li, ci + off)))
        args.append(a)
    for a in prows:
        in_specs.append(pl.BlockSpec((1, tc), lambda bi, ci, li: (0, ci)))
        args.append(a)
    body = functools.partial(_dwconv_body, kc=kc, halo=halo, t=t, tc=tc, n_seq_tiles=nlt,
                             nx=len(xtiles), npr=len(prows), epi=epi)
    return pl.pallas_call(
        body, grid=(bsz, c // tc, nlt), in_specs=in_specs,
        out_specs=pl.BlockSpec((1, t, tc), lambda bi, ci, li: (bi, li, ci)),
        out_shape=jax.ShapeDtypeStruct((bsz, l, c), out_dtype),
        scratch_shapes=[pltpu.VMEM((halo + t, tc), F32), pltpu.VMEM((t, tc), F32)],
        compiler_params=_cparams(("parallel", "parallel", "arbitrary")))(*args)


def _epi_conv_a(cv, xs, ps):
    g, b = ps
    mu = jnp.mean(cv, axis=-1, keepdims=True)
    d = cv - mu
    var = jnp.mean(d * d, axis=-1, keepdims=True)
    y = d * lax.rsqrt(var + LN_EPS) * g + b
    return y * jax.nn.sigmoid(y)


def _epi_silu(cv, xs, ps):
    return cv * jax.nn.sigmoid(cv)


def _epi_ffn(cv, xs, ps):
    return jax.nn.gelu(cv) * xs[0]


UP_HALO = 8
UP_MAX_TM = 1024


def _upconv_body(h_ref, wa_ref, wb_ref, prev_ref, cw_ref, cb_ref, act_ref,
                 win, carry, *, tm, tn, tpb, kc):
    i = pl.program_id(0)
    j = pl.program_id(1)
    first = i % tpb == 0

    @pl.when(first)
    def _():
        win[0:UP_HALO, :] = prev_ref[0]

    @pl.when(jnp.logical_not(first))
    def _():
        win[0:UP_HALO, :] = carry[j]

    base = UP_HALO - (kc - 1)
    rm = min(tm, 512)
    rc = min(rm, 64)
    cc = min(tn, 256)
    for m0 in range(0, tm, rm):
        h = h_ref[m0:m0 + rm, :]
        fa = jnp.dot(h, wa_ref[...], preferred_element_type=F32)
        fb = jnp.dot(h, wb_ref[...], preferred_element_type=F32)
        win[UP_HALO + m0:UP_HALO + m0 + rm, :] = fa
        for r0 in range(0, rm, rc):
            for c0 in range(0, tn, cc):
                acc = jnp.broadcast_to(cb_ref[:, c0:c0 + cc], (rc, cc))
                for t in range(kc):
                    start = base + m0 + r0 + t
                    acc = acc + cw_ref[t:t + 1, c0:c0 + cc] * win[start:start + rc, c0:c0 + cc]
                act_ref[m0 + r0:m0 + r0 + rc, c0:c0 + cc] = (
                    jax.nn.gelu(acc) * fb[r0:r0 + rc, c0:c0 + cc]).astype(act_ref.dtype)
    carry[j] = win[tm:tm + UP_HALO, :]


def _upconv(h, w_up, li, prev, cw, cb, *, d_ff, rows_per_batch):
    m, d = h.shape
    kc = cw.shape[0]
    block_bytes = lambda tm, tn: (2 * (tm * d * 2 + 2 * d * tn * 2 + tm * tn * 2)
                                  + (tm + UP_HALO) * tn * 4 + 2 * min(tm, 512) * tn * 4)
    rows = [c for c in _row_cands(rows_per_batch) if c <= UP_MAX_TM] or [rows_per_batch]
    tm, tn = _fit_tiles(rows, _col_cands(d_ff), block_bytes)
    tpb = rows_per_batch // tm
    nj = d_ff // tn
    assert m % tm == 0 and d_ff % tn == 0 and kc - 1 <= UP_HALO <= tm
    body = functools.partial(_upconv_body, tm=tm, tn=tn, tpb=tpb, kc=kc)
    return pl.pallas_call(
        body, grid=(m // tm, nj),
        in_specs=[pl.BlockSpec((tm, d), lambda i, j: (i, 0)),
                  pl.BlockSpec((None, d, tn), lambda i, j: (li, 0, j)),
                  pl.BlockSpec((None, d, tn), lambda i, j: (li, 0, j + nj)),
                  pl.BlockSpec((1, UP_HALO, tn), lambda i, j: (i // tpb, 0, j)),
                  pl.BlockSpec((kc, tn), lambda i, j: (0, j)),
                  pl.BlockSpec((1, tn), lambda i, j: (0, j))],
        out_specs=pl.BlockSpec((tm, tn), lambda i, j: (i, j)),
        out_shape=jax.ShapeDtypeStruct((m, d_ff), BF16),
        scratch_shapes=[pltpu.VMEM((UP_HALO + tm, tn), F32), pltpu.VMEM((nj, UP_HALO, tn), F32)],
        compiler_params=_cparams(("arbitrary", "arbitrary")))(h, w_up, w_up, prev, cw, cb)


GLU_HALO = 32
GLU_TM = 512


def _gluconv_body(h_ref, wa_ref, wb_ref, prev_ref, cw_ref, cb_ref, g_ref, b_ref, o_ref,
                  win, cv, *, tm, c, tpb, kc):
    halo = GLU_HALO

    @pl.when(pl.program_id(0) % tpb == 0)
    def _():
        win[0:halo, :] = prev_ref[0]

    base = halo - (kc - 1)
    rm = min(tm, 256)
    rc = min(rm, 64)
    cc = min(c, LANES)
    for m0 in range(0, tm, rm):
        h = h_ref[m0:m0 + rm, :]
        a = jnp.dot(h, wa_ref[...], preferred_element_type=F32)
        b = jnp.dot(h, wb_ref[...], preferred_element_type=F32)
        win[halo + m0:halo + m0 + rm, :] = a * jax.nn.sigmoid(b)
        for r0 in range(m0, m0 + rm, rc):
            for c0 in range(0, c, cc):
                acc = jnp.broadcast_to(cb_ref[:, c0:c0 + cc], (rc, cc))
                wlen = rc + halo
                window = win[r0:r0 + wlen, c0:c0 + cc]
                for r in range(8):
                    taps = [j for j in range(kc) if (base + j) % 8 == r]
                    if not taps:
                        continue
                    sh = window if r == 0 else pltpu.roll(window, wlen - r, 0)
                    for j in taps:
                        t8 = (base + j) // 8 * 8
                        acc = acc + cw_ref[j:j + 1, c0:c0 + cc] * sh[t8:t8 + rc]
                cv[r0:r0 + rc, c0:c0 + cc] = acc
        y = _epi_conv_a(cv[m0:m0 + rm, :], (), (g_ref[...], b_ref[...]))
        o_ref[m0:m0 + rm, :] = y.astype(o_ref.dtype)
    win[0:halo, :] = win[tm:tm + halo, :]


def _gluconv(h, w_in, li, prev, cw, cb, ln_g, ln_b, *, c, rows_per_batch):
    m, d = h.shape
    kc = cw.shape[0]
    tm = _pick(rows_per_batch, (GLU_TM, 256, 128))
    tpb = rows_per_batch // tm
    assert m % tm == 0 and kc - 1 <= GLU_HALO <= tm and c % LANES == 0
    full = lambda a: pl.BlockSpec(a.shape, lambda i: (0,) * a.ndim)
    body = functools.partial(_gluconv_body, tm=tm, c=c, tpb=tpb, kc=kc)
    return pl.pallas_call(
        body, grid=(m // tm,),
        in_specs=[pl.BlockSpec((tm, d), lambda i: (i, 0)),
                  pl.BlockSpec((None, d, c), lambda i: (li, 0, 0)),
                  pl.BlockSpec((None, d, c), lambda i: (li, 0, 1)),
                  pl.BlockSpec((1, GLU_HALO, c), lambda i: (i // tpb, 0, 0)),
                  full(cw), full(cb), full(ln_g), full(ln_b)],
        out_specs=pl.BlockSpec((tm, c), lambda i: (i, 0)),
        out_shape=jax.ShapeDtypeStruct((m, c), BF16),
        scratch_shapes=[pltpu.VMEM((GLU_HALO + tm, c), F32), pltpu.VMEM((tm, c), F32)],
        compiler_params=_cparams(("arbitrary",)))(h, w_in, w_in, prev, cw, cb, ln_g, ln_b)


def _conv_state(prev, u, k1):
    l = u.shape[1]
    if l >= k1:
        return u[:, l - k1:]
    return jnp.concatenate([prev[:, l:], u], axis=1)


def _pad_prev(prev, halo):
    return jnp.pad(prev, ((0, 0), (halo - prev.shape[1], 0), (0, 0)))


def _t5_bucket(rel):
    n = jnp.maximum(rel, 0)
    max_exact = NUM_BUCKETS // 2
    nf = jnp.maximum(n, 1).astype(F32)
    large = max_exact + (jnp.log(nf / max_exact) / math.log(MAX_DISTANCE / max_exact)
                         * (NUM_BUCKETS - max_exact)).astype(jnp.int32)
    large = jnp.minimum(large, NUM_BUCKETS - 1)
    return jnp.where(n < max_exact, n, large)


def _bias_of_rel(t5_bias, rel):
    bucket = _t5_bucket(rel)
    table = t5_bias.astype(F32)
    expand = (slice(None),) + (None,) * rel.ndim
    bias = jnp.zeros((table.shape[1],) + rel.shape, F32)
    for b in range(NUM_BUCKETS):
        bias = jnp.where((bucket == b)[None], table[b][expand], bias)
    return jnp.where((rel >= 0)[None], bias, NEG_INF)


def _lambda_full(lq1, lk1, lq2, lk2, li, lam_init):
    a = jnp.sum(lq1[li:li + 1, :] * lk1[li:li + 1, :], axis=-1, keepdims=True)
    b = jnp.sum(lq2[li:li + 1, :] * lk2[li:li + 1, :], axis=-1, keepdims=True)
    return jnp.exp(a) - jnp.exp(b) + lam_init


def _softmax_step(s, v, m_sc, l_sc, acc_sc):
    m_prev = m_sc[...]
    m_new = jnp.maximum(m_prev, jnp.max(s, axis=-1, keepdims=True))
    alpha = jnp.exp(m_prev - m_new)
    p = jnp.exp(s - m_new)
    l_sc[...] = alpha * l_sc[...] + jnp.sum(p, axis=-1, keepdims=True)
    acc_sc[...] = alpha * acc_sc[...] + jnp.dot(p.astype(BF16), v, preferred_element_type=F32)
    m_sc[...] = m_new


def _sub_rms(o, g, lam_init):
    y = o * lax.rsqrt(jnp.mean(o * o, axis=-1, keepdims=True) + EPS) * g
    return y * (1.0 - lam_init)


def _pattn_body(q_ref, k_ref, v_ref, bt_ref, lq1, lk1, lq2, lk2, g_ref, o_ref,
                vp_sc, m_sc, acc_sc, *, li, tq, nh, lam_init):
    i = pl.program_id(2)

    @pl.when(i == 0)
    def _():
        for a in range(nh):
            vp_sc[a, :, :KD] = v_ref[0, :, a * KD:(a + 1) * KD]
            vp_sc[a, :, KD:] = jnp.ones((vp_sc.shape[1], KD), BF16)

    m_sc[...] = jnp.full(m_sc.shape, NEG_INF, F32)
    acc_sc[...] = jnp.zeros(acc_sc.shape, F32)
    lane = lax.broadcasted_iota(jnp.int32, (tq, KD), 1)
    qqs = []
    for a in range(nh):
        q = q_ref[0, :, a * KD:(a + 1) * KD]
        zero = jnp.zeros_like(q)
        qqs.append(jnp.concatenate([jnp.where(lane < DIFF_DH, q, zero),
                                    jnp.where(lane >= DIFF_DH, q, zero)], axis=0))

    def block(off, width, bias_of_head):
        for a in range(nh):
            kb = k_ref[0, pl.ds(off, width), a * KD:(a + 1) * KD]
            vb = vp_sc[a, pl.ds(off, width), :]
            s = lax.dot_general(qqs[a], kb, (((1,), (1,)), ((), ())),
                                preferred_element_type=F32) + bias_of_head(a)
            m_prev = m_sc[a]
            m_new = jnp.maximum(m_prev, jnp.max(s, axis=-1, keepdims=True))
            alpha = jnp.exp(m_prev - m_new)
            p = jnp.exp(s - jnp.concatenate([m_new] * (width // LANES), axis=1))
            acc_sc[a] = (jnp.concatenate([alpha, alpha], axis=1) * acc_sc[a]
                         + jnp.dot(p.astype(BF16), vb, preferred_element_type=F32))
            m_sc[a] = m_new

    n_far = jnp.maximum(i - 1, 0)
    far_of_head = lambda a: bt_ref[a, 0, tq - 1:tq, 0:1]

    def far_body(j, carry):
        block(pl.multiple_of(j * 2 * tq, 2 * tq), 2 * tq, far_of_head)
        return carry

    lax.fori_loop(0, n_far // 2, far_body, 0)

    @pl.when(n_far % 2 == 1)
    def _():
        block(pl.multiple_of((n_far - 1) * tq, tq), tq, far_of_head)

    first = (i == 0).astype(jnp.int32)

    def near_of_head(a):
        b = bt_ref[a, first]
        return jnp.concatenate([b, b], axis=0)

    block(pl.multiple_of(n_far * tq, tq), 2 * tq, near_of_head)

    lam = _lambda_full(lq1, lk1, lq2, lk2, li, lam_init)
    g = g_ref[li:li + 1, :]
    for a in range(nh):
        acc = acc_sc[a]
        o = acc[:, :KD] / acc[:, KD:]
        o = o[:tq] - lam * o[tq:]
        o_ref[0, :, a * KD:(a + 1) * KD] = _sub_rms(o, g, lam_init).astype(o_ref.dtype)


def _prompt_attn(qkv, bt, lams, subln_g, *, li, n_heads, lam_init):
    bsz, l, _ = qkv.shape
    tq = bt.shape[2]
    nh = _pick(n_heads, (4, 2, 1))
    hb = n_heads // nh
    assert l % (2 * tq) == 0
    full = lambda a: pl.BlockSpec(a.shape, lambda b, hh, i: (0,) * a.ndim)
    body = functools.partial(_pattn_body, li=li, tq=tq, nh=nh, lam_init=lam_init)
    return pl.pallas_call(
        body, grid=(bsz, hb, l // tq),
        in_specs=[pl.BlockSpec((1, tq, nh * KD), lambda b, hh, i: (b, i, hh)),
                  pl.BlockSpec((1, l, nh * KD), lambda b, hh, i: (b, 0, hb + hh)),
                  pl.BlockSpec((1, l, nh * KD), lambda b, hh, i: (b, 0, 2 * hb + hh)),
                  pl.BlockSpec((nh, 2, tq, 2 * tq), lambda b, hh, i: (hh, 0, 0, 0))]
                 + [full(a) for a in lams] + [full(subln_g)],
        out_specs=pl.BlockSpec((1, tq, nh * KD), lambda b, hh, i: (b, i, hh)),
        out_shape=jax.ShapeDtypeStruct((bsz, l, n_heads * KD), BF16),
        scratch_shapes=[pltpu.VMEM((nh, l, 2 * KD), BF16), pltpu.VMEM((nh, 2 * tq, LANES), F32),
                        pltpu.VMEM((nh, 2 * tq, 2 * KD), F32)],
        compiler_params=_cparams(("parallel", "parallel", "arbitrary")))(
            qkv, qkv, qkv, bt, *lams, subln_g)


def _sattn_body(pt_ref, qt_ref, *refs, li, pp, n_heads, n_q, lam_init):
    krefs = refs[:pp]
    vrefs = refs[pp:2 * pp]
    kn_ref, vn_ref, bias_ref, lq1, lk1, lq2, lk2, g_ref, o_ref, m_sc, l_sc, acc_sc = refs[2 * pp:]
    jp = pl.program_id(1)
    last = pl.num_programs(1) - 1

    @pl.when(jp == 0)
    def _():
        m_sc[...] = jnp.full(m_sc.shape, NEG_INF, F32)
        l_sc[...] = jnp.zeros(l_sc.shape, F32)
        acc_sc[...] = jnp.zeros(acc_sc.shape, F32)

    def block(kbs, vbs, bias):
        s = jnp.stack([lax.dot_general(qt_ref[0, h], kbs[h], (((1,), (1,)), ((), ())),
                                       preferred_element_type=F32)
                       for h in range(n_heads)]) + bias
        m_prev = m_sc[...]
        m_new = jnp.maximum(m_prev, jnp.max(s, axis=-1, keepdims=True))
        alpha = jnp.exp(m_prev - m_new)
        p = jnp.exp(s - m_new)
        l_sc[...] = alpha * l_sc[...] + jnp.sum(p, axis=-1, keepdims=True)
        pb = p.astype(BF16)
        pv = jnp.stack([jnp.dot(pb[h], vbs[h], preferred_element_type=F32)
                        for h in range(n_heads)])
        acc_sc[...] = alpha * acc_sc[...] + pv
        m_sc[...] = m_new

    def head_rows(page_refs, h):
        rows_h = pl.ds(h, PAGE_SIZE, stride=n_heads)
        return jnp.concatenate([r[0, 0, rows_h, :] for r in page_refs], axis=0).astype(BF16)

    far = bias_ref[0]
    near = jnp.where(jp == last, bias_ref[1], far)
    block([head_rows(krefs, h) for h in range(n_heads)],
          [head_rows(vrefs, h) for h in range(n_heads)],
          jnp.concatenate([far] * (pp - 1) + [near], axis=-1))

    @pl.when(jp == last)
    def _():
        cols = [slice(h * KD, (h + 1) * KD) for h in range(n_heads)]
        block([kn_ref[0, :, c] for c in cols], [vn_ref[0, :, c] for c in cols], bias_ref[2])
        lam = _lambda_full(lq1, lk1, lq2, lk2, li, lam_init)
        g = g_ref[li:li + 1, :]
        o = acc_sc[...] / l_sc[...]
        o = _sub_rms(o[:, :n_q] - lam * o[:, n_q:], g, lam_init)
        for h in range(n_heads):
            o_ref[0, :, cols[h]] = o[h]


def _sample_attn(page_table, qt, cache_k, cache_v, k_new, v_new, bias3, lams, subln_g,
                 *, li, n_heads, n_q, lam_init):
    dbsz, n_pages = page_table.shape
    pp = _pick(n_pages, (PAGES_PER_STEP, 8, 4, 2, 1))
    w = n_heads * KD
    pt_flat = page_table.reshape(-1)

    def page_spec(g):
        return pl.BlockSpec((1, 1, PAGE_SIZE * n_heads, KD),
                            lambda b, j, pt, g=g: (li, pt[b * n_pages + j * pp + g], 0, 0))

    full = lambda a: pl.BlockSpec(a.shape, lambda b, j, pt: (0,) * a.ndim)
    per_b = pl.BlockSpec((1, LANES, w), lambda b, j, pt: (b, 0, 0))
    grid_spec = pltpu.PrefetchScalarGridSpec(
        num_scalar_prefetch=1, grid=(dbsz, n_pages // pp),
        in_specs=[pl.BlockSpec((1, n_heads, 2 * n_q, KD), lambda b, j, pt: (b, 0, 0, 0))]
                 + [page_spec(g) for g in range(pp)] * 2 + [per_b, per_b, full(bias3)]
                 + [full(a) for a in lams] + [full(subln_g)],
        out_specs=pl.BlockSpec((1, n_q, w), lambda b, j, pt: (b, 0, 0)),
        scratch_shapes=[pltpu.VMEM((n_heads, 2 * n_q, 1), F32),
                        pltpu.VMEM((n_heads, 2 * n_q, 1), F32),
                        pltpu.VMEM((n_heads, 2 * n_q, KD), F32)])
    body = functools.partial(_sattn_body, li=li, pp=pp, n_heads=n_heads, n_q=n_q,
                             lam_init=lam_init)
    return pl.pallas_call(
        body, grid_spec=grid_spec,
        out_shape=jax.ShapeDtypeStruct((dbsz, n_q, w), F32),
        compiler_params=_cparams(("parallel", "arbitrary")))(
            pt_flat, qt, *([cache_k] * pp), *([cache_v] * pp), k_new, v_new, bias3,
            *lams, subln_g)


def _block_queries(q, n_heads):
    dbsz, n_q, _ = q.shape
    qh = jnp.transpose(q.reshape(dbsz, n_q, n_heads, KD), (0, 2, 1, 3))
    col_map = (jnp.arange(KD) // DIFF_DH)[None, :] == jnp.arange(2)[:, None]
    t = qh[:, :, None] * col_map[None, None, :, None, :].astype(q.dtype)
    return t.reshape(dbsz, n_heads, 2 * n_q, KD)


def _sample_bias(t5_bias, n_heads, n_q):
    qi = jnp.arange(n_q, dtype=jnp.int32)
    key = jnp.arange(LANES, dtype=jnp.int32)

    def tile(rel):
        b = _bias_of_rel(t5_bias, rel)
        return jnp.concatenate([b, b], axis=1)

    far = tile(jnp.full((n_q, LANES), MAX_DISTANCE, jnp.int32))
    lastp = tile(PAGE_SIZE + qi[:, None] - key[None, :])
    new = tile(jnp.where(key[None, :] < n_q, qi[:, None] - key[None, :], -1))
    return jnp.stack([far, lastp, new])


def _ssd_body(xbc_ref, z_ref, dt_ref, h0_ref, dtb_ref, alog_ref, dx_ref, gn_ref,
              y_ref, hout_ref, hs, ysc, *, n_heads, valid_len):
    c = pl.program_id(1)
    t = SSD_CHUNK
    di = n_heads * SSD_P
    e_per_g = n_heads // SSD_G

    @pl.when(c == 0)
    def _():
        hs[...] = h0_ref[0]

    xbc = xbc_ref[0]
    xs = xbc[:, :di]
    row = lax.broadcasted_iota(jnp.int32, (t, LANES), 0)
    lane = lax.broadcasted_iota(jnp.int32, (t, LANES), 1)
    raw = dt_ref[0] + dtb_ref[...]
    dt = jnp.maximum(raw, 0.0) + jnp.log1p(jnp.exp(-jnp.abs(raw)))
    dt = jnp.where((row + c * t < valid_len) & (lane < n_heads), dt, 0.0)
    da = dt * (-jnp.exp(alog_ref[...]))
    tri = (lax.broadcasted_iota(jnp.int32, (t, t), 0)
           >= lax.broadcasted_iota(jnp.int32, (t, t), 1))
    cum = jnp.dot(tri.astype(F32), da, precision=lax.Precision.HIGHEST,
                  preferred_element_type=F32)
    cum_t = cum.T
    dt_t = dt.T
    cum_last = cum[t - 1:t, :]
    w_end = dt * jnp.exp(cum_last - cum)
    e_cum = jnp.exp(cum)
    e_last = jnp.exp(cum_last)
    xs_bf = xs.astype(BF16)
    xs_t = xs.T.astype(BF16)

    for g in range(SSD_G):
        b_g = xbc[:, di + g * SSD_N:di + (g + 1) * SSD_N]
        c_g = xbc[:, di + (SSD_G + g) * SSD_N:di + (SSD_G + g + 1) * SSD_N]
        cb = lax.dot_general(c_g.astype(BF16), b_g.astype(BF16), (((1,), (1,)), ((), ())),
                             preferred_element_type=F32)
        for e in range(e_per_g):
            h = g * e_per_g + e
            seg = cum[:, h:h + 1] - cum_t[h:h + 1, :]
            decay = jnp.where(tri, jnp.exp(jnp.where(tri, seg, 0.0)), 0.0)
            mat = (cb * decay * dt_t[h:h + 1, :]).astype(BF16)
            x_e = xs_bf[:, h * SSD_P:(h + 1) * SSD_P]
            y_d = jnp.dot(mat, x_e, preferred_element_type=F32)
            h_e = hs[h]
            c_w = (c_g * e_cum[:, h:h + 1]).astype(BF16)
            y_o = lax.dot_general(c_w, h_e.astype(BF16), (((1,), (1,)), ((), ())),
                                  preferred_element_type=F32)
            b_w = (b_g * w_end[:, h:h + 1]).astype(BF16)
            st = jnp.dot(xs_t[h * SSD_P:(h + 1) * SSD_P, :], b_w,
                         preferred_element_type=F32)
            hs[h] = h_e * e_last[:, h:h + 1] + st
            ysc[:, h * SSD_P:(h + 1) * SSD_P] = y_d + y_o

    y = ysc[...] + xs * dx_ref[...]
    z = z_ref[0]
    y = y * (z * jax.nn.sigmoid(z))
    gw = di // SSD_G
    for g in range(SSD_G):
        seg = y[:, g * gw:(g + 1) * gw]
        seg = seg * lax.rsqrt(jnp.mean(seg * seg, axis=-1, keepdims=True) + EPS)
        y_ref[0, :, g * gw:(g + 1) * gw] = (seg * gn_ref[:, g * gw:(g + 1) * gw]).astype(y_ref.dtype)

    @pl.when(c == pl.num_programs(1) - 1)
    def _():
        hout_ref[0] = hs[...]


def _ssd(xbc_act, zx, dt_raw, h0, dtb, alog, dx, gn, *, n_heads, valid_len):
    bsz, l, xbc_w = xbc_act.shape
    di = n_heads * SSD_P
    t = SSD_CHUNK
    assert l % t == 0
    full = lambda a: pl.BlockSpec(a.shape, lambda b, c: (0,) * a.ndim)
    body = functools.partial(_ssd_body, n_heads=n_heads, valid_len=valid_len)
    return pl.pallas_call(
        body, grid=(bsz, l // t),
        in_specs=[pl.BlockSpec((1, t, xbc_w), lambda b, c: (b, c, 0)),
                  pl.BlockSpec((1, t, di), lambda b, c: (b, c, 0)),
                  pl.BlockSpec((1, t, LANES), lambda b, c: (b, c, 0)),
                  pl.BlockSpec((1, n_heads, SSD_P, SSD_N), lambda b, c: (b, 0, 0, 0)),
                  full(dtb), full(alog), full(dx), full(gn)],
        out_specs=[pl.BlockSpec((1, t, di), lambda b, c: (b, c, 0)),
                   pl.BlockSpec((1, n_heads, SSD_P, SSD_N), lambda b, c: (b, 0, 0, 0))],
        out_shape=[jax.ShapeDtypeStruct((bsz, l, di), BF16),
                   jax.ShapeDtypeStruct((bsz, n_heads, SSD_P, SSD_N), F32)],
        scratch_shapes=[pltpu.VMEM((n_heads, SSD_P, SSD_N), F32), pltpu.VMEM((t, di), F32)],
        compiler_params=_cparams(("parallel", "arbitrary")))(
            xbc_act, zx, dt_raw, h0, dtb, alog, dx, gn)


def _epi_glu(accs, tiles, rows, j):
    a, b = accs
    return (a * jax.nn.sigmoid(b),)


def _epi_id(accs, tiles, rows, j):
    return (accs[0],)


def _epi_sigmoid(accs, tiles, rows, j):
    return (jax.nn.sigmoid(accs[0]),)


def _epi_merge(accs, tiles, rows, j):
    ya, yb, yc = accs
    ga, gb, gc = (t.astype(F32) for t in tiles)
    return (ga * ya + gb * yb + gc * yc,)


def _epi_resid(accs, tiles, rows, j):
    return (tiles[0] + rows[0] * accs[0],)


def _trunk(x, mod, prev_a, prev_c, prev_h, prev_f, wts, p, dims, *, paged=None):
    bsz, l, d = x.shape
    m = bsz * l
    depth, n_heads, conv_c, ssd_h, d_ff = (dims[k] for k in
                                           ("depth", "n_heads", "conv_c", "ssd_h", "d_ff"))
    att_w = n_heads * KD
    ssd_di = ssd_h * SSD_P
    ssd_xbc = ssd_di + 2 * SSD_G * SSD_N
    conv_k = p["conv_a_w"].shape[1]
    ssd_k = p["ssd_conv_w"].shape[1]
    ffn_k = p["ffn_conv_w"].shape[1]
    prompt = paged is None
    tm = _pick(l, (512, 256, 128)) if prompt else m
    tpb = l // tm if prompt else 1
    rpb = l if prompt else None
    t_seq = _pick(l, (128,)) if prompt else l
    lams = (p["lambda_q1"], p["lambda_k1"], p["lambda_q2"], p["lambda_k2"])
    scale = DIFF_DH ** -0.5

    xf = x.reshape(m, d)
    out_a, out_c, out_h, out_f = [], [], [], []
    kst = jnp.zeros((depth, m, att_w), F32)
    vst = jnp.zeros((depth, m, att_w), F32)
    for li in range(depth):
        w = wts
        md = mod[li]
        lam_init = 0.8 - 0.6 * math.exp(-0.3 * li)
        h = _norm_mod(xf, p["norm1_g"][li][None], md, 1, 0, tm=tm, tpb=tpb)

        c_qkv = 2 * conv_c
        c_zx = c_qkv + 3 * att_w
        zx_w = ssd_di + ssd_xbc
        glu_w = [(w["in"], li, 0, 0), (w["in"], li, conv_c, 0)]
        qkv_bf, kst, vst = _qkv(h, w["in"], c_qkv, kst, vst, li=li, scale=scale,
                                rows_per_batch=rpb)
        (zx,) = _mm([h], [(w["in"], li, c_zx, 0)], _epi_id, [F32], zx_w, rows_per_batch=rpb)
        (dt_raw,) = _mm([h], [(w["dt"], li, 0, 0)], _epi_id, [F32], LANES, rows_per_batch=rpb)
        (gates,) = _mm([h], [(w["gates"], li, 0, 0)], _epi_sigmoid, [BF16], 3 * d,
                       rows_per_batch=rpb)

        ln_rows = (p["ln_a_g"][li][None], p["ln_a_b"][li][None])
        if prompt and l >= GLU_HALO:
            a_act = _gluconv(h, w["in"], li, _pad_prev(prev_a[li], GLU_HALO), p["conv_a_w"][li],
                             p["conv_a_b"][li][None], *ln_rows, c=conv_c, rows_per_batch=rpb)
            h_tail = h.reshape(bsz, l, d)[:, l - GLU_HALO:].reshape(bsz * GLU_HALO, d)
            (u_tail,) = _mm([h_tail], glu_w, _epi_glu, [F32], conv_c)
            out_a.append(u_tail.reshape(bsz, GLU_HALO, conv_c)[:, GLU_HALO - (conv_k - 1):])
        else:
            (u,) = _mm([h], glu_w, _epi_glu, [F32], conv_c, rows_per_batch=rpb)
            u3 = u.reshape(bsz, l, conv_c)
            out_a.append(_conv_state(prev_a[li], u3, conv_k - 1))
            a_act = _dwconv(u3, 0, _pad_prev(prev_a[li], GLU_HALO), p["conv_a_w"][li],
                            p["conv_a_b"][li][None], _epi_conv_a, BF16, c=conv_c, t=t_seq,
                            tc=conv_c, halo=GLU_HALO, prows=ln_rows)

        if prompt:
            o = _prompt_attn(qkv_bf.reshape(bsz, l, 3 * att_w), dims["bias_tiles"], lams,
                             p["subln_g"], li=li, n_heads=n_heads, lam_init=lam_init)
        else:
            qb = qkv_bf.reshape(bsz, l, 3 * att_w)
            qt = _block_queries(qb[..., :att_w], n_heads)
            pad = ((0, 0), (0, LANES - l), (0, 0))
            k_new = jnp.pad(qb[..., att_w:2 * att_w], pad)
            v_new = jnp.pad(qb[..., 2 * att_w:], pad)
            o = _sample_attn(paged["page_table"], qt, paged["cache_k"], paged["cache_v"],
                             k_new, v_new, dims["bias3"], lams, p["subln_g"], li=li,
                             n_heads=n_heads, n_q=l, lam_init=lam_init).astype(BF16)
        o = o.reshape(m, att_w)

        zx3 = zx.reshape(bsz, l, zx_w)
        dt3 = dt_raw.reshape(bsz, l, LANES)
        xbc_raw = zx3[..., ssd_di:]
        out_c.append(_conv_state(prev_c[li], xbc_raw, ssd_k - 1))
        tc = _pick(math.gcd(ssd_di, ssd_xbc), (512, 256, 128))
        xbc_act = _dwconv(zx3, ssd_di // tc, _pad_prev(prev_c[li], 8), p["ssd_conv_w"][li],
                          p["ssd_conv_b"][li][None], _epi_silu, F32, c=ssd_xbc,
                          t=_pick(l, (512, 256, 128)) if prompt else l, tc=tc, halo=8)
        if prompt:
            xbc_in, zx_in, dt_in = xbc_act, zx3, dt3
        else:
            pad = ((0, 0), (0, SSD_CHUNK - l), (0, 0))
            xbc_in, zx_in, dt_in = jnp.pad(xbc_act, pad), jnp.pad(zx3, pad), jnp.pad(dt3, pad)
        lane_pad = lambda a: jnp.pad(a, (0, LANES - a.shape[0]))[None]
        y_c, h_t = _ssd(xbc_in, zx_in, dt_in, prev_h[li],
                        lane_pad(p["dt_bias"][li]), lane_pad(p["a_log"][li]),
                        jnp.repeat(p["d_skip"][li], SSD_P)[None], p["ssd_norm_g"][li][None],
                        n_heads=ssd_h, valid_len=l)
        out_h.append(h_t)
        y_c = y_c[:, :l].reshape(m, ssd_di)

        (merged,) = _mm([a_act.reshape(m, conv_c), o, y_c],
                        [(w["proj_a"], li, 0, 0), (w["proj_b"], li, 0, 1), (w["proj_c"], li, 0, 2)],
                        _epi_merge, [BF16], d, tiles=[(gates, 0), (gates, d), (gates, 2 * d)],
                        rows_per_batch=rpb)
        (xf,) = _mm([merged], [(w["out"], li, 0, 0)], _epi_resid, [F32], d, tiles=[(xf, 0)],
                    rows=[(md, 2 * d)], rows_per_batch=rpb)

        h2 = _norm_mod(xf, p["norm2_g"][li][None], md, 4, 3, tm=tm, tpb=tpb)
        if prompt:
            act = _upconv(h2, w["up"], li, _pad_prev(prev_f[li], UP_HALO), p["ffn_conv_w"][li],
                          p["ffn_conv_b"][li][None], d_ff=d_ff, rows_per_batch=rpb)
            assert l >= UP_HALO
            h2_tail = h2.reshape(bsz, l, d)[:, l - UP_HALO:].reshape(bsz * UP_HALO, d)
            (fa_tail,) = _mm([h2_tail], [(w["up"], li, 0, 0)], _epi_id, [F32], d_ff)
            out_f.append(fa_tail.reshape(bsz, UP_HALO, d_ff)[:, UP_HALO - (ffn_k - 1):])
        else:
            (up,) = _mm([h2], [(w["up"], li, 0, 0)], _epi_id, [F32], 2 * d_ff)
            up3 = up.reshape(bsz, l, 2 * d_ff)
            out_f.append(_conv_state(prev_f[li], up3[..., :d_ff], ffn_k - 1))
            tc = d_ff
            act = _dwconv(up3, 0, _pad_prev(prev_f[li], 8), p["ffn_conv_w"][li],
                          p["ffn_conv_b"][li][None], _epi_ffn, BF16, c=d_ff, t=t_seq, tc=tc,
                          halo=8, xtiles=[(up3, d_ff // tc)])
        (xf,) = _mm([act.reshape(m, d_ff)], [(w["down"], li, 0, 0)], _epi_resid, [F32], d,
                    tiles=[(xf, 0)], rows=[(md, 5 * d)], rows_per_batch=rpb)

    y_out = _rms(xf, p["final_g"][None], tm=tm).reshape(bsz, l, d)
    kv_shape = (depth, bsz, l, n_heads, KD)
    return (y_out, kst.reshape(kv_shape), vst.reshape(kv_shape), jnp.stack(out_a), jnp.stack(out_c),
            jnp.stack(out_h), jnp.stack(out_f))


def _bf16_weights(p, dims):
    conv_c, att_w = dims["conv_c"], dims["n_heads"] * KD
    ssd_di = dims["ssd_h"] * SSD_P
    ssd_xbc = ssd_di + 2 * SSD_G * SSD_N
    c_dt = 2 * conv_c + 3 * att_w + ssd_di + ssd_xbc
    c_gates = c_dt + dims["ssd_h"]
    w_in = p["w_in"].astype(BF16)
    dt_cols = jnp.pad(w_in[:, :, c_dt:c_gates], ((0, 0), (0, 0), (0, LANES - dims["ssd_h"])))
    bf = lambda a: a.astype(BF16)
    return {"in": w_in, "dt": dt_cols, "gates": w_in[:, :, c_gates:],
            "proj_a": bf(p["w_proj_a"]), "proj_b": bf(p["w_proj_b"]), "proj_c": bf(p["w_proj_c"]),
            "out": bf(p["w_out"]), "up": bf(p["ffn_up"]), "down": bf(p["ffn_down"])}


def kernel(x_prompt, x_sample, cache_k, cache_v, page_table, state_conv_a, state_ssd_conv,
           state_ssm, state_ffn_conv, c_prompt, c_sample, w_ada, b_ada, norm1_g, w_in,
           conv_a_w, conv_a_b, ln_a_g, ln_a_b, w_proj_a, t5_bias, lambda_q1, lambda_k1,
           lambda_q2, lambda_k2, subln_g, w_proj_b, ssd_conv_w, ssd_conv_b, dt_bias, a_log,
           d_skip, ssd_norm_g, w_proj_c, w_out, norm2_g, ffn_up, ffn_conv_w, ffn_conv_b,
           ffn_down, final_g):
    p = dict(w_ada=w_ada, b_ada=b_ada, norm1_g=norm1_g, w_in=w_in, conv_a_w=conv_a_w,
             conv_a_b=conv_a_b, ln_a_g=ln_a_g, ln_a_b=ln_a_b, w_proj_a=w_proj_a, t5_bias=t5_bias,
             lambda_q1=lambda_q1, lambda_k1=lambda_k1, lambda_q2=lambda_q2, lambda_k2=lambda_k2,
             subln_g=subln_g, w_proj_b=w_proj_b, ssd_conv_w=ssd_conv_w, ssd_conv_b=ssd_conv_b,
             dt_bias=dt_bias, a_log=a_log, d_skip=d_skip, ssd_norm_g=ssd_norm_g, w_proj_c=w_proj_c,
             w_out=w_out, norm2_g=norm2_g, ffn_up=ffn_up, ffn_conv_w=ffn_conv_w,
             ffn_conv_b=ffn_conv_b, ffn_down=ffn_down, final_g=final_g)
    bp, lp, d = x_prompt.shape
    bs, ls, _ = x_sample.shape
    depth = w_ada.shape[0]
    n_heads = cache_k.shape[3]
    n_pages = page_table.shape[1]
    dims = dict(depth=depth, n_heads=n_heads, conv_c=conv_a_w.shape[2], ssd_h=dt_bias.shape[1],
                d_ff=ffn_conv_w.shape[2])
    f32 = x_prompt.dtype

    tq = _pick(lp, (ATTN_TQ, 128))
    assert tq >= MAX_DISTANCE
    qi = jnp.arange(tq, dtype=jnp.int32)
    rel0 = qi[:, None] - qi[None, :]
    b_diag, b_prev = _bias_of_rel(t5_bias, rel0), _bias_of_rel(t5_bias, rel0 + tq)
    dims["bias_tiles"] = jnp.stack(
        [jnp.concatenate([b_prev, b_diag], axis=-1),
         jnp.concatenate([b_diag, jnp.full_like(b_diag, NEG_INF)], axis=-1)], axis=1)
    dims["bias3"] = _sample_bias(t5_bias, n_heads, ls)

    rows = -(-(bp + bs) // 8) * 8
    c_all = jnp.pad(jnp.concatenate([c_prompt, c_sample], axis=0), ((0, rows - bp - bs), (0, 0)))
    mod = _ada(c_all, w_ada, b_ada)
    mod_p = mod[:, :bp, None, :]
    mod_s = jnp.repeat(mod[:, bp:bp + bs], ls, axis=1)[:, None]

    wts = _bf16_weights(p, dims)
    zero = lambda *s: jnp.zeros((depth, bp) + s, f32)
    res_p = _trunk(x_prompt, mod_p, zero(conv_a_w.shape[1] - 1, dims["conv_c"]),
                   zero(ssd_conv_w.shape[1] - 1, ssd_conv_w.shape[2]),
                   zero(dims["ssd_h"], SSD_P, SSD_N), zero(ffn_conv_w.shape[1] - 1, dims["d_ff"]),
                   wts, p, dims)
    paged = dict(page_table=page_table,
                 cache_k=cache_k.reshape(depth, -1, PAGE_SIZE * n_heads, KD),
                 cache_v=cache_v.reshape(depth, -1, PAGE_SIZE * n_heads, KD))
    res_s = _trunk(x_sample, mod_s, state_conv_a, state_ssd_conv, state_ssm, state_ffn_conv,
                   wts, p, dims, paged=paged)
    y_p, k_p, v_p, a_p, c_p, h_p, f_p = res_p
    y_s, k_s, v_s, a_s, c_s, h_s, f_s = res_s
    return (y_p, y_s, k_p, v_p, k_s, v_s, a_p, a_s, c_p, c_s, h_p, h_s, f_p, f_s)
```
